```python
import math
import jax
import jax.numpy as jnp
from jax import lax
import numpy as np

D_MODEL = 1024
BATCH = 8
SEQ = 4096
DEPTH = 2
DEC_BATCH = 32
DEC_SEQ = 8
PAST_LEN = 16384
PAGE_SIZE = 128

HEAD_DIM = 64
NSA_HEADS = 12
NSA_KV_HEADS = 2
NSA_GROUP = NSA_HEADS // NSA_KV_HEADS
NSA_WIDTH = NSA_HEADS * HEAD_DIM
NSA_KV_WIDTH = NSA_KV_HEADS * HEAD_DIM
CMP_BLOCK = 64
CMP_HIDDEN = 128
SLC_BLOCK = CMP_BLOCK
SLC_TOPN = 16
WINDOW = 512
Q_CHUNK = 64
FORCED_SCORE = 1e4
XATTN_HEADS = 4
XATTN_WIDTH = XATTN_HEADS * HEAD_DIM
N_MEM = 256
POOL_WINDOWS = (2, 4, 8, 16)
POOL_GROUPS = len(POOL_WINDOWS)
POOL_GROUP_DIM = 192
POOL_WIDTH = POOL_GROUPS * POOL_GROUP_DIM
POOL_STATE = max(POOL_WINDOWS) - 1
MIX_WIDTH = NSA_WIDTH + XATTN_WIDTH
NSA_IN_WIDTH = NSA_WIDTH + 6 * NSA_KV_WIDTH + 3 * NSA_HEADS + XATTN_WIDTH
POOL_IN_WIDTH = POOL_WIDTH + XATTN_WIDTH
D_FF = 2816
CONV_WIDTH = 3
REL_BUCKETS = 32
REL_MAX_DIST = 128
N_NSA_LAYERS = (DEPTH + 1) // 2
N_POOL_LAYERS = DEPTH // 2
EPS = 1e-6
NEG_INF = -1e30

kernel_name = 'nsa_pool_hybrid_decode_step'


def rmsnorm(x, g):
    xf = x.astype(jnp.float32)
    y = xf * lax.rsqrt(jnp.mean(xf * xf, axis=-1, keepdims=True) + EPS)
    return (y * g.astype(jnp.float32)).astype(x.dtype)


def rel_bucket(dist):
    n = jnp.maximum(dist, 0)
    max_exact = REL_BUCKETS // 2
    nf = jnp.maximum(n, 1).astype(jnp.float32)
    large = max_exact + (jnp.log(nf / max_exact) / math.log(REL_MAX_DIST / max_exact)
                         * (REL_BUCKETS - max_exact)).astype(jnp.int32)
    large = jnp.minimum(large, REL_BUCKETS - 1)
    return jnp.where(n < max_exact, n, large)


def masked_softmax(logits, mask):
    p = jax.nn.softmax(jnp.where(mask, logits, NEG_INF), axis=-1)
    return jnp.where(mask, p, 0.0)


def compress(rows, pos, w1, b1, w2, b2):
    b, lc, g, dh = rows.shape
    blk = rows.reshape(b, lc // CMP_BLOCK, CMP_BLOCK, g, dh) + pos[None, None, :, None, :]
    h = jax.nn.gelu(jnp.einsum('bclgd,ldh->bcgh', blk, w1) + b1)
    return jnp.einsum('bcgh,hd->bcgd', h, w2) + b2


def nsa_sources(full, k_gain, cmp_params):
    cpos, cw1, cb1, cw2, cb2 = cmp_params
    b, l, _, g, dh = full.shape
    nc = l // CMP_BLOCK
    rc = full[:, :nc * CMP_BLOCK]
    kc = rmsnorm(compress(rc[:, :, 0], cpos[0], cw1[0], cb1[0], cw2[0], cb2[0]), k_gain)
    vc = compress(rc[:, :, 1], cpos[1], cw1[1], cb1[1], cw2[1], cb2[1])
    c_end = jnp.arange(nc, dtype=jnp.int32) * CMP_BLOCK + (CMP_BLOCK - 1)
    nb = -(-l // SLC_BLOCK)
    sel = jnp.pad(full[:, :, 2:4], ((0, 0), (0, nb * SLC_BLOCK - l), (0, 0), (0, 0), (0, 0)))
    sel = sel.reshape(b, nb, SLC_BLOCK, 2, g, dh)
    return kc, vc, c_end, sel[:, :, :, 0], sel[:, :, :, 1]


def nsa_attend(q, q_pos, gates, kc, vc, c_end, ks_blk, vs_blk, kw, vw, kw_pos, rel_bias):
    b, nq, g, r, dh = q.shape
    scale = dh ** -0.5
    nb = ks_blk.shape[1]
    nc = kc.shape[1]
    d_c = q_pos[:, None] - c_end[None, :]
    bias_c = rel_bias[rel_bucket(d_c)].transpose(2, 0, 1).reshape(g, r, nq, nc)
    l_c = jnp.einsum('bqgrd,bcgd->bgrqc', q, kc).astype(jnp.float32) * scale + bias_c
    p_c = masked_softmax(l_c, d_c >= 0)
    o_c = jnp.einsum('bgrqc,bcgd->bqgrd', p_c.astype(vc.dtype), vc)
    blk = jnp.arange(nb, dtype=jnp.int32)[None, :]
    cur = (q_pos // SLC_BLOCK)[:, None]
    valid = blk * SLC_BLOCK <= q_pos[:, None]
    forced = (blk == 0) | (blk == cur) | (blk == cur - 1)
    imp = jnp.pad(p_c.sum(axis=2), ((0, 0), (0, 0), (0, 0), (0, nb - nc)))
    score = jnp.where(valid, jnp.where(forced, FORCED_SCORE, imp), -1.0)
    top_val, top_idx = lax.top_k(score, min(SLC_TOPN, nb))
    n_sel = top_idx.shape[-1]
    bi = jnp.arange(b)[:, None, None, None]
    gi = jnp.arange(g)[None, :, None, None]
    k_sel = ks_blk.transpose(0, 3, 1, 2, 4)[bi, gi, top_idx]
    v_sel = vs_blk.transpose(0, 3, 1, 2, 4)[bi, gi, top_idx]
    pos_sel = top_idx[..., None] * SLC_BLOCK + jnp.arange(SLC_BLOCK, dtype=jnp.int32)
    d_s = q_pos[None, None, :, None, None] - pos_sel
    m_s = ((d_s >= 0) & (top_val >= 0.0)[..., None]).reshape(b, g, 1, nq, n_sel * SLC_BLOCK)
    rb = rel_bias.reshape(REL_BUCKETS, g, r)
    bias_s = jnp.moveaxis(rb[rel_bucket(d_s), gi[..., None]], -1, 2).reshape(b, g, r, nq, n_sel * SLC_BLOCK)
    l_s = jnp.einsum('bqgrd,bgqkld->bgrqkl', q, k_sel).astype(jnp.float32)
    l_s = l_s.reshape(b, g, r, nq, n_sel * SLC_BLOCK) * scale + bias_s
    p_s = masked_softmax(l_s, m_s)
    o_s = jnp.einsum('bgrqn,bgqnd->bqgrd', p_s.astype(v_sel.dtype),
                     v_sel.reshape(b, g, nq, n_sel * SLC_BLOCK, dh))
    d_w = q_pos[:, None] - kw_pos[None, :]
    m_w = (d_w >= 0) & (d_w <= WINDOW) & (kw_pos >= 0)[None, :]
    bias_w = rel_bias[rel_bucket(d_w)].transpose(2, 0, 1).reshape(g, r, nq, kw_pos.shape[0])
    l_w = jnp.einsum('bqgrd,bwgd->bgrqw', q, kw).astype(jnp.float32) * scale + bias_w
    p_w = masked_softmax(l_w, m_w)
    o_w = jnp.einsum('bgrqw,bwgd->bqgrd', p_w.astype(vw.dtype), vw)
    o = gates[..., 0:1] * o_c + gates[..., 1:2] * o_s + gates[..., 2:3] * o_w
    return o.reshape(b, nq, g * r * dh)


def nsa_mix(z, q_gain, k_gain, cmp_params, rel_bias, past_rows, win_buf):
    b, l, _ = z.shape
    o1 = NSA_WIDTH + 6 * NSA_KV_WIDTH
    q = rmsnorm(z[..., :NSA_WIDTH].reshape(b, l, NSA_HEADS, HEAD_DIM), q_gain)
    q = q.reshape(b, l, NSA_KV_HEADS, NSA_GROUP, HEAD_DIM)
    kv = z[..., NSA_WIDTH:o1].reshape(b, l, 6, NSA_KV_HEADS, HEAD_DIM)
    gates = jax.nn.sigmoid(z[..., o1:o1 + 3 * NSA_HEADS]).reshape(b, l, NSA_KV_HEADS, NSA_GROUP, 3)
    k_slc = rmsnorm(kv[:, :, 2], k_gain)
    k_win = rmsnorm(kv[:, :, 4], k_gain)
    new_rows = jnp.stack([kv[:, :, 0], kv[:, :, 1], k_slc, kv[:, :, 3]], axis=2)
    win_rows = jnp.stack([k_win, kv[:, :, 5]], axis=2)
    if past_rows is None:
        pos0 = 0
        full = new_rows
    else:
        pos0 = past_rows.shape[1]
        full = jnp.concatenate([past_rows, new_rows], axis=1)
    kc, vc, c_end, ks_blk, vs_blk = nsa_sources(full, k_gain, cmp_params)
    if win_buf is None:
        win_pad = jnp.pad(win_rows, ((0, 0), (WINDOW, 0), (0, 0), (0, 0), (0, 0)))

        def chunk(i):
            s0 = i * Q_CHUNK
            qc = lax.dynamic_slice_in_dim(q, s0, Q_CHUNK, axis=1)
            gc = lax.dynamic_slice_in_dim(gates, s0, Q_CHUNK, axis=1)
            wc = lax.dynamic_slice_in_dim(win_pad, s0, WINDOW + Q_CHUNK, axis=1)
            q_pos = s0 + jnp.arange(Q_CHUNK, dtype=jnp.int32)
            kw_pos = s0 - WINDOW + jnp.arange(WINDOW + Q_CHUNK, dtype=jnp.int32)
            return nsa_attend(qc, q_pos, gc, kc, vc, c_end, ks_blk, vs_blk,
                              wc[:, :, 0], wc[:, :, 1], kw_pos, rel_bias)

        outs = lax.map(chunk, jnp.arange(l // Q_CHUNK, dtype=jnp.int32))
        out = outs.transpose(1, 0, 2, 3).reshape(b, l, NSA_WIDTH)
        new_win = win_rows[:, -min(WINDOW, l):]
    else:
        wb = win_buf.shape[1]
        wall = jnp.concatenate([win_buf, win_rows], axis=1)
        q_pos = pos0 + jnp.arange(l, dtype=jnp.int32)
        kw_pos = pos0 - wb + jnp.arange(wb + l, dtype=jnp.int32)
        out = nsa_attend(q, q_pos, gates, kc, vc, c_end, ks_blk, vs_blk,
                         wall[:, :, 0], wall[:, :, 1], kw_pos, rel_bias)
        new_win = wall[:, -wb:]
    return out, new_rows, new_win


def pool_mix(u, pos0, prev, w_grp, scale):
    b, l, _ = u.shape
    ext = jnp.concatenate([prev, u], axis=1).astype(jnp.float32)
    cs = jnp.cumsum(jnp.pad(ext, ((0, 0), (1, 0), (0, 0))), axis=1)
    pos = pos0 + jnp.arange(l, dtype=jnp.int32)
    e0 = POOL_STATE + 1
    outs = []
    for gidx, w in enumerate(POOL_WINDOWS):
        sl = slice(gidx * POOL_GROUP_DIM, (gidx + 1) * POOL_GROUP_DIM)
        s = cs[:, e0:e0 + l, sl] - cs[:, e0 - w:e0 - w + l, sl]
        cnt = jnp.minimum(w, pos + 1).astype(jnp.float32)[None, :, None]
        outs.append(s / cnt)
    pooled = (jnp.concatenate(outs, axis=-1) - u.astype(jnp.float32)).astype(u.dtype)
    mixed = jnp.einsum('blgc,gcd->blgd', pooled.reshape(b, l, POOL_GROUPS, POOL_GROUP_DIM), w_grp)
    return mixed.reshape(b, l, POOL_WIDTH) * scale, ext[:, -POOL_STATE:].astype(u.dtype)


def memory_kv(mem, g_mem, w_mem_kv, xk_gain):
    b, m, _ = mem.shape
    kv = (rmsnorm(mem, g_mem) @ w_mem_kv).reshape(b, m, 2, XATTN_HEADS, HEAD_DIM)
    return jnp.stack([rmsnorm(kv[:, :, 0], xk_gain), kv[:, :, 1]], axis=2)


def memory_attend(zq, xq_gain, mkv):
    b, l, _ = zq.shape
    q = rmsnorm(zq.reshape(b, l, XATTN_HEADS, HEAD_DIM), xq_gain)
    logits = jnp.einsum('blhd,bmhd->bhlm', q, mkv[:, :, 0]).astype(jnp.float32) * HEAD_DIM ** -0.5
    p = jax.nn.softmax(logits, axis=-1).astype(mkv.dtype)
    return jnp.einsum('bhlm,bmhd->blhd', p, mkv[:, :, 1]).reshape(b, l, XATTN_WIDTH)


def conv_ffn(h, prev, w_up, conv_w, conv_b, w_down):
    l = h.shape[1]
    u = h @ w_up
    ext = jnp.concatenate([prev, u], axis=1)
    y = conv_b + sum(conv_w[k] * ext[:, k:k + l] for k in range(CONV_WIDTH))
    a, v = jnp.split(y, 2, axis=-1)
    return (jax.nn.silu(a) * v) @ w_down, ext[:, -(CONV_WIDTH - 1):]


def setup_inputs(seed: int = 0) -> dict:
    key = jax.random.key(seed)
    keys = iter(jax.random.split(key, 48))

    def nrm(shape, scale):
        return jax.random.normal(next(keys), shape, jnp.float32) * scale

    n_pages = PAST_LEN // PAGE_SIZE
    n_used = DEC_BATCH * n_pages
    n_phys = n_used + (n_used + 3) // 4
    wb = min(WINDOW, PAST_LEN)
    page_table = jax.random.permutation(next(keys), n_phys)[:n_used].reshape(DEC_BATCH, n_pages).astype(jnp.int32)
    return {
        'x_prompt': nrm((BATCH, SEQ, D_MODEL), 1.0),
        'x_sample': nrm((DEC_BATCH, DEC_SEQ, D_MODEL), 1.0),
        'cache_nsa_kv': nrm((N_NSA_LAYERS, n_phys, PAGE_SIZE, 4, NSA_KV_HEADS, HEAD_DIM), 1.0),
        'cache_nsa_win': nrm((N_NSA_LAYERS, DEC_BATCH, wb, 2, NSA_KV_HEADS, HEAD_DIM), 1.0),
        'cache_mem_kv': nrm((DEPTH, DEC_BATCH, N_MEM, 2, XATTN_HEADS, HEAD_DIM), 1.0),
        'state_pool': nrm((N_POOL_LAYERS, DEC_BATCH, POOL_STATE, POOL_WIDTH), 1.0),
        'state_ffn_conv': nrm((DEPTH, DEC_BATCH, CONV_WIDTH - 1, 2 * D_FF), 1.0),
        'page_table': page_table,
        'mem_prompt': nrm((BATCH, N_MEM, D_MODEL), 1.0),
        'rel_bias': nrm((REL_BUCKETS, NSA_HEADS), 0.5),
        'g_mix': 1.0 + nrm((DEPTH, D_MODEL), 0.05),
        'g_mem': 1.0 + nrm((DEPTH, D_MODEL), 0.05),
        'g_ffn': 1.0 + nrm((DEPTH, D_MODEL), 0.05),
        'w_in_nsa': nrm((N_NSA_LAYERS, D_MODEL, NSA_IN_WIDTH), D_MODEL ** -0.5),
        'q_gain_nsa': 1.0 + nrm((N_NSA_LAYERS, HEAD_DIM), 0.05),
        'k_gain_nsa': 1.0 + nrm((N_NSA_LAYERS, HEAD_DIM), 0.05),
        'cmp_pos': nrm((N_NSA_LAYERS, 2, CMP_BLOCK, HEAD_DIM), 0.1),
        'cmp_w1': nrm((N_NSA_LAYERS, 2, CMP_BLOCK, HEAD_DIM, CMP_HIDDEN), (CMP_BLOCK * HEAD_DIM) ** -0.5),
        'cmp_b1': nrm((N_NSA_LAYERS, 2, CMP_HIDDEN), 0.01),
        'cmp_w2': nrm((N_NSA_LAYERS, 2, CMP_HIDDEN, HEAD_DIM), CMP_HIDDEN ** -0.5),
        'cmp_b2': nrm((N_NSA_LAYERS, 2, HEAD_DIM), 0.01),
        'w_in_pool': nrm((N_POOL_LAYERS, D_MODEL, POOL_IN_WIDTH), D_MODEL ** -0.5),
        'w_pool_grp': nrm((N_POOL_LAYERS, POOL_GROUPS, POOL_GROUP_DIM, POOL_GROUP_DIM), POOL_GROUP_DIM ** -0.5),
        'pool_scale': 1.0 + nrm((N_POOL_LAYERS, POOL_WIDTH), 0.05),
        'w_mem_kv': nrm((DEPTH, D_MODEL, 2 * XATTN_WIDTH), D_MODEL ** -0.5),
        'xq_gain': 1.0 + nrm((DEPTH, HEAD_DIM), 0.05),
        'xk_gain': 1.0 + nrm((DEPTH, HEAD_DIM), 0.05),
        'w_out': nrm((DEPTH, MIX_WIDTH, D_MODEL), MIX_WIDTH ** -0.5),
        'w_up': nrm((DEPTH, D_MODEL, 2 * D_FF), D_MODEL ** -0.5),
        'conv_w': nrm((DEPTH, CONV_WIDTH, 2 * D_FF), CONV_WIDTH ** -0.5),
        'conv_b': nrm((DEPTH, 2 * D_FF), 0.01),
        'w_down': nrm((DEPTH, D_FF, D_MODEL), D_FF ** -0.5),
    }


def reference(x_prompt, x_sample, cache_nsa_kv, cache_nsa_win, cache_mem_kv, state_pool, state_ffn_conv,
              page_table, mem_prompt, rel_bias, g_mix, g_mem, g_ffn, w_in_nsa, q_gain_nsa, k_gain_nsa,
              cmp_pos, cmp_w1, cmp_b1, cmp_w2, cmp_b2, w_in_pool, w_pool_grp, pool_scale, w_mem_kv,
              xq_gain, xk_gain, w_out, w_up, conv_w, conv_b, w_down):
    n_pages = page_table.shape[1]
    past_len = n_pages * PAGE_SIZE
    yp, ys = x_prompt, x_sample
    bp, bs = yp.shape[0], ys.shape[0]
    nsa_kv_p, nsa_kv_s, nsa_win_p, nsa_win_s = [], [], [], []
    mem_kv_p, pool_p, pool_s, conv_p, conv_s = [], [], [], [], []
    for i in range(DEPTH):
        j = i // 2
        xn_p = rmsnorm(yp, g_mix[i])
        xn_s = rmsnorm(ys, g_mix[i])
        mkv_p = memory_kv(mem_prompt, g_mem[i], w_mem_kv[i], xk_gain[i])
        mkv_s = cache_mem_kv[i]
        mem_kv_p.append(mkv_p)
        if i % 2 == 0:
            zp = xn_p @ w_in_nsa[j]
            zs = xn_s @ w_in_nsa[j]
            cmp_params = (cmp_pos[j], cmp_w1[j], cmp_b1[j], cmp_w2[j], cmp_b2[j])
            past_rows = cache_nsa_kv[j, page_table].reshape(bs, past_len, 4, NSA_KV_HEADS, HEAD_DIM)
            mix_p, rows_p, win_p = nsa_mix(zp, q_gain_nsa[j], k_gain_nsa[j], cmp_params, rel_bias, None, None)
            mix_s, rows_s, win_s = nsa_mix(zs, q_gain_nsa[j], k_gain_nsa[j], cmp_params, rel_bias,
                                           past_rows, cache_nsa_win[j])
            nsa_kv_p.append(rows_p)
            nsa_kv_s.append(rows_s)
            nsa_win_p.append(win_p)
            nsa_win_s.append(win_s)
        else:
            zp = xn_p @ w_in_pool[j]
            zs = xn_s @ w_in_pool[j]
            mix_p, st_p = pool_mix(zp[..., :POOL_WIDTH], 0, jnp.zeros((bp, POOL_STATE, POOL_WIDTH), zp.dtype),
                                   w_pool_grp[j], pool_scale[j])
            mix_s, st_s = pool_mix(zs[..., :POOL_WIDTH], past_len, state_pool[j], w_pool_grp[j], pool_scale[j])
            pool_p.append(st_p)
            pool_s.append(st_s)
        att_p = memory_attend(zp[..., -XATTN_WIDTH:], xq_gain[i], mkv_p)
        att_s = memory_attend(zs[..., -XATTN_WIDTH:], xq_gain[i], mkv_s)
        yp = yp + jnp.concatenate([mix_p, att_p], axis=-1) @ w_out[i]
        ys = ys + jnp.concatenate([mix_s, att_s], axis=-1) @ w_out[i]
        f_p, c_p = conv_ffn(rmsnorm(yp, g_ffn[i]), jnp.zeros((bp, CONV_WIDTH - 1, 2 * D_FF), yp.dtype),
                            w_up[i], conv_w[i], conv_b[i], w_down[i])
        f_s, c_s = conv_ffn(rmsnorm(ys, g_ffn[i]), state_ffn_conv[i], w_up[i], conv_w[i], conv_b[i], w_down[i])
        yp = yp + f_p
        ys = ys + f_s
        conv_p.append(c_p)
        conv_s.append(c_s)
    return (yp, ys, jnp.stack(nsa_kv_p), jnp.stack(nsa_kv_s), jnp.stack(nsa_win_p), jnp.stack(nsa_win_s),
            jnp.stack(mem_kv_p), jnp.stack(pool_p), jnp.stack(pool_s), jnp.stack(conv_p), jnp.stack(conv_s))
```

```python
import functools
import math

import numpy as np
import jax
import jax.numpy as jnp
from jax import lax
from jax.experimental import pallas as pl
from jax.experimental.pallas import tpu as pltpu

F32 = jnp.float32
BF16 = jnp.bfloat16

D_MODEL = 1024
PAGE_SIZE = 128
HEAD_DIM = 64
NSA_HEADS = 12
NSA_KV_HEADS = 2
NSA_GROUP = NSA_HEADS // NSA_KV_HEADS
NSA_WIDTH = NSA_HEADS * HEAD_DIM
NSA_KV_WIDTH = NSA_KV_HEADS * HEAD_DIM
CMP_BLOCK = 64
CMP_HIDDEN = 128
SLC_TOPN = 16
WINDOW = 512
FORCED_SCORE = 1e4
XATTN_HEADS = 4
XATTN_WIDTH = XATTN_HEADS * HEAD_DIM
POOL_WINDOWS = (2, 4, 8, 16)
POOL_GROUP_DIM = 192
POOL_WIDTH = len(POOL_WINDOWS) * POOL_GROUP_DIM
POOL_STATE = max(POOL_WINDOWS) - 1
D_FF = 2816
REL_BUCKETS = 32
REL_MAX_DIST = 128
EPS = 1e-6
BIG = 1e30

LANES = 128
PAD_ROWS = WINDOW
NSA_IN_COLS = 1920
FF_CHUNK = D_FF // 2
VMEM_LIMIT = 56 * 1024 * 1024


def _cparams(*sem):
    return pltpu.CompilerParams(dimension_semantics=sem, vmem_limit_bytes=VMEM_LIMIT)


def _nt(a, b):
    return lax.dot_general(a, b, (((1,), (1,)), ((), ())), preferred_element_type=F32)


def _tn(a, b):
    return lax.dot_general(a, b, (((0,), (0,)), ((), ())), preferred_element_type=F32)


def _dot(a, b):
    return jnp.dot(a, b, preferred_element_type=F32)


def _rms_rows(x, g):
    return x * lax.rsqrt(jnp.mean(x * x, axis=-1, keepdims=True) + EPS) * g


def _seg_rms(zc, seg, gain):
    zz = zc * zc
    hi = zz.astype(BF16)
    lo = (zz - hi.astype(F32)).astype(BF16)
    ms = (_dot(hi, seg) + _dot(lo, seg)) * (1.0 / HEAD_DIM)
    return zc * lax.rsqrt(ms + EPS) * gain


def _nsa_inproj_kernel(x_ref, g_ref, w_ref, seg_ref, qg_ref, kg_ref, xg_ref,
                       qn_ref, rows_ref, win_ref, katt_ref, xqn_ref, gates_ref):
    xn = _rms_rows(x_ref[...], g_ref[...])
    z = _dot(xn.astype(BF16), w_ref[...])
    seg = seg_ref[...]
    for h in range(6):
        qn_ref[:, h * LANES:(h + 1) * LANES] = _seg_rms(z[:, h * LANES:(h + 1) * LANES], seg, qg_ref[...]).astype(BF16)
    kv = 768
    rows_ref[:, 0:256] = z[:, kv:kv + 256]
    ks = _seg_rms(z[:, kv + 256:kv + 384], seg, kg_ref[...])
    vs = z[:, kv + 384:kv + 512]
    kw = _seg_rms(z[:, kv + 512:kv + 640], seg, kg_ref[...])
    vw = z[:, kv + 640:kv + 768]
    rows_ref[:, 256:384] = ks
    rows_ref[:, 384:512] = vs
    win_ref[:, 0:128] = kw
    win_ref[:, 128:256] = vw
    katt_ref[:, 0:128] = ks.astype(BF16)
    katt_ref[:, 128:256] = vs.astype(BF16)
    katt_ref[:, 256:384] = kw.astype(BF16)
    katt_ref[:, 384:512] = vw.astype(BF16)
    for h in range(2):
        c0 = 1536 + h * LANES
        xqn_ref[:, h * LANES:(h + 1) * LANES] = _seg_rms(z[:, c0:c0 + LANES], seg, xg_ref[...]).astype(BF16)
    gates_ref[...] = jax.nn.sigmoid(z[:, 1792:1920])


def _nsa_inproj(x, g, w, seg, qg, kg, xg, tm):
    t = x.shape[0]
    full = lambda a: pl.BlockSpec(a.shape, lambda i: (0,) * a.ndim)
    row = lambda n: pl.BlockSpec((tm, n), lambda i: (i, 0))
    return pl.pallas_call(
        _nsa_inproj_kernel,
        grid=(t // tm,),
        in_specs=[row(D_MODEL), full(g), full(w), full(seg), full(qg), full(kg), full(xg)],
        out_specs=[row(768), row(512), row(256), row(512), row(256), row(128)],
        out_shape=[jax.ShapeDtypeStruct((t, 768), BF16), jax.ShapeDtypeStruct((t, 512), F32),
                   jax.ShapeDtypeStruct((t, 256), F32), jax.ShapeDtypeStruct((t, 512), BF16),
                   jax.ShapeDtypeStruct((t, 256), BF16), jax.ShapeDtypeStruct((t, 128), F32)],
        compiler_params=_cparams("parallel"),
    )(x, g, w, seg, qg, kg, xg)


def _pool_inproj_kernel(x_ref, g_ref, w_ref, seg_ref, xg_ref, u_ref, xqn_ref):
    xn = _rms_rows(x_ref[...], g_ref[...])
    z = _dot(xn.astype(BF16), w_ref[...])
    u_ref[...] = z[:, 0:POOL_WIDTH]
    seg = seg_ref[...]
    for h in range(2):
        c0 = POOL_WIDTH + h * LANES
        xqn_ref[:, h * LANES:(h + 1) * LANES] = _seg_rms(z[:, c0:c0 + LANES], seg, xg_ref[...]).astype(BF16)


def _pool_inproj(x, g, w, seg, xg, tm):
    t = x.shape[0]
    full = lambda a: pl.BlockSpec(a.shape, lambda i: (0,) * a.ndim)
    row = lambda n: pl.BlockSpec((tm, n), lambda i: (i, 0))
    return pl.pallas_call(
        _pool_inproj_kernel,
        grid=(t // tm,),
        in_specs=[row(D_MODEL), full(g), full(w), full(seg), full(xg)],
        out_specs=[row(POOL_WIDTH), row(256)],
        out_shape=[jax.ShapeDtypeStruct((t, POOL_WIDTH), F32), jax.ShapeDtypeStruct((t, 256), BF16)],
        compiler_params=_cparams("parallel"),
    )(x, g, w, seg, xg)


def _memkv_kernel(x_ref, g_ref, w_ref, seg_ref, kg_ref, o_ref):
    xn = _rms_rows(x_ref[...], g_ref[...])
    z = _dot(xn.astype(BF16), w_ref[...])
    seg = seg_ref[...]
    for h in range(2):
        o_ref[:, h * LANES:(h + 1) * LANES] = _seg_rms(z[:, h * LANES:(h + 1) * LANES], seg, kg_ref[...])
    o_ref[:, 256:512] = z[:, 256:512]


def _memkv(x, g, w, seg, kg, tm):
    t = x.shape[0]
    full = lambda a: pl.BlockSpec(a.shape, lambda i: (0,) * a.ndim)
    row = lambda n: pl.BlockSpec((tm, n), lambda i: (i, 0))
    return pl.pallas_call(
        _memkv_kernel,
        grid=(t // tm,),
        in_specs=[row(D_MODEL), full(g), full(w), full(seg), full(kg)],
        out_specs=row(512),
        out_shape=jax.ShapeDtypeStruct((t, 512), F32),
        compiler_params=_cparams("parallel"),
    )(x, g, w, seg, kg)


def _compress_kernel(x_ref, pos_ref, w1_ref, b1_ref, w2_ref, b2_ref, kg_ref, seg_ref, o_ref, acc_ref):
    lc = pl.program_id(1)

    @pl.when(lc == 0)
    def _():
        acc_ref[...] = jnp.zeros_like(acc_ref)

    for comp in range(2):
        part = None
        for l in range(8):
            xl = x_ref[:, l, comp * LANES:(comp + 1) * LANES] + pos_ref[comp, l]
            d = _dot(xl.astype(BF16), w1_ref[comp, l])
            part = d if part is None else part + d
        acc_ref[:, comp * 256:(comp + 1) * 256] += part

    @pl.when(lc == pl.num_programs(1) - 1)
    def _():
        for comp in range(2):
            h = jax.nn.gelu(acc_ref[:, comp * 256:(comp + 1) * 256] + b1_ref[comp])
            o = _dot(h.astype(BF16), w2_ref[comp]) + b2_ref[comp]
            if comp == 0:
                o = _seg_rms(o, seg_ref[...], kg_ref[...])
            o_ref[:, comp * LANES:(comp + 1) * LANES] = o


def _compress(x3, pos_t, w1bd, b1t, w2bd, b2t, kg, seg, nbt):
    nb, _, w = x3.shape
    nbt = min(nbt, nb)
    assert nb % nbt == 0
    full = lambda a: pl.BlockSpec(a.shape, lambda j, l: (0,) * a.ndim)
    return pl.pallas_call(
        _compress_kernel,
        grid=(nb // nbt, 8),
        in_specs=[pl.BlockSpec((nbt, 8, 256), lambda j, l: (j, l, 0)),
                  pl.BlockSpec((2, 8, 1, LANES), lambda j, l: (0, l, 0, 0)),
                  pl.BlockSpec((2, 8, LANES, 256), lambda j, l: (0, l, 0, 0)),
                  full(b1t), full(w2bd), full(b2t), full(kg), full(seg)],
        out_specs=pl.BlockSpec((nbt, 256), lambda j, l: (j, 0)),
        out_shape=jax.ShapeDtypeStruct((nb, 256), F32),
        scratch_shapes=[pltpu.VMEM((nbt, 512), F32)],
        compiler_params=_cparams("parallel", "arbitrary"),
    )(x3, pos_t, w1bd, b1t, w2bd, b2t, kg, seg)


def _bucket_np(d):
    n = np.maximum(d, 0)
    max_exact = REL_BUCKETS // 2
    nf = np.maximum(n, 1).astype(np.float32)
    large = max_exact + (np.log(nf / max_exact) / math.log(REL_MAX_DIST / max_exact)
                         * (REL_BUCKETS - max_exact)).astype(np.int32)
    large = np.minimum(large, REL_BUCKETS - 1)
    return np.where(n < max_exact, n, large).astype(np.int32)


def _bias_table_kernel(rb_ref, code_ref, o_ref):
    code = code_ref[...]
    for h in range(NSA_HEADS):
        far = rb_ref[REL_BUCKETS - 1, h]
        acc = jnp.full(code.shape, -BIG, F32)
        for k in range(REL_BUCKETS):
            acc = jnp.where(code == k, rb_ref[k, h] - far, acc)
        o_ref[h] = acc


def _bias_table(rel_bias, code):
    code = jnp.asarray(code, jnp.int32)
    return pl.pallas_call(
        _bias_table_kernel,
        in_specs=[pl.BlockSpec(memory_space=pltpu.SMEM), pl.BlockSpec(code.shape, lambda: (0, 0))],
        out_specs=pl.BlockSpec((NSA_HEADS,) + code.shape, lambda: (0, 0, 0)),
        out_shape=jax.ShapeDtypeStruct((NSA_HEADS,) + code.shape, F32),
    )(rel_bias, code)


def _window_codes():
    ql = np.arange(64)[:, None]
    j = np.arange(WINDOW + 64)[None, :]
    d = WINDOW + ql - j
    return np.where((d >= 0) & (d <= WINDOW), _bucket_np(d), -1).astype(np.int32)


def _cmp_codes():
    delta = np.arange(8)[:, None]
    ql = np.arange(64)[None, :]
    d = delta * CMP_BLOCK + ql - (CMP_BLOCK - 1)
    return np.where(d >= 0, _bucket_np(d), -1).astype(np.int32)


def _softmax_update(s, v, m, l, acc):
    m_new = jnp.maximum(m, jnp.max(s, axis=1, keepdims=True))
    a = jnp.exp(m - m_new)
    p = jnp.exp(s - m_new)
    l = a * l + jnp.sum(p, axis=1, keepdims=True)
    acc = a * acc + _dot(p.astype(BF16), v)
    return m_new, l, acc


def _nsa_prompt_kernel(qn_ref, gates_ref, cmp_ref, kv_ref, oh_ref, wb_ref, ct_ref, amat_ref, o_ref):
    i = pl.program_id(1)
    nc = cmp_ref.shape[1]
    rows = NSA_GROUP * 64
    q = qn_ref[0].astype(F32)
    gts = gates_ref[0]
    kc = cmp_ref[0, :, 0:128].astype(BF16)
    vc = cmp_ref[0, :, 128:256].astype(BF16)
    lane = lax.broadcasted_iota(jnp.int32, (64, LANES), 1)
    lane_r = lax.broadcasted_iota(jnp.int32, (rows, LANES), 1)
    half = [lane < 64, lane >= 64]
    qg = [jnp.concatenate([jnp.where(half[g], q[:, h * LANES:(h + 1) * LANES], 0.0) for h in range(NSA_GROUP)],
                          axis=0).astype(BF16) for g in range(2)]

    o_c, imp = [], []
    blk_r = lax.broadcasted_iota(jnp.int32, (nc, rows), 0)
    for g in range(2):
        ctg = ct_ref[g]
        bias = jnp.where(blk_r == i, ctg[0:1], jnp.where(blk_r == i - 1, ctg[1:2], jnp.where(
            blk_r == i - 2, ctg[2:3], jnp.where(blk_r > i, -BIG, 0.0))))
        lc = _nt(kc, qg[g]) + bias
        e = jnp.exp(lc - jnp.max(lc, axis=0, keepdims=True))
        p = jnp.where(bias > -0.5 * BIG, e / jnp.sum(e, axis=0, keepdims=True), 0.0)
        o_c.append(_tn(p.astype(BF16), vc))
        s3 = p[:, 0:128] + p[:, 128:256] + p[:, 256:384]
        imp.append(s3 + pltpu.roll(s3, 64, axis=1))
    lane_c = lax.broadcasted_iota(jnp.int32, (nc, LANES), 1)
    blk = lax.broadcasted_iota(jnp.int32, (nc, LANES), 0)
    impp = jnp.where(lane_c < 64, imp[0], imp[1])
    forced = (blk == 0) | (blk == i) | (blk == i - 1)
    score = jnp.where(blk <= i, jnp.where(forced, FORCED_SCORE, impp), -1.0)
    rank = jnp.zeros((nc, LANES), F32)
    for cp in range(nc):
        row = score[cp:cp + 1, :]
        beats = (row > score) | ((row == score) & (blk > cp))
        rank = rank + jnp.where(beats, 1.0, 0.0)
    notsel = jnp.where((rank < float(min(SLC_TOPN, nc))) & (score >= 0.0), 0.0, 1.0)
    notsel_far = jnp.where(blk > i - 3, 1.0, notsel)
    tail = jnp.where(lax.broadcasted_iota(jnp.int32, (LANES - nc, LANES), 0) == 64 - nc, 1.0, 0.0)
    ns_near = jnp.concatenate([notsel, tail], axis=0).astype(BF16)
    ns_far = jnp.concatenate([notsel_far, tail], axis=0).astype(BF16)
    m_win = jnp.where(lane_r == 64, -BIG, 0.0).astype(BF16)

    outs = []
    for g in range(2):
        a = amat_ref[g]
        m_near = (_nt(a, ns_near) * -BIG).astype(BF16)
        m_far = (_nt(a, ns_far) * -BIG).astype(BF16)
        lhs_far = jnp.concatenate([qg[g], m_far], axis=1)
        lhs_near = jnp.concatenate([qg[g], m_near], axis=1)
        lhs_win = jnp.concatenate([qg[g], m_win], axis=1)

        def far_body(t, carry):
            r0 = pl.multiple_of(PAD_ROWS + t * 256, 256)
            k = jnp.concatenate([kv_ref[0, pl.ds(r0, 256), 0:128], oh_ref[pl.ds(r0, 256), :]], axis=1)
            v = kv_ref[0, pl.ds(r0, 256), 128:256]
            return _softmax_update(_nt(lhs_far, k), v, *carry)

        init = (jnp.full((rows, 1), -BIG, F32), jnp.zeros((rows, 1), F32), jnp.zeros((rows, LANES), F32))
        carry = lax.fori_loop(0, (i + 1) // 4, far_body, init)
        r0 = pl.multiple_of(PAD_ROWS + (i - 2) * 64, 64)
        k = jnp.concatenate([kv_ref[0, pl.ds(r0, 192), 0:128], oh_ref[pl.ds(r0, 192), :]], axis=1)
        v = kv_ref[0, pl.ds(r0, 192), 128:256]
        s = _nt(lhs_near, k) + wb_ref[g, :, 384:576]
        _, l, acc = _softmax_update(s, v, *carry)
        o_s = acc / l

        r0 = pl.multiple_of(PAD_ROWS + (i - 8) * 64, 64)
        k = jnp.concatenate([kv_ref[0, pl.ds(r0, 576), 256:384], oh_ref[pl.ds(r0, 576), :]], axis=1)
        v = kv_ref[0, pl.ds(r0, 576), 384:512]
        s = _nt(lhs_win, k) + wb_ref[g]
        e = jnp.exp(s - jnp.max(s, axis=1, keepdims=True))
        o_w = _dot(e.astype(BF16), v) / jnp.sum(e, axis=1, keepdims=True)

        per_head = []
        for h in range(NSA_GROUP):
            c0 = (g * NSA_GROUP + h) * 3
            r = slice(h * 64, (h + 1) * 64)
            per_head.append(gts[:, c0:c0 + 1] * o_c[g][r] + gts[:, c0 + 1:c0 + 2] * o_s[r]
                            + gts[:, c0 + 2:c0 + 3] * o_w[r])
        outs.append(per_head)
    low = lax.broadcasted_iota(jnp.int32, (64, LANES), 1) < 64
    for h in range(NSA_GROUP):
        o_ref[0, :, h * LANES:(h + 1) * LANES] = jnp.where(low, outs[0][h], outs[1][h]).astype(BF16)


def _nsa_prompt_attn(qn, gates, cmp, kvatt, oh, wb, ct, amat):
    b, l, _ = qn.shape
    nc = l // 64
    full = lambda a: pl.BlockSpec(a.shape, lambda bi, i: (0,) * a.ndim)
    return pl.pallas_call(
        _nsa_prompt_kernel,
        grid=(b, nc),
        in_specs=[pl.BlockSpec((1, 64, 768), lambda bi, i: (bi, i, 0)),
                  pl.BlockSpec((1, 64, 128), lambda bi, i: (bi, i, 0)),
                  pl.BlockSpec((1, nc, 256), lambda bi, i: (bi, 0, 0)),
                  pl.BlockSpec((1, PAD_ROWS + l, 512), lambda bi, i: (bi, 0, 0)),
                  full(oh), full(wb), full(ct), full(amat)],
        out_specs=pl.BlockSpec((1, 64, 768), lambda bi, i: (bi, i, 0)),
        out_shape=jax.ShapeDtypeStruct((b, l, 768), BF16),
        compiler_params=_cparams("parallel", "arbitrary"),
    )(qn, gates, cmp, kvatt, oh, wb, ct, amat)


def _mem_attn_kernel(q_ref, kv_ref, o_ref):
    q = q_ref[0].astype(F32)
    k = kv_ref[0, :, 0:256].astype(BF16)
    v = kv_ref[0, :, 256:512].astype(BF16)
    head = lax.broadcasted_iota(jnp.int32, q.shape, 1) // HEAD_DIM
    out = jnp.zeros(q.shape, F32)
    for h in range(XATTN_HEADS):
        qh = jnp.where(head == h, q, 0.0).astype(BF16)
        s = _nt(qh, k)
        e = jnp.exp(s - jnp.max(s, axis=1, keepdims=True))
        p = e / jnp.sum(e, axis=1, keepdims=True)
        out = out + jnp.where(head == h, _dot(p.astype(BF16), v), 0.0)
    o_ref[0] = out.astype(BF16)


def _mem_attn(xqn, mkv, tq):
    b, l, _ = xqn.shape
    nm = mkv.shape[1]
    return pl.pallas_call(
        _mem_attn_kernel,
        grid=(b, l // tq),
        in_specs=[pl.BlockSpec((1, tq, 256), lambda bi, i: (bi, i, 0)),
                  pl.BlockSpec((1, nm, 512), lambda bi, i: (bi, 0, 0))],
        out_specs=pl.BlockSpec((1, tq, 256), lambda bi, i: (bi, i, 0)),
        out_shape=jax.ShapeDtypeStruct((b, l, 256), BF16),
        compiler_params=_cparams("parallel", "parallel"),
    )(xqn, mkv)


def _pool_kernel(u_ref, prev_ref, w_ref, scale_ref, o_ref, carry_ref, *, pos0):
    i = pl.program_id(1)
    tp = u_ref.shape[1]

    @pl.when(i == 0)
    def _():
        carry_ref[...] = prev_ref[0]

    u = u_ref[0]
    ext = jnp.concatenate([carry_ref[...], u], axis=0)
    carry_ref[...] = ext[tp:tp + 16]
    s2 = ext + pltpu.roll(ext, 1, axis=0)
    s4 = s2 + pltpu.roll(s2, 2, axis=0)
    s8 = s4 + pltpu.roll(s4, 4, axis=0)
    s16 = s8 + pltpu.roll(s8, 8, axis=0)
    lane = lax.broadcasted_iota(jnp.int32, (tp, POOL_WIDTH), 1)
    pos1 = (pos0 + 1 + i * tp + lax.broadcasted_iota(jnp.int32, (tp, POOL_WIDTH), 0)).astype(F32)
    grp = lane // POOL_GROUP_DIM
    ssum = jnp.where(grp == 0, s2[16:], jnp.where(grp == 1, s4[16:], jnp.where(grp == 2, s8[16:], s16[16:])))
    win = jnp.where(grp == 0, 2.0, jnp.where(grp == 1, 4.0, jnp.where(grp == 2, 8.0, 16.0)))
    pooled = ssum / jnp.minimum(win, pos1) - u
    o_ref[0] = (_dot(pooled.astype(BF16), w_ref[...]) * scale_ref[...]).astype(BF16)


def _pool_mix(u, prev16, wbd, scale, tp, pos0):
    b, l, _ = u.shape
    full = lambda a: pl.BlockSpec(a.shape, lambda bi, i: (0,) * a.ndim)
    return pl.pallas_call(
        functools.partial(_pool_kernel, pos0=pos0),
        grid=(b, l // tp),
        in_specs=[pl.BlockSpec((1, tp, POOL_WIDTH), lambda bi, i: (bi, i, 0)),
                  pl.BlockSpec((1, 16, POOL_WIDTH), lambda bi, i: (bi, 0, 0)),
                  full(wbd), full(scale)],
        out_specs=pl.BlockSpec((1, tp, POOL_WIDTH), lambda bi, i: (bi, i, 0)),
        out_shape=jax.ShapeDtypeStruct((b, l, POOL_WIDTH), BF16),
        scratch_shapes=[pltpu.VMEM((16, POOL_WIDTH), F32)],
        compiler_params=_cparams("parallel", "arbitrary"),
    )(u, prev16, wbd, scale)


def _outproj_kernel(y_ref, mix_ref, att_ref, wm_ref, wa_ref, o_ref):
    o_ref[...] = y_ref[...] + _dot(mix_ref[...], wm_ref[...]) + _dot(att_ref[...], wa_ref[...])


def _outproj(y, mix, att, wm, wa, tm):
    t = y.shape[0]
    full = lambda a: pl.BlockSpec(a.shape, lambda i: (0,) * a.ndim)
    row = lambda n: pl.BlockSpec((tm, n), lambda i: (i, 0))
    return pl.pallas_call(
        _outproj_kernel,
        grid=(t // tm,),
        in_specs=[row(D_MODEL), row(768), row(256), full(wm), full(wa)],
        out_specs=row(D_MODEL),
        out_shape=jax.ShapeDtypeStruct((t, D_MODEL), F32),
        compiler_params=_cparams("parallel"),
    )(y, mix, att, wm, wa)


def _ffn_half(h, w_ref, cw_ref, cb_ref, ext_ref, prev8, fix):
    tm = h.shape[0]
    u = _dot(h, w_ref[...])
    ext_ref[0:8, :] = prev8
    ext_ref[8:8 + tm, :] = u
    s1 = ext_ref[7:7 + tm, :]
    s2 = ext_ref[6:6 + tm, :]
    if fix is not None:
        rowm, p1, p2 = fix
        s1 = jnp.where(rowm >= 1, s1, p1)
        s2 = jnp.where(rowm >= 2, s2, p2)
    cw = cw_ref[...]
    return u, cb_ref[...] + cw[0:1] * s2 + cw[1:2] * s1 + cw[2:3] * u


def _ffn_prompt_kernel(y_ref, g_ref, wa_ref, wv_ref, cwa_ref, cwv_ref, cba_ref, cbv_ref, wd_ref,
                       o_ref, sa_ref, sv_ref, h_ref, acc_ref, ext_ref, carry_ref, *, tiles_per_seq):
    j = pl.program_id(0)
    c = pl.program_id(1)
    tm = y_ref.shape[0]

    @pl.when(c == 0)
    def _():
        h_ref[...] = _rms_rows(y_ref[...], g_ref[...]).astype(BF16)

    @pl.when((j % tiles_per_seq) == 0)
    def _():
        carry_ref[c] = jnp.zeros(carry_ref.shape[1:], F32)

    h = h_ref[...]
    ua, ya = _ffn_half(h, wa_ref, cwa_ref, cba_ref, ext_ref, carry_ref[c, 0], None)
    carry_ref[c, 0] = ua[tm - 8:tm]
    sa_ref[0] = ua[tm - 8:tm]
    uv, yv = _ffn_half(h, wv_ref, cwv_ref, cbv_ref, ext_ref, carry_ref[c, 1], None)
    carry_ref[c, 1] = uv[tm - 8:tm]
    sv_ref[0] = uv[tm - 8:tm]
    d = _dot((jax.nn.silu(ya) * yv).astype(BF16), wd_ref[...])

    @pl.when(c == 0)
    def _():
        acc_ref[...] = d

    @pl.when(c == pl.num_programs(1) - 1)
    def _():
        o_ref[...] = y_ref[...] + acc_ref[...] + d


def _ffn_prompt(y, g, w_up, conv_w, conv_b, w_down, tm, seq_len):
    t = y.shape[0]
    nseq = t // seq_len
    tps = seq_len // tm
    fc = FF_CHUNK
    ncf = D_FF // fc
    full = lambda a: pl.BlockSpec(a.shape, lambda j, c: (0,) * a.ndim)
    outs = pl.pallas_call(
        functools.partial(_ffn_prompt_kernel, tiles_per_seq=tps),
        grid=(t // tm, ncf),
        in_specs=[pl.BlockSpec((tm, D_MODEL), lambda j, c: (j, 0)), full(g),
                  pl.BlockSpec((D_MODEL, fc), lambda j, c: (0, c)),
                  pl.BlockSpec((D_MODEL, fc), lambda j, c: (0, ncf + c)),
                  pl.BlockSpec((3, fc), lambda j, c: (0, c)),
                  pl.BlockSpec((3, fc), lambda j, c: (0, ncf + c)),
                  pl.BlockSpec((1, fc), lambda j, c: (0, c)),
                  pl.BlockSpec((1, fc), lambda j, c: (0, ncf + c)),
                  pl.BlockSpec((fc, D_MODEL), lambda j, c: (c, 0))],
        out_specs=[pl.BlockSpec((tm, D_MODEL), lambda j, c: (j, 0)),
                   pl.BlockSpec((1, 8, fc), lambda j, c: (j, 0, c)),
                   pl.BlockSpec((1, 8, fc), lambda j, c: (j, 0, c))],
        out_shape=[jax.ShapeDtypeStruct((t, D_MODEL), F32),
                   jax.ShapeDtypeStruct((t // tm, 8, D_FF), F32),
                   jax.ShapeDtypeStruct((t // tm, 8, D_FF), F32)],
        scratch_shapes=[pltpu.VMEM((tm, D_MODEL), BF16), pltpu.VMEM((tm, D_MODEL), F32),
                        pltpu.VMEM((tm + 8, fc), F32), pltpu.VMEM((ncf, 2, 8, fc), F32)],
        compiler_params=_cparams("arbitrary", "arbitrary"),
    )(y, g, w_up, w_up, conv_w, conv_w, conv_b, conv_b, w_down)
    y_out, sa, sv = outs
    last = lambda s: s.reshape(nseq, tps, 8, D_FF)[:, tps - 1, 6:8]
    state = jnp.concatenate([last(sa), last(sv)], axis=-1)
    return y_out, state


def _ffn_sample_kernel(y_ref, g_ref, wa_ref, wv_ref, cwa_ref, cwv_ref, cba_ref, cbv_ref, wd_ref,
                       p1a_ref, p1v_ref, p2a_ref, p2v_ref, o_ref, ua_ref, uv_ref, acc_ref, ext_ref):
    c = pl.program_id(0)
    h = _rms_rows(y_ref[...], g_ref[...]).astype(BF16)
    tm = h.shape[0]
    rowm = lax.broadcasted_iota(jnp.int32, (tm, wa_ref.shape[1]), 0) % 8
    zero8 = jnp.zeros((8, wa_ref.shape[1]), F32)
    ua, ya = _ffn_half(h, wa_ref, cwa_ref, cba_ref, ext_ref, zero8, (rowm, p1a_ref[...], p2a_ref[...]))
    uv, yv = _ffn_half(h, wv_ref, cwv_ref, cbv_ref, ext_ref, zero8, (rowm, p1v_ref[...], p2v_ref[...]))
    ua_ref[...] = ua
    uv_ref[...] = uv
    d = _dot((jax.nn.silu(ya) * yv).astype(BF16), wd_ref[...])

    @pl.when(c == 0)
    def _():
        acc_ref[...] = d

    @pl.when(c == pl.num_programs(0) - 1)
    def _():
        o_ref[...] = y_ref[...] + acc_ref[...] + d


def _ffn_sample(y, g, w_up, conv_w, conv_b, w_down, state):
    t = y.shape[0]
    nseq = t // 8
    fc = FF_CHUNK
    ncf = D_FF // fc
    z = jnp.zeros((nseq, 1, 2 * D_FF), F32)
    p1 = jnp.concatenate([state[:, 1:2]] + [z] * 7, axis=1).reshape(t, 2 * D_FF)
    p2 = jnp.concatenate([state[:, 0:1], state[:, 1:2]] + [z] * 6, axis=1).reshape(t, 2 * D_FF)
    full = lambda a: pl.BlockSpec(a.shape, lambda c: (0,) * a.ndim)
    ca = lambda r: pl.BlockSpec((r, fc), lambda c: (0, c))
    cv = lambda r: pl.BlockSpec((r, fc), lambda c: (0, ncf + c))
    y_out, ua, uv = pl.pallas_call(
        _ffn_sample_kernel,
        grid=(ncf,),
        in_specs=[full(y), full(g), ca(D_MODEL), cv(D_MODEL), ca(3), cv(3), ca(1), cv(1),
                  pl.BlockSpec((fc, D_MODEL), lambda c: (c, 0)), ca(t), cv(t), ca(t), cv(t)],
        out_specs=[pl.BlockSpec((t, D_MODEL), lambda c: (0, 0)), ca(t), ca(t)],
        out_shape=[jax.ShapeDtypeStruct((t, D_MODEL), F32), jax.ShapeDtypeStruct((t, D_FF), F32),
                   jax.ShapeDtypeStruct((t, D_FF), F32)],
        scratch_shapes=[pltpu.VMEM((t, D_MODEL), F32), pltpu.VMEM((t + 8, fc), F32)],
        compiler_params=_cparams("arbitrary"),
    )(y, g, w_up, w_up, conv_w, conv_w, conv_b, conv_b, w_down, p1, p1, p2, p2)
    u = jnp.concatenate([ua, uv], axis=-1).reshape(nseq, 8, 2 * D_FF)
    return y_out, u[:, 6:8]


KEY_CHUNK = 1024


def _nsa_sample_kernel(pt_ref, qn_ref, gates_ref, rows_ref, winc_ref, winn_ref, scb_ref, sbt_ref, wbt_ref,
                       rmat_ref, cmp_hbm, cache_hbm, o_ref, kvbuf, kcv, s_ref, score_ref, madd_ref, sem, sem2):
    b = pl.program_id(0)
    npages = pt_ref.shape[1]
    ncs = 2 * npages
    nkeys = npages * PAGE_SIZE

    def page_copy(p):
        return pltpu.make_async_copy(cache_hbm.at[pt_ref[b, p], :, pl.ds(256, 256)],
                                     kvbuf.at[pl.ds(pl.multiple_of(p * PAGE_SIZE, PAGE_SIZE), PAGE_SIZE), :], sem)

    def cmp_copy(p):
        return pltpu.make_async_copy(cmp_hbm.at[pt_ref[b, p]], kcv.at[p], sem2)

    def start_all(p, c):
        cmp_copy(p).start()
        page_copy(p).start()
        return c

    lax.fori_loop(0, npages, start_all, 0)

    q = qn_ref[0].astype(F32)
    gts = gates_ref[0]
    lane8 = lax.broadcasted_iota(jnp.int32, (8, LANES), 1)
    half = [lane8 < 64, lane8 >= 64]
    pieces = [jnp.where(half[g], q[:, h * LANES:(h + 1) * LANES], 0.0) for g in range(2) for h in range(NSA_GROUP)]
    qrows = jnp.concatenate(pieces + [jnp.zeros((32, LANES), F32)], axis=0).astype(BF16)

    def wait_cmp(p, c):
        cmp_copy(p).wait()
        return c

    lax.fori_loop(0, npages, wait_cmp, 0)

    kc = jnp.concatenate([kcv[:, 0, 0:128], kcv[:, 1, 0:128]], axis=0).astype(BF16)
    vc = jnp.concatenate([kcv[:, 0, 128:256], kcv[:, 1, 128:256]], axis=0).astype(BF16)
    lc = _nt(kc, qrows) + scb_ref[...]
    e = jnp.exp(lc - jnp.max(lc, axis=0, keepdims=True))
    p_c = e / jnp.sum(e, axis=0, keepdims=True)
    o_c = _tn(p_c.astype(BF16), vc)
    hi = p_c.astype(BF16)
    lo = (p_c - hi.astype(F32)).astype(BF16)
    imp = _dot(hi, rmat_ref[...]) + _dot(lo, rmat_ref[...])

    def blk_of(r):
        return jnp.where(r < npages, 2 * r, jnp.where(r < ncs, 2 * (r - npages) + 1, r))

    blk = blk_of(lax.broadcasted_iota(jnp.int32, (ncs + 8, LANES), 0))
    impx = jnp.concatenate([imp, jnp.zeros((8, LANES), F32)], axis=0)
    forced = (blk == 0) | (blk == ncs) | (blk == ncs - 1)
    score = jnp.where(blk <= ncs, jnp.where(forced, FORCED_SCORE, impx), -1.0)
    for r in range(ncs + 1):
        score_ref[r] = score[r:r + 1, :]

    def rank_body(rp, rank):
        row = score_ref[rp]
        beats = (row > score) | ((row == score) & (blk > blk_of(rp)))
        return rank + jnp.where(beats, 1.0, 0.0)

    rank = lax.fori_loop(0, ncs + 1, rank_body, jnp.zeros((ncs + 8, LANES), F32))
    madd = jnp.where((rank < float(SLC_TOPN)) & (score >= 0.0), 0.0, -BIG)
    for r in range(ncs):
        madd_ref[r] = madd[r:r + 1, :]

    def wait_page(p, c):
        page_copy(p).wait()
        return c

    lax.fori_loop(0, npages, wait_page, 0)

    bpc = KEY_CHUNK // 64

    def s_body(t, c):
        r0 = pl.multiple_of(t * KEY_CHUNK, KEY_CHUNK)
        st = _nt(kvbuf[pl.ds(r0, KEY_CHUNK), 0:128].astype(BF16), qrows)
        for cc in range(bpc):
            mrow = madd_ref[t * (bpc // 2) + cc // 2 + npages * (cc % 2)]
            s_ref[pl.ds(pl.multiple_of(r0 + cc * 64, 64), 64), :] = st[cc * 64:(cc + 1) * 64] + mrow
        return c

    lax.fori_loop(0, nkeys // KEY_CHUNK, s_body, 0)
    s_ref[nkeys - 128:nkeys, :] = s_ref[nkeys - 128:nkeys, :] + sbt_ref[0:128, :]
    knew = rows_ref[0, :, 256:384].astype(BF16)
    vnew = rows_ref[0, :, 384:512].astype(BF16)
    s_new = _nt(knew, qrows) + sbt_ref[128:136, :]

    def max_body(t, m):
        r0 = pl.multiple_of(t * KEY_CHUNK, KEY_CHUNK)
        return jnp.maximum(m, jnp.max(s_ref[pl.ds(r0, KEY_CHUNK), :], axis=0, keepdims=True))

    m = lax.fori_loop(0, nkeys // KEY_CHUNK, max_body, jnp.max(s_new, axis=0, keepdims=True))

    def e_body(t, l):
        r0 = pl.multiple_of(t * KEY_CHUNK, KEY_CHUNK)
        et = jnp.exp(s_ref[pl.ds(r0, KEY_CHUNK), :] - m)
        s_ref[pl.ds(r0, KEY_CHUNK), :] = et
        return l + jnp.sum(et, axis=0, keepdims=True)

    e_new = jnp.exp(s_new - m)
    l = lax.fori_loop(0, nkeys // KEY_CHUNK, e_body, jnp.sum(e_new, axis=0, keepdims=True))
    inv = 1.0 / l

    def pv_body(t, acc):
        r0 = pl.multiple_of(t * KEY_CHUNK, KEY_CHUNK)
        pt = (s_ref[pl.ds(r0, KEY_CHUNK), :] * inv).astype(BF16)
        return acc + _tn(pt, kvbuf[pl.ds(r0, KEY_CHUNK), 128:256].astype(BF16))

    o_s = lax.fori_loop(0, nkeys // KEY_CHUNK, pv_body, _tn((e_new * inv).astype(BF16), vnew))

    kw = jnp.concatenate([winc_ref[0, :, 0:128], winn_ref[0, :, 0:128]], axis=0).astype(BF16)
    vw = jnp.concatenate([winc_ref[0, :, 128:256], winn_ref[0, :, 128:256]], axis=0).astype(BF16)
    sw = _nt(kw, qrows) + wbt_ref[...]
    ew = jnp.exp(sw - jnp.max(sw, axis=0, keepdims=True))
    o_w = _tn((ew / jnp.sum(ew, axis=0, keepdims=True)).astype(BF16), vw)

    def gate_col(br):
        cols = [gts[:, hh * 3 + br:hh * 3 + br + 1] for hh in range(NSA_HEADS)]
        return jnp.concatenate(cols + [jnp.zeros((32, 1), F32)], axis=0)

    o = gate_col(0) * o_c + gate_col(1) * o_s + gate_col(2) * o_w
    low = lax.broadcasted_iota(jnp.int32, (8, LANES), 1) < 64
    for h in range(NSA_GROUP):
        r0, r1 = h * 8, (NSA_GROUP + h) * 8
        o_ref[0, :, h * LANES:(h + 1) * LANES] = jnp.where(low, o[r0:r0 + 8], o[r1:r1 + 8]).astype(BF16)


def _nsa_sample_attn(page_table, qn, gates, rows, winc, winn, scb, sbt, wbt, rmat, cmp_phys, cache):
    b, npages = page_table.shape
    ncs = 2 * npages
    nkeys = npages * PAGE_SIZE
    assert nkeys % KEY_CHUNK == 0
    full = lambda a: pl.BlockSpec(a.shape, lambda bi, pt: (0,) * a.ndim)
    per_b = lambda a: pl.BlockSpec((1,) + a.shape[1:], lambda bi, pt: (bi,) + (0,) * (a.ndim - 1))
    grid_spec = pltpu.PrefetchScalarGridSpec(
        num_scalar_prefetch=1,
        grid=(b,),
        in_specs=[per_b(qn), per_b(gates), per_b(rows), per_b(winc), per_b(winn),
                  full(scb), full(sbt), full(wbt), full(rmat),
                  pl.BlockSpec(memory_space=pl.ANY), pl.BlockSpec(memory_space=pl.ANY)],
        out_specs=pl.BlockSpec((1, 8, 768), lambda bi, pt: (bi, 0, 0)),
        scratch_shapes=[pltpu.VMEM((nkeys, 256), F32), pltpu.VMEM((npages, 2, 256), F32),
                        pltpu.VMEM((nkeys, LANES), F32), pltpu.VMEM((ncs + 8, 1, LANES), F32),
                        pltpu.VMEM((ncs + 8, 1, LANES), F32),
                        pltpu.SemaphoreType.DMA(()), pltpu.SemaphoreType.DMA(())],
    )
    return pl.pallas_call(
        _nsa_sample_kernel,
        grid_spec=grid_spec,
        out_shape=jax.ShapeDtypeStruct((b, 8, 768), BF16),
        compiler_params=_cparams("arbitrary"),
    )(page_table, qn, gates, rows, winc, winn, scb, sbt, wbt, rmat, cmp_phys, cache)


def _block_diag2(w):
    z = jnp.zeros_like(w)
    return jnp.concatenate([jnp.concatenate([w, z], axis=-1), jnp.concatenate([z, w], axis=-1)], axis=-2)


def _q_slot_perm():
    return np.concatenate([np.r_[h * 64:(h + 1) * 64, (NSA_GROUP + h) * 64:(NSA_GROUP + h + 1) * 64]
                           for h in range(NSA_GROUP)])


def _sample_codes(ncs, wbuf):
    ql = np.arange(8)[None, :]
    c = np.arange(ncs)[:, None]
    scb = _bucket_np((ncs - c) * CMP_BLOCK + ql - (CMP_BLOCK - 1))
    kap = np.arange(136)[:, None]
    d = np.where(kap < 128, 128 + ql - kap, ql - (kap - 128))
    sbt = np.where(d >= 0, _bucket_np(d), -1)
    j = np.arange(wbuf + 8)[:, None]
    d = wbuf + ql - j
    wbt = np.where((d >= 0) & (d <= WINDOW), _bucket_np(d), -1)
    return np.concatenate([scb, sbt, wbt], axis=0).astype(np.int32)


def kernel(x_prompt, x_sample, cache_nsa_kv, cache_nsa_win, cache_mem_kv, state_pool, state_ffn_conv,
           page_table, mem_prompt, rel_bias, g_mix, g_mem, g_ffn, w_in_nsa, q_gain_nsa, k_gain_nsa,
           cmp_pos, cmp_w1, cmp_b1, cmp_w2, cmp_b2, w_in_pool, w_pool_grp, pool_scale, w_mem_kv,
           xq_gain, xk_gain, w_out, w_up, conv_w, conv_b, w_down):
    bp, lp, _ = x_prompt.shape
    bs, ls, _ = x_sample.shape
    assert ls == 8 and lp % 512 == 0 and lp // 64 <= 64
    npages = page_table.shape[1]
    past_len = npages * PAGE_SIZE
    n_phys = cache_nsa_kv.shape[1]
    wbuf = cache_nsa_win.shape[2]
    nm = mem_prompt.shape[1]
    tp_, ts_ = bp * lp, bs * ls
    nc = lp // 64
    ncs = 2 * npages
    scale = HEAD_DIM ** -0.5
    tile2 = lambda v: jnp.tile(v, 2)[None, :]

    a = np.arange(LANES)
    seg = jnp.asarray(a[:, None] // 64 == a[None, :] // 64, BF16)
    perm = _q_slot_perm()

    wb = _bias_table(rel_bias, _window_codes()).reshape(2, NSA_GROUP * 64, WINDOW + 64)
    ct = _bias_table(rel_bias, _cmp_codes()).reshape(2, NSA_GROUP, 8, 64)
    ct = ct.transpose(0, 2, 1, 3).reshape(2, 8, NSA_GROUP * 64)
    amat = np.zeros((2, NSA_GROUP * 64, LANES), np.float32)
    for g in range(2):
        for h in range(NSA_GROUP):
            amat[g, h * 64 + np.arange(64), g * 64 + np.arange(64)] = 1.0
    amat = jnp.asarray(amat, BF16)
    r = np.arange(PAD_ROWS + lp)
    oh_np = np.zeros((PAD_ROWS + lp, LANES), np.float32)
    oh_np[r, np.where(r < PAD_ROWS, 64, (r - PAD_ROWS) // 64)] = 1.0
    oh = jnp.asarray(oh_np, BF16)
    stab = _bias_table(rel_bias, _sample_codes(ncs, wbuf))
    stab = jnp.pad(stab.transpose(1, 0, 2).reshape(-1, NSA_HEADS * 8), ((0, 0), (0, LANES - NSA_HEADS * 8)))
    scb, sbt, wbt = stab[:ncs], stab[ncs:ncs + 136], stab[ncs + 136:]
    scb = jnp.concatenate([scb[0::2], scb[1::2]], axis=0)
    lam = np.arange(LANES)
    rm = (lam[:, None] < 96) & (lam[None, :] < 96) & (lam[:, None] // 48 == lam[None, :] // 48) \
        & (lam[:, None] % 8 == lam[None, :] % 8)
    rmat = jnp.asarray(rm, BF16)

    yp = x_prompt.reshape(tp_, D_MODEL)
    ys = x_sample.reshape(ts_, D_MODEL)
    tm = 512
    outs = {k: [] for k in ("kv_p", "kv_s", "win_p", "win_s", "mem_p", "pool_p", "pool_s", "conv_p", "conv_s")}
    depth = g_mix.shape[0]
    for i in range(depth):
        j = i // 2
        wo = w_out[i]
        xg = tile2(xq_gain[i]) * scale
        mkv_p = _memkv(mem_prompt.reshape(bp * nm, D_MODEL), g_mem[i][None], w_mem_kv[i].astype(BF16), seg,
                       tile2(xk_gain[i]), min(tm, bp * nm)).reshape(bp, nm, 2 * XATTN_WIDTH)
        outs["mem_p"].append(mkv_p.reshape(bp, nm, 2, XATTN_HEADS, HEAD_DIM))
        mkv_s = cache_mem_kv[i].reshape(bs, nm, 2 * XATTN_WIDTH)
        if i % 2 == 0:
            w = w_in_nsa[j]
            w = jnp.concatenate([w[:, perm], w[:, 768:1536], w[:, 1572:1828], w[:, 1536:1572],
                                 jnp.zeros((D_MODEL, NSA_IN_COLS - 1828), F32)], axis=1).astype(BF16)
            qg = tile2(q_gain_nsa[j]) * scale
            kg = tile2(k_gain_nsa[j])
            cmp_args = (jnp.tile(cmp_pos[j], (1, 1, 2))[:, :, None, :], _block_diag2(cmp_w1[j]).astype(BF16),
                        jnp.tile(cmp_b1[j], (1, 2))[:, None, :], _block_diag2(cmp_w2[j]).astype(BF16),
                        jnp.tile(cmp_b2[j], (1, 2))[:, None, :], kg, seg)
            wo_mix = wo[:768][perm].astype(BF16)
            qn, rows, win, katt, xqn_p, gates = _nsa_inproj(yp, g_mix[i][None], w, seg, qg, kg, xg, tm)
            cmp_p = _compress(rows.reshape(tp_ // 64, 64, 512), *cmp_args, nbt=512).reshape(bp, nc, 256)
            kvatt = jnp.pad(katt.reshape(bp, lp, 512), ((0, 0), (PAD_ROWS, 0), (0, 0)))
            mix_p = _nsa_prompt_attn(qn.reshape(bp, lp, 768), gates.reshape(bp, lp, 128), cmp_p, kvatt,
                                     oh, wb, ct, amat).reshape(tp_, 768)
            outs["kv_p"].append(rows.reshape(bp, lp, 4, NSA_KV_HEADS, HEAD_DIM))
            outs["win_p"].append(win.reshape(bp, lp, 2, NSA_KV_HEADS, HEAD_DIM)[:, -min(WINDOW, lp):])
            qn, rows, win, _, xqn_s, gates = _nsa_inproj(ys, g_mix[i][None], w, seg, qg, kg, xg, ts_)
            cache2 = cache_nsa_kv[j].reshape(n_phys, PAGE_SIZE, 512)
            cmp_phys = _compress(cache2.reshape(2 * n_phys, 64, 512), *cmp_args, nbt=1024)
            winc = cache_nsa_win[j].reshape(bs, wbuf, 256)
            mix_s = _nsa_sample_attn(page_table, qn.reshape(bs, ls, 768), gates.reshape(bs, ls, 128),
                                     rows.reshape(bs, ls, 512), winc, win.reshape(bs, ls, 256),
                                     scb, sbt, wbt, rmat, cmp_phys.reshape(n_phys, 2, 256), cache2).reshape(ts_, 768)
            outs["kv_s"].append(rows.reshape(bs, ls, 4, NSA_KV_HEADS, HEAD_DIM))
            wall = jnp.concatenate([cache_nsa_win[j], win.reshape(bs, ls, 2, NSA_KV_HEADS, HEAD_DIM)], axis=1)
            outs["win_s"].append(wall[:, -wbuf:])
        else:
            w = w_in_pool[j].astype(BF16)
            wbd = jnp.zeros((POOL_WIDTH, POOL_WIDTH), F32)
            for gi in range(len(POOL_WINDOWS)):
                sl = slice(gi * POOL_GROUP_DIM, (gi + 1) * POOL_GROUP_DIM)
                wbd = wbd.at[sl, sl].set(w_pool_grp[j, gi])
            wbd = wbd.astype(BF16)
            psc = pool_scale[j][None]
            wo_mix = wo[:768].astype(BF16)
            u_p, xqn_p = _pool_inproj(yp, g_mix[i][None], w, seg, xg, tm)
            u_p3 = u_p.reshape(bp, lp, POOL_WIDTH)
            mix_p = _pool_mix(u_p3, jnp.zeros((bp, 16, POOL_WIDTH), F32), wbd, psc, tm, 0).reshape(tp_, 768)
            outs["pool_p"].append(u_p3[:, -POOL_STATE:])
            u_s, xqn_s = _pool_inproj(ys, g_mix[i][None], w, seg, xg, ts_)
            u_s3 = u_s.reshape(bs, ls, POOL_WIDTH)
            prev16 = jnp.concatenate([jnp.zeros((bs, 1, POOL_WIDTH), F32), state_pool[j]], axis=1)
            mix_s = _pool_mix(u_s3, prev16, wbd, psc, ls, past_len).reshape(ts_, 768)
            outs["pool_s"].append(jnp.concatenate([state_pool[j], u_s3], axis=1)[:, -POOL_STATE:])
        wo_att = wo[768:].astype(BF16)
        att_p = _mem_attn(xqn_p.reshape(bp, lp, 256), mkv_p, tm).reshape(tp_, 256)
        att_s = _mem_attn(xqn_s.reshape(bs, ls, 256), mkv_s, ls).reshape(ts_, 256)
        yp = _outproj(yp, mix_p, att_p, wo_mix, wo_att, tm)
        ys = _outproj(ys, mix_s, att_s, wo_mix, wo_att, ts_)
        wu, wd = w_up[i].astype(BF16), w_down[i].astype(BF16)
        yp, c_p = _ffn_prompt(yp, g_ffn[i][None], wu, conv_w[i], conv_b[i][None], wd, tm, lp)
        ys, c_s = _ffn_sample(ys, g_ffn[i][None], wu, conv_w[i], conv_b[i][None], wd, state_ffn_conv[i])
        outs["conv_p"].append(c_p)
        outs["conv_s"].append(c_s)
    st = lambda k: jnp.stack(outs[k])
    return (yp.reshape(bp, lp, D_MODEL), ys.reshape(bs, ls, D_MODEL), st("kv_p"), st("kv_s"), st("win_p"),
            st("win_s"), st("mem_p"), st("pool_p"), st("pool_s"), st("conv_p"), st("conv_s"))
```

```python
import functools
import math

import numpy as np
import jax
import jax.numpy as jnp
from jax import lax
from jax.experimental import pallas as pl
from jax.experimental.pallas import tpu as pltpu

F32 = jnp.float32
BF16 = jnp.bfloat16

D_MODEL = 1024
PAGE_SIZE = 128
HEAD_DIM = 64
NSA_HEADS = 12
NSA_KV_HEADS = 2
NSA_GROUP = NSA_HEADS // NSA_KV_HEADS
NSA_WIDTH = NSA_HEADS * HEAD_DIM
NSA_KV_WIDTH = NSA_KV_HEADS * HEAD_DIM
CMP_BLOCK = 64
CMP_HIDDEN = 128
SLC_TOPN = 16
WINDOW = 512
FORCED_SCORE = 1e4
XATTN_HEADS = 4
XATTN_WIDTH = XATTN_HEADS * HEAD_DIM
POOL_WINDOWS = (2, 4, 8, 16)
POOL_GROUP_DIM = 192
POOL_WIDTH = len(POOL_WINDOWS) * POOL_GROUP_DIM
POOL_STATE = max(POOL_WINDOWS) - 1
D_FF = 2816
REL_BUCKETS = 32
REL_MAX_DIST = 128
EPS = 1e-6
BIG = 1e30

LANES = 128
PAD_ROWS = WINDOW
NSA_IN_COLS = 1920
FF_CHUNK = D_FF // 2
VMEM_LIMIT = 56 * 1024 * 1024


def _cparams(*sem):
    return pltpu.CompilerParams(dimension_semantics=sem, vmem_limit_bytes=VMEM_LIMIT)


def _nt(a, b):
    return lax.dot_general(a, b, (((1,), (1,)), ((), ())), preferred_element_type=F32)


def _tn(a, b):
    return lax.dot_general(a, b, (((0,), (0,)), ((), ())), preferred_element_type=F32)


def _dot(a, b):
    return jnp.dot(a, b, preferred_element_type=F32)


def _rms_rows(x, g):
    return x * lax.rsqrt(jnp.mean(x * x, axis=-1, keepdims=True) + EPS) * g


def _seg_rms(zc, seg, gain):
    zz = zc * zc
    hi = zz.astype(BF16)
    lo = (zz - hi.astype(F32)).astype(BF16)
    ms = (_dot(hi, seg) + _dot(lo, seg)) * (1.0 / HEAD_DIM)
    return zc * lax.rsqrt(ms + EPS) * gain


def _nsa_inproj_kernel(x_ref, g_ref, w_ref, seg_ref, qg_ref, kg_ref, xg_ref,
                       qn_ref, rows_ref, win_ref, katt_ref, xqn_ref, gates_ref):
    xn = _rms_rows(x_ref[...], g_ref[...])
    z = _dot(xn.astype(BF16), w_ref[...])
    seg = seg_ref[...]
    for h in range(6):
        qn_ref[:, h * LANES:(h + 1) * LANES] = _seg_rms(z[:, h * LANES:(h + 1) * LANES], seg, qg_ref[...]).astype(BF16)
    kv = 768
    rows_ref[:, 0:256] = z[:, kv:kv + 256]
    ks = _seg_rms(z[:, kv + 256:kv + 384], seg, kg_ref[...])
    vs = z[:, kv + 384:kv + 512]
    kw = _seg_rms(z[:, kv + 512:kv + 640], seg, kg_ref[...])
    vw = z[:, kv + 640:kv + 768]
    rows_ref[:, 256:384] = ks
    rows_ref[:, 384:512] = vs
    win_ref[:, 0:128] = kw
    win_ref[:, 128:256] = vw
    katt_ref[:, 0:128] = ks.astype(BF16)
    katt_ref[:, 128:256] = vs.astype(BF16)
    katt_ref[:, 256:384] = kw.astype(BF16)
    katt_ref[:, 384:512] = vw.astype(BF16)
    for h in range(2):
        c0 = 1536 + h * LANES
        xqn_ref[:, h * LANES:(h + 1) * LANES] = _seg_rms(z[:, c0:c0 + LANES], seg, xg_ref[...]).astype(BF16)
    gates_ref[...] = jax.nn.sigmoid(z[:, 1792:1920])


def _nsa_inproj(x, g, w, seg, qg, kg, xg, tm):
    t = x.shape[0]
    full = lambda a: pl.BlockSpec(a.shape, lambda i: (0,) * a.ndim)
    row = lambda n: pl.BlockSpec((tm, n), lambda i: (i, 0))
    return pl.pallas_call(
        _nsa_inproj_kernel,
        grid=(t // tm,),
        in_specs=[row(D_MODEL), full(g), full(w), full(seg), full(qg), full(kg), full(xg)],
        out_specs=[row(768), row(512), row(256), row(512), row(256), row(128)],
        out_shape=[jax.ShapeDtypeStruct((t, 768), BF16), jax.ShapeDtypeStruct((t, 512), F32),
                   jax.ShapeDtypeStruct((t, 256), F32), jax.ShapeDtypeStruct((t, 512), BF16),
                   jax.ShapeDtypeStruct((t, 256), BF16), jax.ShapeDtypeStruct((t, 128), F32)],
        compiler_params=_cparams("parallel"),
    )(x, g, w, seg, qg, kg, xg)


def _pool_inproj_kernel(x_ref, g_ref, w_ref, seg_ref, xg_ref, u_ref, xqn_ref):
    xn = _rms_rows(x_ref[...], g_ref[...])
    z = _dot(xn.astype(BF16), w_ref[...])
    u_ref[...] = z[:, 0:POOL_WIDTH]
    seg = seg_ref[...]
    for h in range(2):
        c0 = POOL_WIDTH + h * LANES
        xqn_ref[:, h * LANES:(h + 1) * LANES] = _seg_rms(z[:, c0:c0 + LANES], seg, xg_ref[...]).astype(BF16)


def _pool_inproj(x, g, w, seg, xg, tm):
    t = x.shape[0]
    full = lambda a: pl.BlockSpec(a.shape, lambda i: (0,) * a.ndim)
    row = lambda n: pl.BlockSpec((tm, n), lambda i: (i, 0))
    return pl.pallas_call(
        _pool_inproj_kernel,
        grid=(t // tm,),
        in_specs=[row(D_MODEL), full(g), full(w), full(seg), full(xg)],
        out_specs=[row(POOL_WIDTH), row(256)],
        out_shape=[jax.ShapeDtypeStruct((t, POOL_WIDTH), F32), jax.ShapeDtypeStruct((t, 256), BF16)],
        compiler_params=_cparams("parallel"),
    )(x, g, w, seg, xg)


def _memkv_kernel(x_ref, g_ref, w_ref, seg_ref, kg_ref, o_ref):
    xn = _rms_rows(x_ref[...], g_ref[...])
    z = _dot(xn.astype(BF16), w_ref[...])
    seg = seg_ref[...]
    for h in range(2):
        o_ref[:, h * LANES:(h + 1) * LANES] = _seg_rms(z[:, h * LANES:(h + 1) * LANES], seg, kg_ref[...])
    o_ref[:, 256:512] = z[:, 256:512]


def _memkv(x, g, w, seg, kg, tm):
    t = x.shape[0]
    full = lambda a: pl.BlockSpec(a.shape, lambda i: (0,) * a.ndim)
    row = lambda n: pl.BlockSpec((tm, n), lambda i: (i, 0))
    return pl.pallas_call(
        _memkv_kernel,
        grid=(t // tm,),
        in_specs=[row(D_MODEL), full(g), full(w), full(seg), full(kg)],
        out_specs=row(512),
        out_shape=jax.ShapeDtypeStruct((t, 512), F32),
        compiler_params=_cparams("parallel"),
    )(x, g, w, seg, kg)


def _compress_kernel(x_ref, pos_ref, w1_ref, b1_ref, w2_ref, b2_ref, kg_ref, seg_ref, o_ref, acc_ref):
    lc = pl.program_id(1)

    @pl.when(lc == 0)
    def _():
        acc_ref[...] = jnp.zeros_like(acc_ref)

    for comp in range(2):
        part = None
        for l in range(8):
            xl = x_ref[:, l, comp * LANES:(comp + 1) * LANES] + pos_ref[comp, l]
            d = _dot(xl.astype(BF16), w1_ref[comp, l])
            part = d if part is None else part + d
        acc_ref[:, comp * 256:(comp + 1) * 256] += part

    @pl.when(lc == pl.num_programs(1) - 1)
    def _():
        for comp in range(2):
            h = jax.nn.gelu(acc_ref[:, comp * 256:(comp + 1) * 256] + b1_ref[comp])
            o = _dot(h.astype(BF16), w2_ref[comp]) + b2_ref[comp]
            if comp == 0:
                o = _seg_rms(o, seg_ref[...], kg_ref[...])
            o_ref[:, comp * LANES:(comp + 1) * LANES] = o


def _compress(x3, pos_t, w1bd, b1t, w2bd, b2t, kg, seg, nbt):
    nb, _, w = x3.shape
    nbt = min(nbt, nb)
    assert nb % nbt == 0
    full = lambda a: pl.BlockSpec(a.shape, lambda j, l: (0,) * a.ndim)
    return pl.pallas_call(
        _compress_kernel,
        grid=(nb // nbt, 8),
        in_specs=[pl.BlockSpec((nbt, 8, 256), lambda j, l: (j, l, 0)),
                  pl.BlockSpec((2, 8, 1, LANES), lambda j, l: (0, l, 0, 0)),
                  pl.BlockSpec((2, 8, LANES, 256), lambda j, l: (0, l, 0, 0)),
                  full(b1t), full(w2bd), full(b2t), full(kg), full(seg)],
        out_specs=pl.BlockSpec((nbt, 256), lambda j, l: (j, 0)),
        out_shape=jax.ShapeDtypeStruct((nb, 256), F32),
        scratch_shapes=[pltpu.VMEM((nbt, 512), F32)],
        compiler_params=_cparams("parallel", "arbitrary"),
    )(x3, pos_t, w1bd, b1t, w2bd, b2t, kg, seg)


def _compress_cache_kernel(x_ref, pos_ref, w1_ref, b1_ref, w2_ref, b2_ref, kg_ref, seg_ref, o_ref, acc_ref):
    dc = pl.program_id(1)

    @pl.when(dc == 0)
    def _():
        acc_ref[...] = jnp.zeros_like(acc_ref)

    for comp in range(2):
        for g in range(2):
            part = None
            for dd in range(8):
                xd = x_ref[:, comp, g, dd, :] + pos_ref[comp, dd]
                d = _dot(xd.astype(BF16), w1_ref[comp, dd])
                part = d if part is None else part + d
            c0 = (comp * 2 + g) * 256
            acc_ref[:, c0:c0 + 256] += part

    @pl.when(dc == pl.num_programs(1) - 1)
    def _():
        for comp in range(2):
            hid = [jax.nn.gelu(acc_ref[:, (comp * 2 + g) * 256:(comp * 2 + g + 1) * 256] + b1_ref[comp]).astype(BF16)
                   for g in range(2)]
            for blk in range(2):
                o = _dot(hid[0], w2_ref[comp, blk, 0]) + _dot(hid[1], w2_ref[comp, blk, 1]) + b2_ref[comp]
                if comp == 0:
                    o = _seg_rms(o, seg_ref[...], kg_ref[...])
                o_ref[:, blk * 256 + comp * LANES:blk * 256 + (comp + 1) * LANES] = o


def _compress_cache(xt, pos_t, w1t, b1t, w2sel, b2t, kg, seg, pt):
    n_phys = xt.shape[0]
    pt = min(pt, n_phys)
    assert n_phys % pt == 0
    full = lambda a: pl.BlockSpec(a.shape, lambda j, d: (0,) * a.ndim)
    return pl.pallas_call(
        _compress_cache_kernel,
        grid=(n_phys // pt, 8),
        in_specs=[pl.BlockSpec((pt, 2, 2, 8, LANES), lambda j, d: (j, 0, 0, d, 0)),
                  pl.BlockSpec((2, 8, 1, LANES), lambda j, d: (0, d, 0, 0)),
                  pl.BlockSpec((2, 8, LANES, 256), lambda j, d: (0, d, 0, 0)),
                  full(b1t), full(w2sel), full(b2t), full(kg), full(seg)],
        out_specs=pl.BlockSpec((pt, 512), lambda j, d: (j, 0)),
        out_shape=jax.ShapeDtypeStruct((n_phys, 512), F32),
        scratch_shapes=[pltpu.VMEM((pt, 1024), F32)],
        compiler_params=_cparams("parallel", "arbitrary"),
    )(xt, pos_t, w1t, b1t, w2sel, b2t, kg, seg)


def _bucket_np(d):
    n = np.maximum(d, 0)
    max_exact = REL_BUCKETS // 2
    nf = np.maximum(n, 1).astype(np.float32)
    large = max_exact + (np.log(nf / max_exact) / math.log(REL_MAX_DIST / max_exact)
                         * (REL_BUCKETS - max_exact)).astype(np.int32)
    large = np.minimum(large, REL_BUCKETS - 1)
    return np.where(n < max_exact, n, large).astype(np.int32)


def _bias_table_kernel(rb_ref, code_ref, o_ref):
    code = code_ref[...]
    for h in range(NSA_HEADS):
        far = rb_ref[REL_BUCKETS - 1, h]
        acc = jnp.full(code.shape, -BIG, F32)
        for k in range(REL_BUCKETS):
            acc = jnp.where(code == k, rb_ref[k, h] - far, acc)
        o_ref[h] = acc


def _bias_table(rel_bias, code):
    code = jnp.asarray(code, jnp.int32)
    return pl.pallas_call(
        _bias_table_kernel,
        in_specs=[pl.BlockSpec(memory_space=pltpu.SMEM), pl.BlockSpec(code.shape, lambda: (0, 0))],
        out_specs=pl.BlockSpec((NSA_HEADS,) + code.shape, lambda: (0, 0, 0)),
        out_shape=jax.ShapeDtypeStruct((NSA_HEADS,) + code.shape, F32),
    )(rel_bias, code)


def _window_codes():
    ql = np.arange(64)[:, None]
    j = np.arange(WINDOW + 64)[None, :]
    d = WINDOW + ql - j
    return np.where((d >= 0) & (d <= WINDOW), _bucket_np(d), -1).astype(np.int32)


def _cmp_codes():
    delta = np.arange(8)[:, None]
    ql = np.arange(64)[None, :]
    d = delta * CMP_BLOCK + ql - (CMP_BLOCK - 1)
    return np.where(d >= 0, _bucket_np(d), -1).astype(np.int32)


def _softmax_update(s, v, m, l, acc):
    m_new = jnp.maximum(m, jnp.max(s, axis=1, keepdims=True))
    a = jnp.exp(m - m_new)
    p = jnp.exp(s - m_new)
    l = a * l + jnp.sum(p, axis=1, keepdims=True)
    acc = a * acc + _dot(p.astype(BF16), v)
    return m_new, l, acc


def _nsa_prompt_kernel(qn_ref, gates_ref, cmp_ref, kv_ref, oh_ref, wb_ref, ct_ref, amat_ref, o_ref):
    i = pl.program_id(1)
    nc = cmp_ref.shape[1]
    rows = NSA_GROUP * 64
    q = qn_ref[0].astype(F32)
    gts = gates_ref[0]
    kc = cmp_ref[0, :, 0:128].astype(BF16)
    vc = cmp_ref[0, :, 128:256].astype(BF16)
    lane = lax.broadcasted_iota(jnp.int32, (64, LANES), 1)
    lane_r = lax.broadcasted_iota(jnp.int32, (rows, LANES), 1)
    half = [lane < 64, lane >= 64]
    qg = [jnp.concatenate([jnp.where(half[g], q[:, h * LANES:(h + 1) * LANES], 0.0) for h in range(NSA_GROUP)],
                          axis=0).astype(BF16) for g in range(2)]

    o_c, imp = [], []
    blk_r = lax.broadcasted_iota(jnp.int32, (nc, rows), 0)
    for g in range(2):
        ctg = ct_ref[g]
        bias = jnp.where(blk_r == i, ctg[0:1], jnp.where(blk_r == i - 1, ctg[1:2], jnp.where(
            blk_r == i - 2, ctg[2:3], jnp.where(blk_r > i, -BIG, 0.0))))
        lc = _nt(kc, qg[g]) + bias
        e = jnp.exp(lc - jnp.max(lc, axis=0, keepdims=True))
        p = jnp.where(bias > -0.5 * BIG, e / jnp.sum(e, axis=0, keepdims=True), 0.0)
        o_c.append(_tn(p.astype(BF16), vc))
        s3 = p[:, 0:128] + p[:, 128:256] + p[:, 256:384]
        imp.append(s3 + pltpu.roll(s3, 64, axis=1))
    lane_c = lax.broadcasted_iota(jnp.int32, (nc, LANES), 1)
    blk = lax.broadcasted_iota(jnp.int32, (nc, LANES), 0)
    impp = jnp.where(lane_c < 64, imp[0], imp[1])
    forced = (blk == 0) | (blk == i) | (blk == i - 1)
    score = jnp.where(blk <= i, jnp.where(forced, FORCED_SCORE, impp), -1.0)
    rank = jnp.zeros((nc, LANES), F32)
    for cp in range(nc):
        row = score[cp:cp + 1, :]
        beats = (row > score) | ((row == score) & (blk > cp))
        rank = rank + jnp.where(beats, 1.0, 0.0)
    notsel = jnp.where((rank < float(min(SLC_TOPN, nc))) & (score >= 0.0), 0.0, 1.0)
    notsel_far = jnp.where(blk > i - 3, 1.0, notsel)
    tail = jnp.where(lax.broadcasted_iota(jnp.int32, (LANES - nc, LANES), 0) == 64 - nc, 1.0, 0.0)
    ns_near = jnp.concatenate([notsel, tail], axis=0).astype(BF16)
    ns_far = jnp.concatenate([notsel_far, tail], axis=0).astype(BF16)
    m_win = jnp.where(lane_r == 64, -BIG, 0.0).astype(BF16)

    outs = []
    for g in range(2):
        a = amat_ref[g]
        m_near = (_nt(a, ns_near) * -BIG).astype(BF16)
        m_far = (_nt(a, ns_far) * -BIG).astype(BF16)
        lhs_far = jnp.concatenate([qg[g], m_far], axis=1)
        lhs_near = jnp.concatenate([qg[g], m_near], axis=1)
        lhs_win = jnp.concatenate([qg[g], m_win], axis=1)

        def far_body(t, carry):
            r0 = pl.multiple_of(PAD_ROWS + t * 256, 256)
            k = jnp.concatenate([kv_ref[0, pl.ds(r0, 256), 0:128], oh_ref[pl.ds(r0, 256), :]], axis=1)
            v = kv_ref[0, pl.ds(r0, 256), 128:256]
            return _softmax_update(_nt(lhs_far, k), v, *carry)

        init = (jnp.full((rows, 1), -BIG, F32), jnp.zeros((rows, 1), F32), jnp.zeros((rows, LANES), F32))
        carry = lax.fori_loop(0, (i + 1) // 4, far_body, init)
        r0 = pl.multiple_of(PAD_ROWS + (i - 2) * 64, 64)
        k = jnp.concatenate([kv_ref[0, pl.ds(r0, 192), 0:128], oh_ref[pl.ds(r0, 192), :]], axis=1)
        v = kv_ref[0, pl.ds(r0, 192), 128:256]
        s = _nt(lhs_near, k) + wb_ref[g, :, 384:576]
        _, l, acc = _softmax_update(s, v, *carry)
        o_s = acc / l

        r0 = pl.multiple_of(PAD_ROWS + (i - 8) * 64, 64)
        k = jnp.concatenate([kv_ref[0, pl.ds(r0, 576), 256:384], oh_ref[pl.ds(r0, 576), :]], axis=1)
        v = kv_ref[0, pl.ds(r0, 576), 384:512]
        s = _nt(lhs_win, k) + wb_ref[g]
        e = jnp.exp(s - jnp.max(s, axis=1, keepdims=True))
        o_w = _dot(e.astype(BF16), v) / jnp.sum(e, axis=1, keepdims=True)

        per_head = []
        for h in range(NSA_GROUP):
            c0 = (g * NSA_GROUP + h) * 3
            r = slice(h * 64, (h + 1) * 64)
            per_head.append(gts[:, c0:c0 + 1] * o_c[g][r] + gts[:, c0 + 1:c0 + 2] * o_s[r]
                            + gts[:, c0 + 2:c0 + 3] * o_w[r])
        outs.append(per_head)
    low = lax.broadcasted_iota(jnp.int32, (64, LANES), 1) < 64
    for h in range(NSA_GROUP):
        o_ref[0, :, h * LANES:(h + 1) * LANES] = jnp.where(low, outs[0][h], outs[1][h]).astype(BF16)


def _nsa_prompt_attn(qn, gates, cmp, kvatt, oh, wb, ct, amat):
    b, l, _ = qn.shape
    nc = l // 64
    full = lambda a: pl.BlockSpec(a.shape, lambda bi, i: (0,) * a.ndim)
    return pl.pallas_call(
        _nsa_prompt_kernel,
        grid=(b, nc),
        in_specs=[pl.BlockSpec((1, 64, 768), lambda bi, i: (bi, i, 0)),
                  pl.BlockSpec((1, 64, 128), lambda bi, i: (bi, i, 0)),
                  pl.BlockSpec((1, nc, 256), lambda bi, i: (bi, 0, 0)),
                  pl.BlockSpec((1, PAD_ROWS + l, 512), lambda bi, i: (bi, 0, 0)),
                  full(oh), full(wb), full(ct), full(amat)],
        out_specs=pl.BlockSpec((1, 64, 768), lambda bi, i: (bi, i, 0)),
        out_shape=jax.ShapeDtypeStruct((b, l, 768), BF16),
        compiler_params=_cparams("parallel", "arbitrary"),
    )(qn, gates, cmp, kvatt, oh, wb, ct, amat)


def _mem_attn_kernel(q_ref, kv_ref, o_ref):
    q = q_ref[0].astype(F32)
    k = kv_ref[0, :, 0:256].astype(BF16)
    v = kv_ref[0, :, 256:512].astype(BF16)
    head = lax.broadcasted_iota(jnp.int32, q.shape, 1) // HEAD_DIM
    out = jnp.zeros(q.shape, F32)
    for h in range(XATTN_HEADS):
        qh = jnp.where(head == h, q, 0.0).astype(BF16)
        s = _nt(qh, k)
        e = jnp.exp(s - jnp.max(s, axis=1, keepdims=True))
        p = e / jnp.sum(e, axis=1, keepdims=True)
        out = out + jnp.where(head == h, _dot(p.astype(BF16), v), 0.0)
    o_ref[0] = out.astype(BF16)


def _mem_attn(xqn, mkv, tq):
    b, l, _ = xqn.shape
    nm = mkv.shape[1]
    return pl.pallas_call(
        _mem_attn_kernel,
        grid=(b, l // tq),
        in_specs=[pl.BlockSpec((1, tq, 256), lambda bi, i: (bi, i, 0)),
                  pl.BlockSpec((1, nm, 512), lambda bi, i: (bi, 0, 0))],
        out_specs=pl.BlockSpec((1, tq, 256), lambda bi, i: (bi, i, 0)),
        out_shape=jax.ShapeDtypeStruct((b, l, 256), BF16),
        compiler_params=_cparams("parallel", "parallel"),
    )(xqn, mkv)


def _pool_kernel(u_ref, prev_ref, w_ref, scale_ref, o_ref, carry_ref, *, pos0):
    i = pl.program_id(1)
    tp = u_ref.shape[1]

    @pl.when(i == 0)
    def _():
        carry_ref[...] = prev_ref[0]

    u = u_ref[0]
    ext = jnp.concatenate([carry_ref[...], u], axis=0)
    carry_ref[...] = ext[tp:tp + 16]
    s2 = ext + pltpu.roll(ext, 1, axis=0)
    s4 = s2 + pltpu.roll(s2, 2, axis=0)
    s8 = s4 + pltpu.roll(s4, 4, axis=0)
    s16 = s8 + pltpu.roll(s8, 8, axis=0)
    lane = lax.broadcasted_iota(jnp.int32, (tp, POOL_WIDTH), 1)
    pos1 = (pos0 + 1 + i * tp + lax.broadcasted_iota(jnp.int32, (tp, POOL_WIDTH), 0)).astype(F32)
    grp = lane // POOL_GROUP_DIM
    ssum = jnp.where(grp == 0, s2[16:], jnp.where(grp == 1, s4[16:], jnp.where(grp == 2, s8[16:], s16[16:])))
    win = jnp.where(grp == 0, 2.0, jnp.where(grp == 1, 4.0, jnp.where(grp == 2, 8.0, 16.0)))
    pooled = ssum / jnp.minimum(win, pos1) - u
    o_ref[0] = (_dot(pooled.astype(BF16), w_ref[...]) * scale_ref[...]).astype(BF16)


def _pool_mix(u, prev16, wbd, scale, tp, pos0):
    b, l, _ = u.shape
    full = lambda a: pl.BlockSpec(a.shape, lambda bi, i: (0,) * a.ndim)
    return pl.pallas_call(
        functools.partial(_pool_kernel, pos0=pos0),
        grid=(b, l // tp),
        in_specs=[pl.BlockSpec((1, tp, POOL_WIDTH), lambda bi, i: (bi, i, 0)),
                  pl.BlockSpec((1, 16, POOL_WIDTH), lambda bi, i: (bi, 0, 0)),
                  full(wbd), full(scale)],
        out_specs=pl.BlockSpec((1, tp, POOL_WIDTH), lambda bi, i: (bi, i, 0)),
        out_shape=jax.ShapeDtypeStruct((b, l, POOL_WIDTH), BF16),
        scratch_shapes=[pltpu.VMEM((16, POOL_WIDTH), F32)],
        compiler_params=_cparams("parallel", "arbitrary"),
    )(u, prev16, wbd, scale)


def _outproj_kernel(y_ref, mix_ref, att_ref, wm_ref, wa_ref, o_ref):
    o_ref[...] = y_ref[...] + _dot(mix_ref[...], wm_ref[...]) + _dot(att_ref[...], wa_ref[...])


def _outproj(y, mix, att, wm, wa, tm):
    t = y.shape[0]
    full = lambda a: pl.BlockSpec(a.shape, lambda i: (0,) * a.ndim)
    row = lambda n: pl.BlockSpec((tm, n), lambda i: (i, 0))
    return pl.pallas_call(
        _outproj_kernel,
        grid=(t // tm,),
        in_specs=[row(D_MODEL), row(768), row(256), full(wm), full(wa)],
        out_specs=row(D_MODEL),
        out_shape=jax.ShapeDtypeStruct((t, D_MODEL), F32),
        compiler_params=_cparams("parallel"),
    )(y, mix, att, wm, wa)


def _ffn_half(h, w_ref, cw_ref, cb_ref, ext_ref, prev8, fix):
    tm = h.shape[0]
    u = _dot(h, w_ref[...])
    ext_ref[0:8, :] = prev8
    ext_ref[8:8 + tm, :] = u
    s1 = ext_ref[7:7 + tm, :]
    s2 = ext_ref[6:6 + tm, :]
    if fix is not None:
        rowm, p1, p2 = fix
        s1 = jnp.where(rowm >= 1, s1, p1)
        s2 = jnp.where(rowm >= 2, s2, p2)
    cw = cw_ref[...]
    return u, cb_ref[...] + cw[0:1] * s2 + cw[1:2] * s1 + cw[2:3] * u


def _ffn_prompt_kernel(y_ref, g_ref, wa_ref, wv_ref, cwa_ref, cwv_ref, cba_ref, cbv_ref, wd_ref,
                       o_ref, sa_ref, sv_ref, h_ref, acc_ref, ext_ref, carry_ref, *, tiles_per_seq):
    j = pl.program_id(0)
    c = pl.program_id(1)
    tm = y_ref.shape[0]

    @pl.when(c == 0)
    def _():
        h_ref[...] = _rms_rows(y_ref[...], g_ref[...]).astype(BF16)

    @pl.when((j % tiles_per_seq) == 0)
    def _():
        carry_ref[c] = jnp.zeros(carry_ref.shape[1:], F32)

    h = h_ref[...]
    ua, ya = _ffn_half(h, wa_ref, cwa_ref, cba_ref, ext_ref, carry_ref[c, 0], None)
    carry_ref[c, 0] = ua[tm - 8:tm]
    sa_ref[0] = ua[tm - 8:tm]
    uv, yv = _ffn_half(h, wv_ref, cwv_ref, cbv_ref, ext_ref, carry_ref[c, 1], None)
    carry_ref[c, 1] = uv[tm - 8:tm]
    sv_ref[0] = uv[tm - 8:tm]
    d = _dot((jax.nn.silu(ya) * yv).astype(BF16), wd_ref[...])

    @pl.when(c == 0)
    def _():
        acc_ref[...] = d

    @pl.when(c == pl.num_programs(1) - 1)
    def _():
        o_ref[...] = y_ref[...] + acc_ref[...] + d


def _ffn_prompt(y, g, w_up, conv_w, conv_b, w_down, tm, seq_len):
    t = y.shape[0]
    nseq = t // seq_len
    tps = seq_len // tm
    fc = FF_CHUNK
    ncf = D_FF // fc
    full = lambda a: pl.BlockSpec(a.shape, lambda j, c: (0,) * a.ndim)
    outs = pl.pallas_call(
        functools.partial(_ffn_prompt_kernel, tiles_per_seq=tps),
        grid=(t // tm, ncf),
        in_specs=[pl.BlockSpec((tm, D_MODEL), lambda j, c: (j, 0)), full(g),
                  pl.BlockSpec((D_MODEL, fc), lambda j, c: (0, c)),
                  pl.BlockSpec((D_MODEL, fc), lambda j, c: (0, ncf + c)),
                  pl.BlockSpec((3, fc), lambda j, c: (0, c)),
                  pl.BlockSpec((3, fc), lambda j, c: (0, ncf + c)),
                  pl.BlockSpec((1, fc), lambda j, c: (0, c)),
                  pl.BlockSpec((1, fc), lambda j, c: (0, ncf + c)),
                  pl.BlockSpec((fc, D_MODEL), lambda j, c: (c, 0))],
        out_specs=[pl.BlockSpec((tm, D_MODEL), lambda j, c: (j, 0)),
                   pl.BlockSpec((1, 8, fc), lambda j, c: (j, 0, c)),
                   pl.BlockSpec((1, 8, fc), lambda j, c: (j, 0, c))],
        out_shape=[jax.ShapeDtypeStruct((t, D_MODEL), F32),
                   jax.ShapeDtypeStruct((t // tm, 8, D_FF), F32),
                   jax.ShapeDtypeStruct((t // tm, 8, D_FF), F32)],
        scratch_shapes=[pltpu.VMEM((tm, D_MODEL), BF16), pltpu.VMEM((tm, D_MODEL), F32),
                        pltpu.VMEM((tm + 8, fc), F32), pltpu.VMEM((ncf, 2, 8, fc), F32)],
        compiler_params=_cparams("arbitrary", "arbitrary"),
    )(y, g, w_up, w_up, conv_w, conv_w, conv_b, conv_b, w_down)
    y_out, sa, sv = outs
    last = lambda s: s.reshape(nseq, tps, 8, D_FF)[:, tps - 1, 6:8]
    state = jnp.concatenate([last(sa), last(sv)], axis=-1)
    return y_out, state


def _ffn_sample_kernel(y_ref, g_ref, wa_ref, wv_ref, cwa_ref, cwv_ref, cba_ref, cbv_ref, wd_ref,
                       p1a_ref, p1v_ref, p2a_ref, p2v_ref, o_ref, ua_ref, uv_ref, acc_ref, ext_ref):
    c = pl.program_id(0)
    h = _rms_rows(y_ref[...], g_ref[...]).astype(BF16)
    tm = h.shape[0]
    rowm = lax.broadcasted_iota(jnp.int32, (tm, wa_ref.shape[1]), 0) % 8
    zero8 = jnp.zeros((8, wa_ref.shape[1]), F32)
    ua, ya = _ffn_half(h, wa_ref, cwa_ref, cba_ref, ext_ref, zero8, (rowm, p1a_ref[...], p2a_ref[...]))
    uv, yv = _ffn_half(h, wv_ref, cwv_ref, cbv_ref, ext_ref, zero8, (rowm, p1v_ref[...], p2v_ref[...]))
    ua_ref[...] = ua
    uv_ref[...] = uv
    d = _dot((jax.nn.silu(ya) * yv).astype(BF16), wd_ref[...])

    @pl.when(c == 0)
    def _():
        acc_ref[...] = d

    @pl.when(c == pl.num_programs(0) - 1)
    def _():
        o_ref[...] = y_ref[...] + acc_ref[...] + d


def _ffn_sample(y, g, w_up, conv_w, conv_b, w_down, state):
    t = y.shape[0]
    nseq = t // 8
    fc = FF_CHUNK
    ncf = D_FF // fc
    z = jnp.zeros((nseq, 1, 2 * D_FF), F32)
    p1 = jnp.concatenate([state[:, 1:2]] + [z] * 7, axis=1).reshape(t, 2 * D_FF)
    p2 = jnp.concatenate([state[:, 0:1], state[:, 1:2]] + [z] * 6, axis=1).reshape(t, 2 * D_FF)
    full = lambda a: pl.BlockSpec(a.shape, lambda c: (0,) * a.ndim)
    ca = lambda r: pl.BlockSpec((r, fc), lambda c: (0, c))
    cv = lambda r: pl.BlockSpec((r, fc), lambda c: (0, ncf + c))
    y_out, ua, uv = pl.pallas_call(
        _ffn_sample_kernel,
        grid=(ncf,),
        in_specs=[full(y), full(g), ca(D_MODEL), cv(D_MODEL), ca(3), cv(3), ca(1), cv(1),
                  pl.BlockSpec((fc, D_MODEL), lambda c: (c, 0)), ca(t), cv(t), ca(t), cv(t)],
        out_specs=[pl.BlockSpec((t, D_MODEL), lambda c: (0, 0)), ca(t), ca(t)],
        out_shape=[jax.ShapeDtypeStruct((t, D_MODEL), F32), jax.ShapeDtypeStruct((t, D_FF), F32),
                   jax.ShapeDtypeStruct((t, D_FF), F32)],
        scratch_shapes=[pltpu.VMEM((t, D_MODEL), F32), pltpu.VMEM((t + 8, fc), F32)],
        compiler_params=_cparams("arbitrary"),
    )(y, g, w_up, w_up, conv_w, conv_w, conv_b, conv_b, w_down, p1, p1, p2, p2)
    u = jnp.concatenate([ua, uv], axis=-1).reshape(nseq, 8, 2 * D_FF)
    return y_out, u[:, 6:8]


KEY_CHUNK = 1024
NEW_PAD = 16


def _nsa_sample_kernel(pt_ref, qn_ref, gates_ref, rows_ref, winc_ref, winn_ref, scb_ref, sbt_ref, wbt_ref,
                       rmat_ref, eloc_ref, cmp_hbm, cache_hbm, o_ref, kvbuf, kcv, s_ref, score_ref, m3_ref,
                       sem, sem2):
    b = pl.program_id(0)
    npages = pt_ref.shape[1]
    ncs = 2 * npages
    nkeys = npages * PAGE_SIZE
    nch = nkeys // KEY_CHUNK
    ppc = KEY_CHUNK // PAGE_SIZE
    wbuf = winc_ref.shape[-1]
    gq = NSA_GROUP * 8

    def page_copy(p):
        return pltpu.make_async_copy(cache_hbm.at[pt_ref[b, p], pl.ds(2, 2)], kvbuf.at[p], sem)

    def cmp_copy(p):
        return pltpu.make_async_copy(cmp_hbm.at[pt_ref[b, p]], kcv.at[p], sem2)

    def start_all(p, c):
        cmp_copy(p).start()
        page_copy(p).start()
        return c

    lax.fori_loop(0, npages, start_all, 0)

    q = qn_ref[0].astype(F32)
    gts = gates_ref[0]
    lane8 = lax.broadcasted_iota(jnp.int32, (8, LANES), 1)
    half = [lane8 < 64, lane8 >= 64]
    pieces = [jnp.where(half[g], q[:, h * LANES:(h + 1) * LANES], 0.0) for g in range(2) for h in range(NSA_GROUP)]
    qrows = jnp.concatenate(pieces + [jnp.zeros((32, LANES), F32)], axis=0).astype(BF16)

    def wait_cmp(p, c):
        cmp_copy(p).wait()
        return c

    lax.fori_loop(0, npages, wait_cmp, 0)

    kc = jnp.concatenate([kcv[:, 0, 0:128], kcv[:, 1, 0:128]], axis=0).astype(BF16)
    vc = jnp.concatenate([kcv[:, 0, 128:256], kcv[:, 1, 128:256]], axis=0).astype(BF16)
    lc = _nt(kc, qrows) + scb_ref[...]
    e = jnp.exp(lc - jnp.max(lc, axis=0, keepdims=True))
    p_c = e / jnp.sum(e, axis=0, keepdims=True)
    o_c = _tn(p_c.astype(BF16), vc)
    hi = p_c.astype(BF16)
    lo = (p_c - hi.astype(F32)).astype(BF16)
    imp = _dot(hi, rmat_ref[...]) + _dot(lo, rmat_ref[...])

    def blk_of(r):
        return jnp.where(r < npages, 2 * r, jnp.where(r < ncs, 2 * (r - npages) + 1, r))

    blk = blk_of(lax.broadcasted_iota(jnp.int32, (ncs + 8, LANES), 0))
    impx = jnp.concatenate([imp, jnp.zeros((8, LANES), F32)], axis=0)
    forced = (blk == 0) | (blk == ncs) | (blk == ncs - 1)
    score = jnp.where(blk <= ncs, jnp.where(forced, FORCED_SCORE, impx), -1.0)
    for r in range(ncs + 1):
        score_ref[r] = score[r:r + 1, :]

    def rank_body(rp, rank):
        row = score_ref[rp]
        beats = (row > score) | ((row == score) & (blk > blk_of(rp)))
        return rank + jnp.where(beats, 1.0, 0.0)

    rank = lax.fori_loop(0, ncs + 1, rank_body, jnp.zeros((ncs + 8, LANES), F32))
    madd = jnp.where((rank < float(SLC_TOPN)) & (score >= 0.0), 0.0, -BIG)
    eye = jnp.where(lax.broadcasted_iota(jnp.int32, (LANES, LANES), 0)
                    == lax.broadcasted_iota(jnp.int32, (LANES, LANES), 1), 1.0, 0.0).astype(BF16)
    hb = ppc
    for t in range(nch):
        mt = jnp.concatenate([madd[t * hb:(t + 1) * hb], madd[npages + t * hb:npages + (t + 1) * hb],
                              jnp.zeros((LANES - 2 * hb, LANES), F32)], axis=0).astype(BF16)
        m3_ref[t] = _nt(eye, mt).astype(BF16)

    def wait_page(p, c):
        page_copy(p).wait()
        return c

    lax.fori_loop(0, npages, wait_page, 0)

    eloc = eloc_ref[...]
    pad8 = jnp.zeros((NEW_PAD - 8, LANES), F32)
    knew = jnp.concatenate([rows_ref[0, :, 256:384], pad8], axis=0).astype(BF16)
    vnew = jnp.concatenate([rows_ref[0, :, 384:512], pad8], axis=0).astype(BF16)
    kwn = jnp.concatenate([winn_ref[0, :, 0:128], pad8], axis=0).astype(BF16)
    vwn = jnp.concatenate([winn_ref[0, :, 128:256], pad8], axis=0).astype(BF16)
    kwt = winc_ref[0, 0].reshape(LANES, wbuf).astype(BF16)
    vwt = winc_ref[0, 1].reshape(LANES, wbuf).astype(BF16)
    near_pad = jnp.concatenate([jnp.zeros((LANES, KEY_CHUNK - PAGE_SIZE), F32), sbt_ref[:, 0:PAGE_SIZE]], axis=1)
    low = lax.broadcasted_iota(jnp.int32, (8, LANES), 1) < 64

    def page_rows(t, comp):
        tiles = [kvbuf[t * ppc + pp, comp].reshape(LANES, PAGE_SIZE) for pp in range(ppc)]
        return jnp.concatenate(tiles, axis=1).astype(BF16)

    og = []
    for g in range(2):
        rs = slice(g * gq, (g + 1) * gq)
        qz = jnp.concatenate(pieces[g * NSA_GROUP:(g + 1) * NSA_GROUP], axis=0).astype(BF16)
        nearb = near_pad[rs]

        def s_body(t, m):
            lhs = jnp.concatenate([m3_ref[t, rs, :], qz], axis=1)
            s = _dot(lhs, jnp.concatenate([eloc, page_rows(t, 0)], axis=0))
            s = s + jnp.where(t == nch - 1, 1.0, 0.0) * nearb
            s_ref[g, t] = s
            return jnp.maximum(m, jnp.max(s, axis=1, keepdims=True))

        s_new = _nt(qz, knew) + sbt_ref[rs, PAGE_SIZE:PAGE_SIZE + NEW_PAD]
        m = lax.fori_loop(0, nch, s_body, jnp.max(s_new, axis=1, keepdims=True))
        e_new = jnp.exp(s_new - m)

        def pv_body(t, carry):
            l, acc = carry
            et = jnp.exp(s_ref[g, t] - m)
            return l + jnp.sum(et, axis=1, keepdims=True), acc + _nt(et.astype(BF16), page_rows(t, 1))

        l, acc = lax.fori_loop(0, nch, pv_body, (jnp.sum(e_new, axis=1, keepdims=True),
                                                 _dot(e_new.astype(BF16), vnew)))
        o_s = acc / l

        sw = _dot(qz, kwt) + wbt_ref[rs, 0:wbuf]
        swn = _nt(qz, kwn) + wbt_ref[rs, wbuf:wbuf + NEW_PAD]
        mw = jnp.maximum(jnp.max(sw, axis=1, keepdims=True), jnp.max(swn, axis=1, keepdims=True))
        ew = jnp.exp(sw - mw)
        ewn = jnp.exp(swn - mw)
        o_w = (_nt(ew.astype(BF16), vwt) + _dot(ewn.astype(BF16), vwn)) / (
            jnp.sum(ew, axis=1, keepdims=True) + jnp.sum(ewn, axis=1, keepdims=True))

        def gate_col(br):
            return jnp.concatenate([gts[:, (g * NSA_GROUP + h) * 3 + br:(g * NSA_GROUP + h) * 3 + br + 1]
                                    for h in range(NSA_GROUP)], axis=0)

        og.append(gate_col(0) * o_c[rs] + gate_col(1) * o_s + gate_col(2) * o_w)
    for h in range(NSA_GROUP):
        o_ref[0, :, h * LANES:(h + 1) * LANES] = jnp.where(low, og[0][h * 8:(h + 1) * 8],
                                                           og[1][h * 8:(h + 1) * 8]).astype(BF16)


def _nsa_sample_attn(page_table, qn, gates, rows, winc, winn, scb, sbt, wbt, rmat, eloc, cmp_phys, cache):
    b, npages = page_table.shape
    ncs = 2 * npages
    nkeys = npages * PAGE_SIZE
    assert nkeys % KEY_CHUNK == 0
    nch = nkeys // KEY_CHUNK
    full = lambda a: pl.BlockSpec(a.shape, lambda bi, pt: (0,) * a.ndim)
    per_b = lambda a: pl.BlockSpec((1,) + a.shape[1:], lambda bi, pt: (bi,) + (0,) * (a.ndim - 1))
    grid_spec = pltpu.PrefetchScalarGridSpec(
        num_scalar_prefetch=1,
        grid=(b,),
        in_specs=[per_b(qn), per_b(gates), per_b(rows), per_b(winc), per_b(winn),
                  full(scb), full(sbt), full(wbt), full(rmat), full(eloc),
                  pl.BlockSpec(memory_space=pl.ANY), pl.BlockSpec(memory_space=pl.ANY)],
        out_specs=pl.BlockSpec((1, 8, 768), lambda bi, pt: (bi, 0, 0)),
        scratch_shapes=[pltpu.VMEM((npages, 2, 2, 64, PAGE_SIZE), F32), pltpu.VMEM((npages, 2, 256), F32),
                        pltpu.VMEM((2, nch, NSA_GROUP * 8, KEY_CHUNK), F32),
                        pltpu.VMEM((ncs + 8, 1, LANES), F32), pltpu.VMEM((nch, LANES, LANES), BF16),
                        pltpu.SemaphoreType.DMA(()), pltpu.SemaphoreType.DMA(())],
    )
    return pl.pallas_call(
        _nsa_sample_kernel,
        grid_spec=grid_spec,
        out_shape=jax.ShapeDtypeStruct((b, 8, 768), BF16),
        compiler_params=_cparams("arbitrary"),
    )(page_table, qn, gates, rows, winc, winn, scb, sbt, wbt, rmat, eloc, cmp_phys, cache)


def _block_diag2(w):
    z = jnp.zeros_like(w)
    return jnp.concatenate([jnp.concatenate([w, z], axis=-1), jnp.concatenate([z, w], axis=-1)], axis=-2)


def _q_slot_perm():
    return np.concatenate([np.r_[h * 64:(h + 1) * 64, (NSA_GROUP + h) * 64:(NSA_GROUP + h + 1) * 64]
                           for h in range(NSA_GROUP)])


def _sample_codes(ncs, wbuf):
    ql = np.arange(8)[None, :]
    c = np.arange(ncs)[:, None]
    scb = _bucket_np((ncs - c) * CMP_BLOCK + ql - (CMP_BLOCK - 1))
    kap = np.arange(PAGE_SIZE + NEW_PAD)[:, None]
    d = np.where(kap < PAGE_SIZE, PAGE_SIZE + ql - kap, ql - (kap - PAGE_SIZE))
    sbt = np.where((d >= 0) & (kap < PAGE_SIZE + 8), _bucket_np(d), -1)
    j = np.arange(wbuf + NEW_PAD)[:, None]
    d = wbuf + ql - j
    wbt = np.where((d >= 0) & (d <= WINDOW) & (j < wbuf + 8), _bucket_np(d), -1)
    return np.concatenate([scb, sbt, wbt], axis=0).astype(np.int32)


def kernel(x_prompt, x_sample, cache_nsa_kv, cache_nsa_win, cache_mem_kv, state_pool, state_ffn_conv,
           page_table, mem_prompt, rel_bias, g_mix, g_mem, g_ffn, w_in_nsa, q_gain_nsa, k_gain_nsa,
           cmp_pos, cmp_w1, cmp_b1, cmp_w2, cmp_b2, w_in_pool, w_pool_grp, pool_scale, w_mem_kv,
           xq_gain, xk_gain, w_out, w_up, conv_w, conv_b, w_down):
    bp, lp, _ = x_prompt.shape
    bs, ls, _ = x_sample.shape
    assert ls == 8 and lp % 512 == 0 and lp // 64 <= 64
    npages = page_table.shape[1]
    past_len = npages * PAGE_SIZE
    n_phys = cache_nsa_kv.shape[1]
    wbuf = cache_nsa_win.shape[2]
    nm = mem_prompt.shape[1]
    tp_, ts_ = bp * lp, bs * ls
    nc = lp // 64
    ncs = 2 * npages
    scale = HEAD_DIM ** -0.5
    tile2 = lambda v: jnp.tile(v, 2)[None, :]

    a = np.arange(LANES)
    seg = jnp.asarray(a[:, None] // 64 == a[None, :] // 64, BF16)
    perm = _q_slot_perm()

    wb = _bias_table(rel_bias, _window_codes()).reshape(2, NSA_GROUP * 64, WINDOW + 64)
    ct = _bias_table(rel_bias, _cmp_codes()).reshape(2, NSA_GROUP, 8, 64)
    ct = ct.transpose(0, 2, 1, 3).reshape(2, 8, NSA_GROUP * 64)
    amat = np.zeros((2, NSA_GROUP * 64, LANES), np.float32)
    for g in range(2):
        for h in range(NSA_GROUP):
            amat[g, h * 64 + np.arange(64), g * 64 + np.arange(64)] = 1.0
    amat = jnp.asarray(amat, BF16)
    r = np.arange(PAD_ROWS + lp)
    oh_np = np.zeros((PAD_ROWS + lp, LANES), np.float32)
    oh_np[r, np.where(r < PAD_ROWS, 64, (r - PAD_ROWS) // 64)] = 1.0
    oh = jnp.asarray(oh_np, BF16)
    stab = _bias_table(rel_bias, _sample_codes(ncs, wbuf))
    stab = jnp.pad(stab.transpose(1, 0, 2).reshape(-1, NSA_HEADS * 8), ((0, 0), (0, LANES - NSA_HEADS * 8)))
    ns = PAGE_SIZE + NEW_PAD
    scb, sbt, wbt = stab[:ncs], stab[ncs:ncs + ns].T, stab[ncs + ns:].T
    scb = jnp.concatenate([scb[0::2], scb[1::2]], axis=0)
    ppc = KEY_CHUNK // PAGE_SIZE
    eloc_np = np.zeros((LANES, KEY_CHUNK), np.float32)
    for pp in range(ppc):
        eloc_np[pp, pp * PAGE_SIZE:pp * PAGE_SIZE + 64] = 1.0
        eloc_np[ppc + pp, pp * PAGE_SIZE + 64:(pp + 1) * PAGE_SIZE] = 1.0
    eloc = jnp.asarray(eloc_np, BF16)
    lam = np.arange(LANES)
    rm = (lam[:, None] < 96) & (lam[None, :] < 96) & (lam[:, None] // 48 == lam[None, :] // 48) \
        & (lam[:, None] % 8 == lam[None, :] % 8)
    rmat = jnp.asarray(rm, BF16)

    yp = x_prompt.reshape(tp_, D_MODEL)
    ys = x_sample.reshape(ts_, D_MODEL)
    tm = 512
    outs = {k: [] for k in ("kv_p", "kv_s", "win_p", "win_s", "mem_p", "pool_p", "pool_s", "conv_p", "conv_s")}
    depth = g_mix.shape[0]
    for i in range(depth):
        j = i // 2
        wo = w_out[i]
        xg = tile2(xq_gain[i]) * scale
        mkv_p = _memkv(mem_prompt.reshape(bp * nm, D_MODEL), g_mem[i][None], w_mem_kv[i].astype(BF16), seg,
                       tile2(xk_gain[i]), min(tm, bp * nm)).reshape(bp, nm, 2 * XATTN_WIDTH)
        outs["mem_p"].append(mkv_p.reshape(bp, nm, 2, XATTN_HEADS, HEAD_DIM))
        mkv_s = cache_mem_kv[i].reshape(bs, nm, 2 * XATTN_WIDTH)
        if i % 2 == 0:
            w = w_in_nsa[j]
            w = jnp.concatenate([w[:, perm], w[:, 768:1536], w[:, 1572:1828], w[:, 1536:1572],
                                 jnp.zeros((D_MODEL, NSA_IN_COLS - 1828), F32)], axis=1).astype(BF16)
            qg = tile2(q_gain_nsa[j]) * scale
            kg = tile2(k_gain_nsa[j])
            cmp_args = (jnp.tile(cmp_pos[j], (1, 1, 2))[:, :, None, :], _block_diag2(cmp_w1[j]).astype(BF16),
                        jnp.tile(cmp_b1[j], (1, 2))[:, None, :], _block_diag2(cmp_w2[j]).astype(BF16),
                        jnp.tile(cmp_b2[j], (1, 2))[:, None, :], kg, seg)
            wo_mix = wo[:768][perm].astype(BF16)
            qn, rows, win, katt, xqn_p, gates = _nsa_inproj(yp, g_mix[i][None], w, seg, qg, kg, xg, tm)
            cmp_p = _compress(rows.reshape(tp_ // 64, 64, 512), *cmp_args, nbt=512).reshape(bp, nc, 256)
            kvatt = jnp.pad(katt.reshape(bp, lp, 512), ((0, 0), (PAD_ROWS, 0), (0, 0)))
            mix_p = _nsa_prompt_attn(qn.reshape(bp, lp, 768), gates.reshape(bp, lp, 128), cmp_p, kvatt,
                                     oh, wb, ct, amat).reshape(tp_, 768)
            outs["kv_p"].append(rows.reshape(bp, lp, 4, NSA_KV_HEADS, HEAD_DIM))
            outs["win_p"].append(win.reshape(bp, lp, 2, NSA_KV_HEADS, HEAD_DIM)[:, -min(WINDOW, lp):])
            qn, rows, win, _, xqn_s, gates = _nsa_inproj(ys, g_mix[i][None], w, seg, qg, kg, xg, ts_)
            cache_t = cache_nsa_kv[j].transpose(0, 2, 3, 4, 1)
            w2 = cmp_w2[j]
            w2sel = jnp.zeros((2, 2, 2, 2 * CMP_HIDDEN, LANES), F32)
            for blk in range(2):
                for g in range(2):
                    w2sel = w2sel.at[:, blk, g, blk * CMP_HIDDEN:(blk + 1) * CMP_HIDDEN,
                                     g * HEAD_DIM:(g + 1) * HEAD_DIM].set(w2)
            cmp_phys = _compress_cache(
                cache_t, jnp.tile(cmp_pos[j].transpose(0, 2, 1), (1, 1, 2))[:, :, None, :],
                _block_diag2(cmp_w1[j].transpose(0, 2, 1, 3)).astype(BF16), cmp_args[2], w2sel.astype(BF16),
                cmp_args[4], kg, seg, 512)
            winc = cache_nsa_win[j].transpose(0, 2, 3, 4, 1)
            mix_s = _nsa_sample_attn(page_table, qn.reshape(bs, ls, 768), gates.reshape(bs, ls, 128),
                                     rows.reshape(bs, ls, 512), winc, win.reshape(bs, ls, 256),
                                     scb, sbt, wbt, rmat, eloc, cmp_phys.reshape(n_phys, 2, 256),
                                     cache_t).reshape(ts_, 768)
            outs["kv_s"].append(rows.reshape(bs, ls, 4, NSA_KV_HEADS, HEAD_DIM))
            wall = jnp.concatenate([cache_nsa_win[j], win.reshape(bs, ls, 2, NSA_KV_HEADS, HEAD_DIM)], axis=1)
            outs["win_s"].append(wall[:, -wbuf:])
        else:
            w = w_in_pool[j].astype(BF16)
            wbd = jnp.zeros((POOL_WIDTH, POOL_WIDTH), F32)
            for gi in range(len(POOL_WINDOWS)):
                sl = slice(gi * POOL_GROUP_DIM, (gi + 1) * POOL_GROUP_DIM)
                wbd = wbd.at[sl, sl].set(w_pool_grp[j, gi])
            wbd = wbd.astype(BF16)
            psc = pool_scale[j][None]
            wo_mix = wo[:768].astype(BF16)
            u_p, xqn_p = _pool_inproj(yp, g_mix[i][None], w, seg, xg, tm)
            u_p3 = u_p.reshape(bp, lp, POOL_WIDTH)
            mix_p = _pool_mix(u_p3, jnp.zeros((bp, 16, POOL_WIDTH), F32), wbd, psc, tm, 0).reshape(tp_, 768)
            outs["pool_p"].append(u_p3[:, -POOL_STATE:])
            u_s, xqn_s = _pool_inproj(ys, g_mix[i][None], w, seg, xg, ts_)
            u_s3 = u_s.reshape(bs, ls, POOL_WIDTH)
            prev16 = jnp.concatenate([jnp.zeros((bs, 1, POOL_WIDTH), F32), state_pool[j]], axis=1)
            mix_s = _pool_mix(u_s3, prev16, wbd, psc, ls, past_len).reshape(ts_, 768)
            outs["pool_s"].append(jnp.concatenate([state_pool[j], u_s3], axis=1)[:, -POOL_STATE:])
        wo_att = wo[768:].astype(BF16)
        att_p = _mem_attn(xqn_p.reshape(bp, lp, 256), mkv_p, tm).reshape(tp_, 256)
        att_s = _mem_attn(xqn_s.reshape(bs, ls, 256), mkv_s, ls).reshape(ts_, 256)
        yp = _outproj(yp, mix_p, att_p, wo_mix, wo_att, tm)
        ys = _outproj(ys, mix_s, att_s, wo_mix, wo_att, ts_)
        wu, wd = w_up[i].astype(BF16), w_down[i].astype(BF16)
        yp, c_p = _ffn_prompt(yp, g_ffn[i][None], wu, conv_w[i], conv_b[i][None], wd, tm, lp)
        ys, c_s = _ffn_sample(ys, g_ffn[i][None], wu, conv_w[i], conv_b[i][None], wd, state_ffn_conv[i])
        outs["conv_p"].append(c_p)
        outs["conv_s"].append(c_s)
    st = lambda k: jnp.stack(outs[k])
    return (yp.reshape(bp, lp, D_MODEL), ys.reshape(bs, ls, D_MODEL), st("kv_p"), st("kv_s"), st("win_p"),
            st("win_s"), st("mem_p"), st("pool_p"), st("pool_s"), st("conv_p"), st("conv_s"))
```

```python
import functools
import math

import numpy as np
import jax
import jax.numpy as jnp
from jax import lax
from jax.experimental import pallas as pl
from jax.experimental.pallas import tpu as pltpu

F32 = jnp.float32
BF16 = jnp.bfloat16

D_MODEL = 1024
PAGE_SIZE = 128
HEAD_DIM = 64
NSA_HEADS = 12
NSA_KV_HEADS = 2
NSA_GROUP = NSA_HEADS // NSA_KV_HEADS
NSA_WIDTH = NSA_HEADS * HEAD_DIM
NSA_KV_WIDTH = NSA_KV_HEADS * HEAD_DIM
CMP_BLOCK = 64
CMP_HIDDEN = 128
SLC_TOPN = 16
WINDOW = 512
FORCED_SCORE = 1e4
XATTN_HEADS = 4
XATTN_WIDTH = XATTN_HEADS * HEAD_DIM
POOL_WINDOWS = (2, 4, 8, 16)
POOL_GROUP_DIM = 192
POOL_WIDTH = len(POOL_WINDOWS) * POOL_GROUP_DIM
POOL_STATE = max(POOL_WINDOWS) - 1
D_FF = 2816
REL_BUCKETS = 32
REL_MAX_DIST = 128
EPS = 1e-6
BIG = 1e30

LANES = 128
PAD_ROWS = WINDOW
NSA_IN_COLS = 1920
FF_CHUNK = D_FF // 2
VMEM_LIMIT = 56 * 1024 * 1024


def _cparams(*sem):
    return pltpu.CompilerParams(dimension_semantics=sem, vmem_limit_bytes=VMEM_LIMIT)


def _nt(a, b):
    return lax.dot_general(a, b, (((1,), (1,)), ((), ())), preferred_element_type=F32)


def _tn(a, b):
    return lax.dot_general(a, b, (((0,), (0,)), ((), ())), preferred_element_type=F32)


def _dot(a, b):
    return jnp.dot(a, b, preferred_element_type=F32)


def _rms_rows(x, g):
    return x * lax.rsqrt(jnp.mean(x * x, axis=-1, keepdims=True) + EPS) * g


def _seg_rms(zc, seg, gain):
    zz = zc * zc
    hi = zz.astype(BF16)
    lo = (zz - hi.astype(F32)).astype(BF16)
    ms = (_dot(hi, seg) + _dot(lo, seg)) * (1.0 / HEAD_DIM)
    return zc * lax.rsqrt(ms + EPS) * gain


def _nsa_inproj_kernel(x_ref, g_ref, w_ref, seg_ref, qg_ref, kg_ref, xg_ref,
                       qn_ref, rows_ref, win_ref, katt_ref, xqn_ref, gates_ref):
    xn = _rms_rows(x_ref[...], g_ref[...])
    z = _dot(xn.astype(BF16), w_ref[...])
    seg = seg_ref[...]
    for h in range(6):
        qn_ref[:, h * LANES:(h + 1) * LANES] = _seg_rms(z[:, h * LANES:(h + 1) * LANES], seg, qg_ref[...]).astype(BF16)
    kv = 768
    rows_ref[:, 0:256] = z[:, kv:kv + 256]
    ks = _seg_rms(z[:, kv + 256:kv + 384], seg, kg_ref[...])
    vs = z[:, kv + 384:kv + 512]
    kw = _seg_rms(z[:, kv + 512:kv + 640], seg, kg_ref[...])
    vw = z[:, kv + 640:kv + 768]
    rows_ref[:, 256:384] = ks
    rows_ref[:, 384:512] = vs
    win_ref[:, 0:128] = kw
    win_ref[:, 128:256] = vw
    katt_ref[:, 0:128] = ks.astype(BF16)
    katt_ref[:, 128:256] = vs.astype(BF16)
    katt_ref[:, 256:384] = kw.astype(BF16)
    katt_ref[:, 384:512] = vw.astype(BF16)
    for h in range(2):
        c0 = 1536 + h * LANES
        xqn_ref[:, h * LANES:(h + 1) * LANES] = _seg_rms(z[:, c0:c0 + LANES], seg, xg_ref[...]).astype(BF16)
    gates_ref[...] = jax.nn.sigmoid(z[:, 1792:1920])


def _nsa_inproj(x, g, w, seg, qg, kg, xg, tm):
    t = x.shape[0]
    full = lambda a: pl.BlockSpec(a.shape, lambda i: (0,) * a.ndim)
    row = lambda n: pl.BlockSpec((tm, n), lambda i: (i, 0))
    return pl.pallas_call(
        _nsa_inproj_kernel,
        grid=(t // tm,),
        in_specs=[row(D_MODEL), full(g), full(w), full(seg), full(qg), full(kg), full(xg)],
        out_specs=[row(768), row(512), row(256), row(512), row(256), row(128)],
        out_shape=[jax.ShapeDtypeStruct((t, 768), BF16), jax.ShapeDtypeStruct((t, 512), F32),
                   jax.ShapeDtypeStruct((t, 256), F32), jax.ShapeDtypeStruct((t, 512), BF16),
                   jax.ShapeDtypeStruct((t, 256), BF16), jax.ShapeDtypeStruct((t, 128), F32)],
        compiler_params=_cparams("parallel"),
    )(x, g, w, seg, qg, kg, xg)


def _pool_inproj_kernel(x_ref, g_ref, w_ref, seg_ref, xg_ref, u_ref, xqn_ref):
    xn = _rms_rows(x_ref[...], g_ref[...])
    z = _dot(xn.astype(BF16), w_ref[...])
    u_ref[...] = z[:, 0:POOL_WIDTH]
    seg = seg_ref[...]
    for h in range(2):
        c0 = POOL_WIDTH + h * LANES
        xqn_ref[:, h * LANES:(h + 1) * LANES] = _seg_rms(z[:, c0:c0 + LANES], seg, xg_ref[...]).astype(BF16)


def _pool_inproj(x, g, w, seg, xg, tm):
    t = x.shape[0]
    full = lambda a: pl.BlockSpec(a.shape, lambda i: (0,) * a.ndim)
    row = lambda n: pl.BlockSpec((tm, n), lambda i: (i, 0))
    return pl.pallas_call(
        _pool_inproj_kernel,
        grid=(t // tm,),
        in_specs=[row(D_MODEL), full(g), full(w), full(seg), full(xg)],
        out_specs=[row(POOL_WIDTH), row(256)],
        out_shape=[jax.ShapeDtypeStruct((t, POOL_WIDTH), F32), jax.ShapeDtypeStruct((t, 256), BF16)],
        compiler_params=_cparams("parallel"),
    )(x, g, w, seg, xg)


def _memkv_kernel(x_ref, g_ref, w_ref, seg_ref, kg_ref, o_ref):
    xn = _rms_rows(x_ref[...], g_ref[...])
    z = _dot(xn.astype(BF16), w_ref[...])
    seg = seg_ref[...]
    for h in range(2):
        o_ref[:, h * LANES:(h + 1) * LANES] = _seg_rms(z[:, h * LANES:(h + 1) * LANES], seg, kg_ref[...])
    o_ref[:, 256:512] = z[:, 256:512]


def _memkv(x, g, w, seg, kg, tm):
    t = x.shape[0]
    full = lambda a: pl.BlockSpec(a.shape, lambda i: (0,) * a.ndim)
    row = lambda n: pl.BlockSpec((tm, n), lambda i: (i, 0))
    return pl.pallas_call(
        _memkv_kernel,
        grid=(t // tm,),
        in_specs=[row(D_MODEL), full(g), full(w), full(seg), full(kg)],
        out_specs=row(512),
        out_shape=jax.ShapeDtypeStruct((t, 512), F32),
        compiler_params=_cparams("parallel"),
    )(x, g, w, seg, kg)


def _compress_kernel(x_ref, pos_ref, w1_ref, b1_ref, w2_ref, b2_ref, kg_ref, seg_ref, o_ref, acc_ref):
    lc = pl.program_id(1)

    @pl.when(lc == 0)
    def _():
        acc_ref[...] = jnp.zeros_like(acc_ref)

    for comp in range(2):
        part = None
        for l in range(8):
            xl = x_ref[:, l, comp * LANES:(comp + 1) * LANES] + pos_ref[comp, l]
            d = _dot(xl.astype(BF16), w1_ref[comp, l])
            part = d if part is None else part + d
        acc_ref[:, comp * 256:(comp + 1) * 256] += part

    @pl.when(lc == pl.num_programs(1) - 1)
    def _():
        for comp in range(2):
            h = jax.nn.gelu(acc_ref[:, comp * 256:(comp + 1) * 256] + b1_ref[comp])
            o = _dot(h.astype(BF16), w2_ref[comp]) + b2_ref[comp]
            if comp == 0:
                o = _seg_rms(o, seg_ref[...], kg_ref[...])
            o_ref[:, comp * LANES:(comp + 1) * LANES] = o


def _compress(x3, pos_t, w1bd, b1t, w2bd, b2t, kg, seg, nbt):
    nb, _, w = x3.shape
    nbt = min(nbt, nb)
    assert nb % nbt == 0
    full = lambda a: pl.BlockSpec(a.shape, lambda j, l: (0,) * a.ndim)
    return pl.pallas_call(
        _compress_kernel,
        grid=(nb // nbt, 8),
        in_specs=[pl.BlockSpec((nbt, 8, 256), lambda j, l: (j, l, 0)),
                  pl.BlockSpec((2, 8, 1, LANES), lambda j, l: (0, l, 0, 0)),
                  pl.BlockSpec((2, 8, LANES, 256), lambda j, l: (0, l, 0, 0)),
                  full(b1t), full(w2bd), full(b2t), full(kg), full(seg)],
        out_specs=pl.BlockSpec((nbt, 256), lambda j, l: (j, 0)),
        out_shape=jax.ShapeDtypeStruct((nb, 256), F32),
        scratch_shapes=[pltpu.VMEM((nbt, 512), F32)],
        compiler_params=_cparams("parallel", "arbitrary"),
    )(x3, pos_t, w1bd, b1t, w2bd, b2t, kg, seg)


def _compress_cache_kernel(x_ref, pos_ref, w1_ref, b1_ref, w2_ref, b2_ref, kg_ref, seg_ref, o_ref, acc_ref):
    dc = pl.program_id(1)

    @pl.when(dc == 0)
    def _():
        acc_ref[...] = jnp.zeros_like(acc_ref)

    pt = x_ref.shape[0]
    x2 = x_ref.reshape(pt * 32, LANES)

    def feature_rows(comp, g, dd):
        return (x2[pl.ds((comp * 2 + g) * 8 + dd, pt, stride=32), :] + pos_ref[comp, dd]).astype(BF16)

    for comp in range(2):
        for g in range(2):
            part = None
            for dp in range(4):
                xd = jnp.concatenate([feature_rows(comp, g, 2 * dp), feature_rows(comp, g, 2 * dp + 1)], axis=1)
                d = _dot(xd, w1_ref[comp, dp])
                part = d if part is None else part + d
            c0 = (comp * 2 + g) * 256
            acc_ref[:, c0:c0 + 256] += part

    @pl.when(dc == pl.num_programs(1) - 1)
    def _():
        for comp in range(2):
            hid = [jax.nn.gelu(acc_ref[:, (comp * 2 + g) * 256:(comp * 2 + g + 1) * 256] + b1_ref[comp]).astype(BF16)
                   for g in range(2)]
            for blk in range(2):
                o = _dot(hid[0], w2_ref[comp, blk, 0]) + _dot(hid[1], w2_ref[comp, blk, 1]) + b2_ref[comp]
                if comp == 0:
                    o = _seg_rms(o, seg_ref[...], kg_ref[...])
                o_ref[:, blk * 256 + comp * LANES:blk * 256 + (comp + 1) * LANES] = o


def _compress_cache(xt, pos_t, w1t, b1t, w2sel, b2t, kg, seg, pt):
    n_phys = xt.shape[0]
    pt = min(pt, n_phys)
    assert n_phys % pt == 0
    full = lambda a: pl.BlockSpec(a.shape, lambda j, d: (0,) * a.ndim)
    return pl.pallas_call(
        _compress_cache_kernel,
        grid=(n_phys // pt, 8),
        in_specs=[pl.BlockSpec((pt, 2, 2, 8, LANES), lambda j, d: (j, 0, 0, d, 0)),
                  pl.BlockSpec((2, 8, 1, LANES), lambda j, d: (0, d, 0, 0)),
                  pl.BlockSpec((2, 4, 2 * LANES, 256), lambda j, d: (0, d, 0, 0)),
                  full(b1t), full(w2sel), full(b2t), full(kg), full(seg)],
        out_specs=pl.BlockSpec((pt, 512), lambda j, d: (j, 0)),
        out_shape=jax.ShapeDtypeStruct((n_phys, 512), F32),
        scratch_shapes=[pltpu.VMEM((pt, 1024), F32)],
        compiler_params=_cparams("parallel", "arbitrary"),
    )(xt, pos_t, w1t, b1t, w2sel, b2t, kg, seg)


def _bucket_np(d):
    n = np.maximum(d, 0)
    max_exact = REL_BUCKETS // 2
    nf = np.maximum(n, 1).astype(np.float32)
    large = max_exact + (np.log(nf / max_exact) / math.log(REL_MAX_DIST / max_exact)
                         * (REL_BUCKETS - max_exact)).astype(np.int32)
    large = np.minimum(large, REL_BUCKETS - 1)
    return np.where(n < max_exact, n, large).astype(np.int32)


def _bias_table_kernel(rb_ref, code_ref, o_ref):
    code = code_ref[...]
    for h in range(NSA_HEADS):
        far = rb_ref[REL_BUCKETS - 1, h]
        acc = jnp.full(code.shape, -BIG, F32)
        for k in range(REL_BUCKETS):
            acc = jnp.where(code == k, rb_ref[k, h] - far, acc)
        o_ref[h] = acc


def _bias_table(rel_bias, code):
    code = jnp.asarray(code, jnp.int32)
    return pl.pallas_call(
        _bias_table_kernel,
        in_specs=[pl.BlockSpec(memory_space=pltpu.SMEM), pl.BlockSpec(code.shape, lambda: (0, 0))],
        out_specs=pl.BlockSpec((NSA_HEADS,) + code.shape, lambda: (0, 0, 0)),
        out_shape=jax.ShapeDtypeStruct((NSA_HEADS,) + code.shape, F32),
    )(rel_bias, code)


def _window_codes():
    ql = np.arange(64)[:, None]
    j = np.arange(WINDOW + 64)[None, :]
    d = WINDOW + ql - j
    return np.where((d >= 0) & (d <= WINDOW), _bucket_np(d), -1).astype(np.int32)


def _cmp_codes():
    delta = np.arange(8)[:, None]
    ql = np.arange(64)[None, :]
    d = delta * CMP_BLOCK + ql - (CMP_BLOCK - 1)
    return np.where(d >= 0, _bucket_np(d), -1).astype(np.int32)


FAR_TILE = 512


def _nsa_prompt_kernel(qn_ref, gates_ref, cmp_ref, kv_ref, oh_ref, wb_ref, ct_ref, amat_ref, o_ref, s_ref):
    i = pl.program_id(1)
    nc = cmp_ref.shape[1]
    rows = NSA_GROUP * 64
    q = qn_ref[0].astype(F32)
    gts = gates_ref[0]
    kc = cmp_ref[0, :, 0:128].astype(BF16)
    vc = cmp_ref[0, :, 128:256].astype(BF16)
    lane = lax.broadcasted_iota(jnp.int32, (64, LANES), 1)
    lane_r = lax.broadcasted_iota(jnp.int32, (rows, LANES), 1)
    half = [lane < 64, lane >= 64]
    qg = [jnp.concatenate([jnp.where(half[g], q[:, h * LANES:(h + 1) * LANES], 0.0) for h in range(NSA_GROUP)],
                          axis=0).astype(BF16) for g in range(2)]

    o_c, imp = [], []
    blk_r = lax.broadcasted_iota(jnp.int32, (nc, rows), 0)
    for g in range(2):
        ctg = ct_ref[g]
        bias = jnp.where(blk_r == i, ctg[0:1], jnp.where(blk_r == i - 1, ctg[1:2], jnp.where(
            blk_r == i - 2, ctg[2:3], jnp.where(blk_r > i, -BIG, 0.0))))
        lc = _nt(kc, qg[g]) + bias
        e = jnp.exp(lc - jnp.max(lc, axis=0, keepdims=True))
        p = jnp.where(bias > -0.5 * BIG, e / jnp.sum(e, axis=0, keepdims=True), 0.0)
        o_c.append(_tn(p.astype(BF16), vc))
        s3 = p[:, 0:128] + p[:, 128:256] + p[:, 256:384]
        imp.append(s3 + pltpu.roll(s3, 64, axis=1))
    lane_c = lax.broadcasted_iota(jnp.int32, (nc, LANES), 1)
    blk = lax.broadcasted_iota(jnp.int32, (nc, LANES), 0)
    impp = jnp.where(lane_c < 64, imp[0], imp[1])
    forced = (blk == 0) | (blk == i) | (blk == i - 1)
    score = jnp.where(blk <= i, jnp.where(forced, FORCED_SCORE, impp), -1.0)
    sc8 = [score[r * 8:(r + 1) * 8] for r in range(nc // 8)]
    blk8 = lax.broadcasted_iota(jnp.int32, (8, LANES), 0)
    rank8 = [jnp.zeros((8, LANES), F32) for _ in sc8]
    for cp in range(nc):
        row = score[cp:cp + 1, :]
        for r in range(nc // 8):
            if r * 8 > cp:
                beats = row >= sc8[r]
            elif r * 8 + 7 < cp:
                beats = row > sc8[r]
            else:
                beats = (row > sc8[r]) | ((row == sc8[r]) & (blk8 + r * 8 > cp))
            rank8[r] = rank8[r] + jnp.where(beats, 1.0, 0.0)
    rank = jnp.concatenate(rank8, axis=0)
    notsel =jnp.where((rank < float(min(SLC_TOPN, nc))) & (score >= 0.0), 0.0, 1.0)
    notsel_far = jnp.where(blk > i - 3, 1.0, notsel)
    tail = jnp.where(lax.broadcasted_iota(jnp.int32, (LANES - nc, LANES), 0) == 64 - nc, 1.0, 0.0)
    ns_near = jnp.concatenate([notsel, tail], axis=0).astype(BF16)
    ns_far = jnp.concatenate([notsel_far, tail], axis=0).astype(BF16)
    m_win = jnp.where(lane_r == 64, -BIG, 0.0).astype(BF16)

    lhs_far, lhs_near, lhs_win = [], [], []
    for g in range(2):
        a = amat_ref[g]
        lhs_near.append(jnp.concatenate([qg[g], (_nt(a, ns_near) * -BIG).astype(BF16)], axis=1))
        lhs_far.append(jnp.concatenate([qg[g], (_nt(a, ns_far) * -BIG).astype(BF16)], axis=1))
        lhs_win.append(jnp.concatenate([qg[g], m_win], axis=1))

    def keys_aug(r0, n, c0):
        return jnp.concatenate([kv_ref[0, pl.ds(r0, n), c0:c0 + LANES], oh_ref[pl.ds(r0, n), :]], axis=1)

    def fold(x, op):
        return functools.reduce(op, [x[:, c * LANES:(c + 1) * LANES] for c in range(x.shape[1] // LANES)])

    ntiles = (i + 5) // (FAR_TILE // 64)
    r0n = pl.multiple_of(PAD_ROWS + (i - 2) * 64, 64)
    kn = keys_aug(r0n, 192, 0)
    s_near = [_nt(lhs_near[g], kn) + wb_ref[g, :, 384:576] for g in range(2)]

    def pass_a(t, mrun):
        k = keys_aug(pl.multiple_of(PAD_ROWS + t * FAR_TILE, FAR_TILE), FAR_TILE, 0)
        out = []
        for g in range(2):
            s = _nt(lhs_far[g], k)
            s_ref[g, t] = s
            out.append(jnp.maximum(mrun[g], fold(s, jnp.maximum)))
        return tuple(out)

    mrun = lax.fori_loop(0, ntiles, pass_a, (jnp.full((rows, LANES), -BIG, F32),) * 2)
    m = [jnp.maximum(jnp.max(mrun[g], axis=1, keepdims=True), jnp.max(s_near[g], axis=1, keepdims=True))
         for g in range(2)]

    def pass_b(t, carry):
        v = kv_ref[0, pl.ds(pl.multiple_of(PAD_ROWS + t * FAR_TILE, FAR_TILE), FAR_TILE), 128:256]
        out = []
        for g in range(2):
            e = jnp.exp(s_ref[g, t] - m[g])
            out.append((carry[g][0] + fold(e, jnp.add), carry[g][1] + _dot(e.astype(BF16), v)))
        return tuple(out)

    zero = jnp.zeros((rows, LANES), F32)
    far = lax.fori_loop(0, ntiles, pass_b, ((zero, zero),) * 2)
    vn = kv_ref[0, pl.ds(r0n, 192), 128:256]
    r0w = pl.multiple_of(PAD_ROWS + (i - 8) * 64, 64)
    kw = keys_aug(r0w, WINDOW + 64, 256)
    vw = kv_ref[0, pl.ds(r0w, WINDOW + 64), 384:512]

    outs = []
    for g in range(2):
        e = jnp.exp(s_near[g] - m[g])
        l = jnp.sum(far[g][0], axis=1, keepdims=True) + jnp.sum(e, axis=1, keepdims=True)
        o_s = (far[g][1] + _dot(e.astype(BF16), vn)) / l

        s = _nt(lhs_win[g], kw) + wb_ref[g]
        e = jnp.exp(s - jnp.max(s, axis=1, keepdims=True))
        o_w = _dot(e.astype(BF16), vw) / jnp.sum(e, axis=1, keepdims=True)

        per_head = []
        for h in range(NSA_GROUP):
            c0 = (g * NSA_GROUP + h) * 3
            r = slice(h * 64, (h + 1) * 64)
            per_head.append(gts[:, c0:c0 + 1] * o_c[g][r] + gts[:, c0 + 1:c0 + 2] * o_s[r]
                            + gts[:, c0 + 2:c0 + 3] * o_w[r])
        outs.append(per_head)
    low = lax.broadcasted_iota(jnp.int32, (64, LANES), 1) < 64
    for h in range(NSA_GROUP):
        o_ref[0, :, h * LANES:(h + 1) * LANES] = jnp.where(low, outs[0][h], outs[1][h]).astype(BF16)


def _nsa_prompt_attn(qn, gates, cmp, kvatt, oh, wb, ct, amat):
    b, l, _ = qn.shape
    nc = l // 64
    full = lambda a: pl.BlockSpec(a.shape, lambda bi, i: (0,) * a.ndim)
    return pl.pallas_call(
        _nsa_prompt_kernel,
        grid=(b, nc),
        in_specs=[pl.BlockSpec((1, 64, 768), lambda bi, i: (bi, i, 0)),
                  pl.BlockSpec((1, 64, 128), lambda bi, i: (bi, i, 0)),
                  pl.BlockSpec((1, nc, 256), lambda bi, i: (bi, 0, 0)),
                  pl.BlockSpec((1, PAD_ROWS + l, 512), lambda bi, i: (bi, 0, 0)),
                  full(oh), full(wb), full(ct), full(amat)],
        out_specs=pl.BlockSpec((1, 64, 768), lambda bi, i: (bi, i, 0)),
        out_shape=jax.ShapeDtypeStruct((b, l, 768), BF16),
        scratch_shapes=[pltpu.VMEM((2, (nc + 4) // (FAR_TILE // 64), NSA_GROUP * 64, FAR_TILE), F32)],
        compiler_params=_cparams("parallel", "arbitrary"),
    )(qn, gates, cmp, kvatt, oh, wb, ct, amat)


def _mem_attn_kernel(q_ref, kv_ref, o_ref):
    q = q_ref[0].astype(F32)
    k = kv_ref[0, :, 0:256].astype(BF16)
    v = kv_ref[0, :, 256:512].astype(BF16)
    head = lax.broadcasted_iota(jnp.int32, q.shape, 1) // HEAD_DIM
    out = jnp.zeros(q.shape, F32)
    for h in range(XATTN_HEADS):
        qh = jnp.where(head == h, q, 0.0).astype(BF16)
        s = _nt(qh, k)
        e = jnp.exp(s - jnp.max(s, axis=1, keepdims=True))
        p = e / jnp.sum(e, axis=1, keepdims=True)
        out = out + jnp.where(head == h, _dot(p.astype(BF16), v), 0.0)
    o_ref[0] = out.astype(BF16)


def _mem_attn(xqn, mkv, tq):
    b, l, _ = xqn.shape
    nm = mkv.shape[1]
    return pl.pallas_call(
        _mem_attn_kernel,
        grid=(b, l // tq),
        in_specs=[pl.BlockSpec((1, tq, 256), lambda bi, i: (bi, i, 0)),
                  pl.BlockSpec((1, nm, 512), lambda bi, i: (bi, 0, 0))],
        out_specs=pl.BlockSpec((1, tq, 256), lambda bi, i: (bi, i, 0)),
        out_shape=jax.ShapeDtypeStruct((b, l, 256), BF16),
        compiler_params=_cparams("parallel", "parallel"),
    )(xqn, mkv)


def _pool_kernel(u_ref, prev_ref, w_ref, scale_ref, o_ref, carry_ref, *, pos0):
    i = pl.program_id(1)
    tp = u_ref.shape[1]

    @pl.when(i == 0)
    def _():
        carry_ref[...] = prev_ref[0]

    u = u_ref[0]
    ext = jnp.concatenate([carry_ref[...], u], axis=0)
    carry_ref[...] = ext[tp:tp + 16]
    s2 = ext + pltpu.roll(ext, 1, axis=0)
    s4 = s2 + pltpu.roll(s2, 2, axis=0)
    s8 = s4 + pltpu.roll(s4, 4, axis=0)
    s16 = s8 + pltpu.roll(s8, 8, axis=0)
    lane = lax.broadcasted_iota(jnp.int32, (tp, POOL_WIDTH), 1)
    pos1 = (pos0 + 1 + i * tp + lax.broadcasted_iota(jnp.int32, (tp, POOL_WIDTH), 0)).astype(F32)
    grp = lane // POOL_GROUP_DIM
    ssum = jnp.where(grp == 0, s2[16:], jnp.where(grp == 1, s4[16:], jnp.where(grp == 2, s8[16:], s16[16:])))
    win = jnp.where(grp == 0, 2.0, jnp.where(grp == 1, 4.0, jnp.where(grp == 2, 8.0, 16.0)))
    pooled = ssum / jnp.minimum(win, pos1) - u
    o_ref[0] = (_dot(pooled.astype(BF16), w_ref[...]) * scale_ref[...]).astype(BF16)


def _pool_mix(u, prev16, wbd, scale, tp, pos0):
    b, l, _ = u.shape
    full = lambda a: pl.BlockSpec(a.shape, lambda bi, i: (0,) * a.ndim)
    return pl.pallas_call(
        functools.partial(_pool_kernel, pos0=pos0),
        grid=(b, l // tp),
        in_specs=[pl.BlockSpec((1, tp, POOL_WIDTH), lambda bi, i: (bi, i, 0)),
                  pl.BlockSpec((1, 16, POOL_WIDTH), lambda bi, i: (bi, 0, 0)),
                  full(wbd), full(scale)],
        out_specs=pl.BlockSpec((1, tp, POOL_WIDTH), lambda bi, i: (bi, i, 0)),
        out_shape=jax.ShapeDtypeStruct((b, l, POOL_WIDTH), BF16),
        scratch_shapes=[pltpu.VMEM((16, POOL_WIDTH), F32)],
        compiler_params=_cparams("parallel", "arbitrary"),
    )(u, prev16, wbd, scale)


def _outproj_kernel(y_ref, mix_ref, att_ref, wm_ref, wa_ref, o_ref):
    o_ref[...] = y_ref[...] + _dot(mix_ref[...], wm_ref[...]) + _dot(att_ref[...], wa_ref[...])


def _outproj(y, mix, att, wm, wa, tm):
    t = y.shape[0]
    full = lambda a: pl.BlockSpec(a.shape, lambda i: (0,) * a.ndim)
    row = lambda n: pl.BlockSpec((tm, n), lambda i: (i, 0))
    return pl.pallas_call(
        _outproj_kernel,
        grid=(t // tm,),
        in_specs=[row(D_MODEL), row(768), row(256), full(wm), full(wa)],
        out_specs=row(D_MODEL),
        out_shape=jax.ShapeDtypeStruct((t, D_MODEL), F32),
        compiler_params=_cparams("parallel"),
    )(y, mix, att, wm, wa)


def _ffn_half(h, w_ref, cw_ref, cb_ref, ext_ref, prev8, fix):
    tm = h.shape[0]
    u = _dot(h, w_ref[...])
    ext_ref[0:8, :] = prev8
    ext_ref[8:8 + tm, :] = u
    s1 = ext_ref[7:7 + tm, :]
    s2 = ext_ref[6:6 + tm, :]
    if fix is not None:
        rowm, p1, p2 = fix
        s1 = jnp.where(rowm >= 1, s1, p1)
        s2 = jnp.where(rowm >= 2, s2, p2)
    cw = cw_ref[...]
    return u, cb_ref[...] + cw[0:1] * s2 + cw[1:2] * s1 + cw[2:3] * u


def _ffn_prompt_kernel(y_ref, g_ref, wa_ref, wv_ref, cwa_ref, cwv_ref, cba_ref, cbv_ref, wd_ref,
                       o_ref, sa_ref, sv_ref, h_ref, acc_ref, ext_ref, carry_ref, *, tiles_per_seq):
    j = pl.program_id(0)
    c = pl.program_id(1)
    tm = y_ref.shape[0]

    @pl.when(c == 0)
    def _():
        h_ref[...] = _rms_rows(y_ref[...], g_ref[...]).astype(BF16)

    @pl.when((j % tiles_per_seq) == 0)
    def _():
        carry_ref[c] = jnp.zeros(carry_ref.shape[1:], F32)

    h = h_ref[...]
    ua, ya = _ffn_half(h, wa_ref, cwa_ref, cba_ref, ext_ref, carry_ref[c, 0], None)
    carry_ref[c, 0] = ua[tm - 8:tm]
    sa_ref[0] = ua[tm - 8:tm]
    uv, yv = _ffn_half(h, wv_ref, cwv_ref, cbv_ref, ext_ref, carry_ref[c, 1], None)
    carry_ref[c, 1] = uv[tm - 8:tm]
    sv_ref[0] = uv[tm - 8:tm]
    d = _dot((jax.nn.silu(ya) * yv).astype(BF16), wd_ref[...])

    @pl.when(c == 0)
    def _():
        acc_ref[...] = d

    @pl.when(c == pl.num_programs(1) - 1)
    def _():
        o_ref[...] = y_ref[...] + acc_ref[...] + d


def _ffn_prompt(y, g, w_up, conv_w, conv_b, w_down, tm, seq_len):
    t = y.shape[0]
    nseq = t // seq_len
    tps = seq_len // tm
    fc = FF_CHUNK
    ncf = D_FF // fc
    full = lambda a: pl.BlockSpec(a.shape, lambda j, c: (0,) * a.ndim)
    outs = pl.pallas_call(
        functools.partial(_ffn_prompt_kernel, tiles_per_seq=tps),
        grid=(t // tm, ncf),
        in_specs=[pl.BlockSpec((tm, D_MODEL), lambda j, c: (j, 0)), full(g),
                  pl.BlockSpec((D_MODEL, fc), lambda j, c: (0, c)),
                  pl.BlockSpec((D_MODEL, fc), lambda j, c: (0, ncf + c)),
                  pl.BlockSpec((3, fc), lambda j, c: (0, c)),
                  pl.BlockSpec((3, fc), lambda j, c: (0, ncf + c)),
                  pl.BlockSpec((1, fc), lambda j, c: (0, c)),
                  pl.BlockSpec((1, fc), lambda j, c: (0, ncf + c)),
                  pl.BlockSpec((fc, D_MODEL), lambda j, c: (c, 0))],
        out_specs=[pl.BlockSpec((tm, D_MODEL), lambda j, c: (j, 0)),
                   pl.BlockSpec((1, 8, fc), lambda j, c: (j, 0, c)),
                   pl.BlockSpec((1, 8, fc), lambda j, c: (j, 0, c))],
        out_shape=[jax.ShapeDtypeStruct((t, D_MODEL), F32),
                   jax.ShapeDtypeStruct((t // tm, 8, D_FF), F32),
                   jax.ShapeDtypeStruct((t // tm, 8, D_FF), F32)],
        scratch_shapes=[pltpu.VMEM((tm, D_MODEL), BF16), pltpu.VMEM((tm, D_MODEL), F32),
                        pltpu.VMEM((tm + 8, fc), F32), pltpu.VMEM((ncf, 2, 8, fc), F32)],
        compiler_params=_cparams("arbitrary", "arbitrary"),
    )(y, g, w_up, w_up, conv_w, conv_w, conv_b, conv_b, w_down)
    y_out, sa, sv = outs
    last = lambda s: s.reshape(nseq, tps, 8, D_FF)[:, tps - 1, 6:8]
    state = jnp.concatenate([last(sa), last(sv)], axis=-1)
    return y_out, state


def _ffn_sample_kernel(y_ref, g_ref, wa_ref, wv_ref, cwa_ref, cwv_ref, cba_ref, cbv_ref, wd_ref,
                       p1a_ref, p1v_ref, p2a_ref, p2v_ref, o_ref, ua_ref, uv_ref, acc_ref, ext_ref):
    c = pl.program_id(0)
    h = _rms_rows(y_ref[...], g_ref[...]).astype(BF16)
    tm = h.shape[0]
    rowm = lax.broadcasted_iota(jnp.int32, (tm, wa_ref.shape[1]), 0) % 8
    zero8 = jnp.zeros((8, wa_ref.shape[1]), F32)
    ua, ya = _ffn_half(h, wa_ref, cwa_ref, cba_ref, ext_ref, zero8, (rowm, p1a_ref[...], p2a_ref[...]))
    uv, yv = _ffn_half(h, wv_ref, cwv_ref, cbv_ref, ext_ref, zero8, (rowm, p1v_ref[...], p2v_ref[...]))
    ua_ref[...] = ua
    uv_ref[...] = uv
    d = _dot((jax.nn.silu(ya) * yv).astype(BF16), wd_ref[...])

    @pl.when(c == 0)
    def _():
        acc_ref[...] = d

    @pl.when(c == pl.num_programs(0) - 1)
    def _():
        o_ref[...] = y_ref[...] + acc_ref[...] + d


def _ffn_sample(y, g, w_up, conv_w, conv_b, w_down, state):
    t = y.shape[0]
    nseq = t // 8
    fc = FF_CHUNK
    ncf = D_FF // fc
    z = jnp.zeros((nseq, 1, 2 * D_FF), F32)
    p1 = jnp.concatenate([state[:, 1:2]] + [z] * 7, axis=1).reshape(t, 2 * D_FF)
    p2 = jnp.concatenate([state[:, 0:1], state[:, 1:2]] + [z] * 6, axis=1).reshape(t, 2 * D_FF)
    full = lambda a: pl.BlockSpec(a.shape, lambda c: (0,) * a.ndim)
    ca = lambda r: pl.BlockSpec((r, fc), lambda c: (0, c))
    cv = lambda r: pl.BlockSpec((r, fc), lambda c: (0, ncf + c))
    y_out, ua, uv = pl.pallas_call(
        _ffn_sample_kernel,
        grid=(ncf,),
        in_specs=[full(y), full(g), ca(D_MODEL), cv(D_MODEL), ca(3), cv(3), ca(1), cv(1),
                  pl.BlockSpec((fc, D_MODEL), lambda c: (c, 0)), ca(t), cv(t), ca(t), cv(t)],
        out_specs=[pl.BlockSpec((t, D_MODEL), lambda c: (0, 0)), ca(t), ca(t)],
        out_shape=[jax.ShapeDtypeStruct((t, D_MODEL), F32), jax.ShapeDtypeStruct((t, D_FF), F32),
                   jax.ShapeDtypeStruct((t, D_FF), F32)],
        scratch_shapes=[pltpu.VMEM((t, D_MODEL), F32), pltpu.VMEM((t + 8, fc), F32)],
        compiler_params=_cparams("arbitrary"),
    )(y, g, w_up, w_up, conv_w, conv_w, conv_b, conv_b, w_down, p1, p1, p2, p2)
    u = jnp.concatenate([ua, uv], axis=-1).reshape(nseq, 8, 2 * D_FF)
    return y_out, u[:, 6:8]


KEY_CHUNK = 1024
NEW_PAD = 16


def _nsa_sample_kernel(pt_ref, qn_ref, gates_ref, rows_ref, winc_ref, winn_ref, scb_ref, sbt_ref, wbt_ref,
                       rmat_ref, eloc_ref, cmp_hbm, cache_hbm, o_ref, kvbuf, kcv, s_ref, m3_ref, sem, sem2):
    b = pl.program_id(0)
    npages = pt_ref.shape[1]
    ncs = 2 * npages
    nkeys = npages * PAGE_SIZE
    nch = nkeys // KEY_CHUNK
    ppc = KEY_CHUNK // PAGE_SIZE
    wbuf = winc_ref.shape[-1]

    def page_copy(p):
        return pltpu.make_async_copy(cache_hbm.at[pt_ref[b, p], pl.ds(2, 2)], kvbuf.at[p], sem)

    def cmp_copy(p):
        return pltpu.make_async_copy(cmp_hbm.at[pt_ref[b, p]], kcv.at[p], sem2)

    def start_all(p, c):
        cmp_copy(p).start()
        page_copy(p).start()
        return c

    lax.fori_loop(0, npages, start_all, 0)

    q = qn_ref[0].astype(F32)
    gts = gates_ref[0]
    lane8 = lax.broadcasted_iota(jnp.int32, (8, LANES), 1)
    half = [lane8 < 64, lane8 >= 64]
    pieces = [jnp.where(half[g], q[:, h * LANES:(h + 1) * LANES], 0.0) for g in range(2) for h in range(NSA_GROUP)]
    qrows = jnp.concatenate(pieces + [jnp.zeros((32, LANES), F32)], axis=0).astype(BF16)

    def wait_cmp(p, c):
        cmp_copy(p).wait()
        return c

    lax.fori_loop(0, npages, wait_cmp, 0)

    kc = jnp.concatenate([kcv[:, 0, 0:128], kcv[:, 1, 0:128]], axis=0).astype(BF16)
    vc = jnp.concatenate([kcv[:, 0, 128:256], kcv[:, 1, 128:256]], axis=0).astype(BF16)
    lc = _nt(kc, qrows) + scb_ref[...]
    e = jnp.exp(lc - jnp.max(lc, axis=0, keepdims=True))
    p_c = e / jnp.sum(e, axis=0, keepdims=True)
    o_c = _tn(p_c.astype(BF16), vc)
    hi = p_c.astype(BF16)
    lo = (p_c - hi.astype(F32)).astype(BF16)
    imp = _dot(hi, rmat_ref[...]) + _dot(lo, rmat_ref[...])

    def blk_of(r):
        return jnp.where(r < npages, 2 * r, jnp.where(r < ncs, 2 * (r - npages) + 1, r))

    blk = blk_of(lax.broadcasted_iota(jnp.int32, (ncs + 8, LANES), 0))
    impx = jnp.concatenate([imp, jnp.zeros((8, LANES), F32)], axis=0)
    forced = (blk == 0) | (blk == ncs) | (blk == ncs - 1)
    score = jnp.where(blk <= ncs, jnp.where(forced, FORCED_SCORE, impx), -1.0)
    blkf = blk.astype(F32)
    work, picked = score, jnp.zeros((ncs + 8, LANES), F32)
    for _ in range(SLC_TOPN):
        top = jnp.max(work, axis=0, keepdims=True)
        first = jnp.min(jnp.where(work == top, blkf, float(2 * ncs + 16)), axis=0, keepdims=True)
        hit = blkf == first
        picked = jnp.where(hit, 1.0, picked)
        work = jnp.where(hit, -2.0, work)
    madd = jnp.where((picked > 0.0) & (score >= 0.0), 0.0, -BIG)
    eye = jnp.where(lax.broadcasted_iota(jnp.int32, (LANES, LANES), 0)
                    == lax.broadcasted_iota(jnp.int32, (LANES, LANES), 1), 1.0, 0.0).astype(BF16)
    hb = ppc
    for t in range(nch):
        mt = jnp.concatenate([madd[t * hb:(t + 1) * hb], madd[npages + t * hb:npages + (t + 1) * hb],
                              jnp.zeros((LANES - 2 * hb, LANES), F32)], axis=0).astype(BF16)
        m3_ref[t] = _nt(eye, mt).astype(BF16)

    def wait_page(p, c):
        page_copy(p).wait()
        return c

    lax.fori_loop(0, npages, wait_page, 0)

    eloc = eloc_ref[...]
    pad8 = jnp.zeros((NEW_PAD - 8, LANES), F32)
    knew = jnp.concatenate([rows_ref[0, :, 256:384], pad8], axis=0).astype(BF16)
    vnew = jnp.concatenate([rows_ref[0, :, 384:512], pad8], axis=0).astype(BF16)
    kwn = jnp.concatenate([winn_ref[0, :, 0:128], pad8], axis=0).astype(BF16)
    vwn = jnp.concatenate([winn_ref[0, :, 128:256], pad8], axis=0).astype(BF16)
    kwt = winc_ref[0, 0].reshape(LANES, wbuf).astype(BF16)
    vwt = winc_ref[0, 1].reshape(LANES, wbuf).astype(BF16)
    near_pad = jnp.concatenate([jnp.zeros((LANES, KEY_CHUNK - PAGE_SIZE), F32), sbt_ref[:, 0:PAGE_SIZE]], axis=1)
    low = lax.broadcasted_iota(jnp.int32, (8, LANES), 1) < 64

    def page_rows(t, comp):
        tiles = [kvbuf[t * ppc + pp, comp].reshape(LANES, PAGE_SIZE) for pp in range(ppc)]
        return jnp.concatenate(tiles, axis=1).astype(BF16)

    def s_body(t, m):
        lhs = jnp.concatenate([m3_ref[t], qrows], axis=1)
        s = _dot(lhs, jnp.concatenate([eloc, page_rows(t, 0)], axis=0))
        s = s + jnp.where(t == nch - 1, 1.0, 0.0) * near_pad
        s_ref[t] = s
        return jnp.maximum(m, jnp.max(s, axis=1, keepdims=True))

    s_new = _nt(qrows, knew) + sbt_ref[:, PAGE_SIZE:PAGE_SIZE + NEW_PAD]
    m = lax.fori_loop(0, nch, s_body, jnp.max(s_new, axis=1, keepdims=True))
    e_new = jnp.exp(s_new - m)

    def pv_body(t, carry):
        l, acc = carry
        et = jnp.exp(s_ref[t] - m)
        return l + jnp.sum(et, axis=1, keepdims=True), acc + _nt(et.astype(BF16), page_rows(t, 1))

    l, acc = lax.fori_loop(0, nch, pv_body, (jnp.sum(e_new, axis=1, keepdims=True),
                                             _dot(e_new.astype(BF16), vnew)))
    o_s = acc / l

    sw = _dot(qrows, kwt) + wbt_ref[:, 0:wbuf]
    swn = _nt(qrows, kwn) + wbt_ref[:, wbuf:wbuf + NEW_PAD]
    mw = jnp.maximum(jnp.max(sw, axis=1, keepdims=True), jnp.max(swn, axis=1, keepdims=True))
    ew = jnp.exp(sw - mw)
    ewn = jnp.exp(swn - mw)
    o_w = (_nt(ew.astype(BF16), vwt) + _dot(ewn.astype(BF16), vwn)) / (
        jnp.sum(ew, axis=1, keepdims=True) + jnp.sum(ewn, axis=1, keepdims=True))

    def gate_col(br):
        cols = [gts[:, hh * 3 + br:hh * 3 + br + 1] for hh in range(NSA_HEADS)]
        return jnp.concatenate(cols + [jnp.zeros((32, 1), F32)], axis=0)

    o = gate_col(0) * o_c + gate_col(1) * o_s + gate_col(2) * o_w
    for h in range(NSA_GROUP):
        r0, r1 = h * 8, (NSA_GROUP + h) * 8
        o_ref[0, :, h * LANES:(h + 1) * LANES] = jnp.where(low, o[r0:r0 + 8], o[r1:r1 + 8]).astype(BF16)


def _nsa_sample_attn(page_table, qn, gates, rows, winc, winn, scb, sbt, wbt, rmat, eloc, cmp_phys, cache):
    b, npages = page_table.shape
    ncs = 2 * npages
    nkeys = npages * PAGE_SIZE
    assert nkeys % KEY_CHUNK == 0
    nch = nkeys // KEY_CHUNK
    full = lambda a: pl.BlockSpec(a.shape, lambda bi, pt: (0,) * a.ndim)
    per_b = lambda a: pl.BlockSpec((1,) + a.shape[1:], lambda bi, pt: (bi,) + (0,) * (a.ndim - 1))
    grid_spec = pltpu.PrefetchScalarGridSpec(
        num_scalar_prefetch=1,
        grid=(b,),
        in_specs=[per_b(qn), per_b(gates), per_b(rows), per_b(winc), per_b(winn),
                  full(scb), full(sbt), full(wbt), full(rmat), full(eloc),
                  pl.BlockSpec(memory_space=pl.ANY), pl.BlockSpec(memory_space=pl.ANY)],
        out_specs=pl.BlockSpec((1, 8, 768), lambda bi, pt: (bi, 0, 0)),
        scratch_shapes=[pltpu.VMEM((npages, 2, 2, 64, PAGE_SIZE), F32), pltpu.VMEM((npages, 2, 256), F32),
                        pltpu.VMEM((nch, LANES, KEY_CHUNK), F32), pltpu.VMEM((nch, LANES, LANES), BF16),
                        pltpu.SemaphoreType.DMA(()), pltpu.SemaphoreType.DMA(())],
    )
    return pl.pallas_call(
        _nsa_sample_kernel,
        grid_spec=grid_spec,
        out_shape=jax.ShapeDtypeStruct((b, 8, 768), BF16),
        compiler_params=_cparams("arbitrary"),
    )(page_table, qn, gates, rows, winc, winn, scb, sbt, wbt, rmat, eloc, cmp_phys, cache)


def _block_diag2(w):
    z = jnp.zeros_like(w)
    return jnp.concatenate([jnp.concatenate([w, z], axis=-1), jnp.concatenate([z, w], axis=-1)], axis=-2)


def _q_slot_perm():
    return np.concatenate([np.r_[h * 64:(h + 1) * 64, (NSA_GROUP + h) * 64:(NSA_GROUP + h + 1) * 64]
                           for h in range(NSA_GROUP)])


def _sample_codes(ncs, wbuf):
    ql = np.arange(8)[None, :]
    c = np.arange(ncs)[:, None]
    scb = _bucket_np((ncs - c) * CMP_BLOCK + ql - (CMP_BLOCK - 1))
    kap = np.arange(PAGE_SIZE + NEW_PAD)[:, None]
    d = np.where(kap < PAGE_SIZE, PAGE_SIZE + ql - kap, ql - (kap - PAGE_SIZE))
    sbt = np.where((d >= 0) & (kap < PAGE_SIZE + 8), _bucket_np(d), -1)
    j = np.arange(wbuf + NEW_PAD)[:, None]
    d = wbuf + ql - j
    wbt = np.where((d >= 0) & (d <= WINDOW) & (j < wbuf + 8), _bucket_np(d), -1)
    return np.concatenate([scb, sbt, wbt], axis=0).astype(np.int32)


def kernel(x_prompt, x_sample, cache_nsa_kv, cache_nsa_win, cache_mem_kv, state_pool, state_ffn_conv,
           page_table, mem_prompt, rel_bias, g_mix, g_mem, g_ffn, w_in_nsa, q_gain_nsa, k_gain_nsa,
           cmp_pos, cmp_w1, cmp_b1, cmp_w2, cmp_b2, w_in_pool, w_pool_grp, pool_scale, w_mem_kv,
           xq_gain, xk_gain, w_out, w_up, conv_w, conv_b, w_down):
    bp, lp, _ = x_prompt.shape
    bs, ls, _ = x_sample.shape
    assert ls == 8 and lp % 512 == 0 and lp // 64 <= 64
    npages = page_table.shape[1]
    past_len = npages * PAGE_SIZE
    n_phys = cache_nsa_kv.shape[1]
    wbuf = cache_nsa_win.shape[2]
    nm = mem_prompt.shape[1]
    tp_, ts_ = bp * lp, bs * ls
    nc = lp // 64
    ncs = 2 * npages
    scale = HEAD_DIM ** -0.5
    tile2 = lambda v: jnp.tile(v, 2)[None, :]

    a = np.arange(LANES)
    seg = jnp.asarray(a[:, None] // 64 == a[None, :] // 64, BF16)
    perm = _q_slot_perm()

    wb = _bias_table(rel_bias, _window_codes()).reshape(2, NSA_GROUP * 64, WINDOW + 64)
    ct = _bias_table(rel_bias, _cmp_codes()).reshape(2, NSA_GROUP, 8, 64)
    ct = ct.transpose(0, 2, 1, 3).reshape(2, 8, NSA_GROUP * 64)
    amat = np.zeros((2, NSA_GROUP * 64, LANES), np.float32)
    for g in range(2):
        for h in range(NSA_GROUP):
            amat[g, h * 64 + np.arange(64), g * 64 + np.arange(64)] = 1.0
    amat = jnp.asarray(amat, BF16)
    r = np.arange(PAD_ROWS + lp)
    oh_np = np.zeros((PAD_ROWS + lp, LANES), np.float32)
    oh_np[r, np.where(r < PAD_ROWS, 64, (r - PAD_ROWS) // 64)] = 1.0
    oh = jnp.asarray(oh_np, BF16)
    stab = _bias_table(rel_bias, _sample_codes(ncs, wbuf))
    stab = jnp.pad(stab.transpose(1, 0, 2).reshape(-1, NSA_HEADS * 8), ((0, 0), (0, LANES - NSA_HEADS * 8)))
    ns = PAGE_SIZE + NEW_PAD
    scb, sbt, wbt = stab[:ncs], stab[ncs:ncs + ns].T, stab[ncs + ns:].T
    scb = jnp.concatenate([scb[0::2], scb[1::2]], axis=0)
    ppc = KEY_CHUNK // PAGE_SIZE
    eloc_np = np.zeros((LANES, KEY_CHUNK), np.float32)
    for pp in range(ppc):
        eloc_np[pp, pp * PAGE_SIZE:pp * PAGE_SIZE + 64] = 1.0
        eloc_np[ppc + pp, pp * PAGE_SIZE + 64:(pp + 1) * PAGE_SIZE] = 1.0
    eloc = jnp.asarray(eloc_np, BF16)
    lam = np.arange(LANES)
    rm = (lam[:, None] < 96) & (lam[None, :] < 96) & (lam[:, None] // 48 == lam[None, :] // 48) \
        & (lam[:, None] % 8 == lam[None, :] % 8)
    rmat = jnp.asarray(rm, BF16)

    yp = x_prompt.reshape(tp_, D_MODEL)
    ys = x_sample.reshape(ts_, D_MODEL)
    tm = 512
    outs = {k: [] for k in ("kv_p", "kv_s", "win_p", "win_s", "mem_p", "pool_p", "pool_s", "conv_p", "conv_s")}
    depth = g_mix.shape[0]
    for i in range(depth):
        j = i // 2
        wo = w_out[i]
        xg = tile2(xq_gain[i]) * scale
        mkv_p = _memkv(mem_prompt.reshape(bp * nm, D_MODEL), g_mem[i][None], w_mem_kv[i].astype(BF16), seg,
                       tile2(xk_gain[i]), min(tm, bp * nm)).reshape(bp, nm, 2 * XATTN_WIDTH)
        outs["mem_p"].append(mkv_p.reshape(bp, nm, 2, XATTN_HEADS, HEAD_DIM))
        mkv_s = cache_mem_kv[i].reshape(bs, nm, 2 * XATTN_WIDTH)
        if i % 2 == 0:
            w = w_in_nsa[j]
            w = jnp.concatenate([w[:, perm], w[:, 768:1536], w[:, 1572:1828], w[:, 1536:1572],
                                 jnp.zeros((D_MODEL, NSA_IN_COLS - 1828), F32)], axis=1).astype(BF16)
            qg = tile2(q_gain_nsa[j]) * scale
            kg = tile2(k_gain_nsa[j])
            cmp_args = (jnp.tile(cmp_pos[j], (1, 1, 2))[:, :, None, :], _block_diag2(cmp_w1[j]).astype(BF16),
                        jnp.tile(cmp_b1[j], (1, 2))[:, None, :], _block_diag2(cmp_w2[j]).astype(BF16),
                        jnp.tile(cmp_b2[j], (1, 2))[:, None, :], kg, seg)
            wo_mix = wo[:768][perm].astype(BF16)
            qn, rows, win, katt, xqn_p, gates = _nsa_inproj(yp, g_mix[i][None], w, seg, qg, kg, xg, tm)
            cmp_p = _compress(rows.reshape(tp_ // 64, 64, 512), *cmp_args, nbt=512).reshape(bp, nc, 256)
            kvatt = jnp.pad(katt.reshape(bp, lp, 512), ((0, 0), (PAD_ROWS, 0), (0, 0)))
            mix_p = _nsa_prompt_attn(qn.reshape(bp, lp, 768), gates.reshape(bp, lp, 128), cmp_p, kvatt,
                                     oh, wb, ct, amat).reshape(tp_, 768)
            outs["kv_p"].append(rows.reshape(bp, lp, 4, NSA_KV_HEADS, HEAD_DIM))
            outs["win_p"].append(win.reshape(bp, lp, 2, NSA_KV_HEADS, HEAD_DIM)[:, -min(WINDOW, lp):])
            qn, rows, win, _, xqn_s, gates = _nsa_inproj(ys, g_mix[i][None], w, seg, qg, kg, xg, ts_)
            cache_t = cache_nsa_kv[j].transpose(0, 2, 3, 4, 1)
            w2 = cmp_w2[j]
            w2sel = jnp.zeros((2, 2, 2, 2 * CMP_HIDDEN, LANES), F32)
            for blk in range(2):
                for g in range(2):
                    w2sel = w2sel.at[:, blk, g, blk * CMP_HIDDEN:(blk + 1) * CMP_HIDDEN,
                                     g * HEAD_DIM:(g + 1) * HEAD_DIM].set(w2)
            cmp_phys = _compress_cache(
                cache_t, jnp.tile(cmp_pos[j].transpose(0, 2, 1), (1, 1, 2))[:, :, None, :],
                _block_diag2(cmp_w1[j].transpose(0, 2, 1, 3)).astype(BF16).reshape(2, 32, 2 * LANES, 256),
                cmp_args[2], w2sel.astype(BF16),
                cmp_args[4], kg, seg, 512)
            winc = cache_nsa_win[j].transpose(0, 2, 3, 4, 1)
            mix_s = _nsa_sample_attn(page_table, qn.reshape(bs, ls, 768), gates.reshape(bs, ls, 128),
                                     rows.reshape(bs, ls, 512), winc, win.reshape(bs, ls, 256),
                                     scb, sbt, wbt, rmat, eloc, cmp_phys.reshape(n_phys, 2, 256),
                                     cache_t).reshape(ts_, 768)
            outs["kv_s"].append(rows.reshape(bs, ls, 4, NSA_KV_HEADS, HEAD_DIM))
            wall = jnp.concatenate([cache_nsa_win[j], win.reshape(bs, ls, 2, NSA_KV_HEADS, HEAD_DIM)], axis=1)
            outs["win_s"].append(wall[:, -wbuf:])
        else:
            w = w_in_pool[j].astype(BF16)
            wbd = jnp.zeros((POOL_WIDTH, POOL_WIDTH), F32)
            for gi in range(len(POOL_WINDOWS)):
                sl = slice(gi * POOL_GROUP_DIM, (gi + 1) * POOL_GROUP_DIM)
                wbd = wbd.at[sl, sl].set(w_pool_grp[j, gi])
            wbd = wbd.astype(BF16)
            psc = pool_scale[j][None]
            wo_mix = wo[:768].astype(BF16)
            u_p, xqn_p = _pool_inproj(yp, g_mix[i][None], w, seg, xg, tm)
            u_p3 = u_p.reshape(bp, lp, POOL_WIDTH)
            mix_p = _pool_mix(u_p3, jnp.zeros((bp, 16, POOL_WIDTH), F32), wbd, psc, tm, 0).reshape(tp_, 768)
            outs["pool_p"].append(u_p3[:, -POOL_STATE:])
            u_s, xqn_s = _pool_inproj(ys, g_mix[i][None], w, seg, xg, ts_)
            u_s3 = u_s.reshape(bs, ls, POOL_WIDTH)
            prev16 = jnp.concatenate([jnp.zeros((bs, 1, POOL_WIDTH), F32), state_pool[j]], axis=1)
            mix_s = _pool_mix(u_s3, prev16, wbd, psc, ls, past_len).reshape(ts_, 768)
            outs["pool_s"].append(jnp.concatenate([state_pool[j], u_s3], axis=1)[:, -POOL_STATE:])
        wo_att = wo[768:].astype(BF16)
        att_p = _mem_attn(xqn_p.reshape(bp, lp, 256), mkv_p, tm).reshape(tp_, 256)
        att_s = _mem_attn(xqn_s.reshape(bs, ls, 256), mkv_s, ls).reshape(ts_, 256)
        yp = _outproj(yp, mix_p, att_p, wo_mix, wo_att, tm)
        ys = _outproj(ys, mix_s, att_s, wo_mix, wo_att, ts_)
        wu, wd = w_up[i].astype(BF16), w_down[i].astype(BF16)
        yp, c_p = _ffn_prompt(yp, g_ffn[i][None], wu, conv_w[i], conv_b[i][None], wd, tm, lp)
        ys, c_s = _ffn_sample(ys, g_ffn[i][None], wu, conv_w[i], conv_b[i][None], wd, state_ffn_conv[i])
        outs["conv_p"].append(c_p)
        outs["conv_s"].append(c_s)
    st = lambda k: jnp.stack(outs[k])
    return (yp.reshape(bp, lp, D_MODEL), ys.reshape(bs, ls, D_MODEL), st("kv_p"), st("kv_s"), st("win_p"),
            st("win_s"), st("mem_p"), st("pool_p"), st("pool_s"), st("conv_p"), st("conv_s"))
```

```python
import functools
import math

import numpy as np
import jax
import jax.numpy as jnp
from jax import lax
from jax.experimental import pallas as pl
from jax.experimental.pallas import tpu as pltpu

F32 = jnp.float32
BF16 = jnp.bfloat16

D_MODEL = 1024
PAGE_SIZE = 128
HEAD_DIM = 64
NSA_HEADS = 12
NSA_KV_HEADS = 2
NSA_GROUP = NSA_HEADS // NSA_KV_HEADS
NSA_WIDTH = NSA_HEADS * HEAD_DIM
NSA_KV_WIDTH = NSA_KV_HEADS * HEAD_DIM
CMP_BLOCK = 64
CMP_HIDDEN = 128
SLC_TOPN = 16
WINDOW = 512
FORCED_SCORE = 1e4
XATTN_HEADS = 4
XATTN_WIDTH = XATTN_HEADS * HEAD_DIM
POOL_WINDOWS = (2, 4, 8, 16)
POOL_GROUP_DIM = 192
POOL_WIDTH = len(POOL_WINDOWS) * POOL_GROUP_DIM
POOL_STATE = max(POOL_WINDOWS) - 1
D_FF = 2816
REL_BUCKETS = 32
REL_MAX_DIST = 128
EPS = 1e-6
BIG = 1e30

LANES = 128
WIN_KEYS = WINDOW + 2 * 64
NEAR_KEYS = 4 * 64
PAD_ROWS = WIN_KEYS
NSA_IN_COLS = 1920
FF_CHUNK = D_FF // 2
VMEM_LIMIT = 56 * 1024 * 1024


def _cparams(*sem):
    return pltpu.CompilerParams(dimension_semantics=sem, vmem_limit_bytes=VMEM_LIMIT)


def _nt(a, b):
    return lax.dot_general(a, b, (((1,), (1,)), ((), ())), preferred_element_type=F32)


def _tn(a, b):
    return lax.dot_general(a, b, (((0,), (0,)), ((), ())), preferred_element_type=F32)


def _dot(a, b):
    return jnp.dot(a, b, preferred_element_type=F32)


def _rms_rows(x, g):
    return x * lax.rsqrt(jnp.mean(x * x, axis=-1, keepdims=True) + EPS) * g


def _seg_rms(zc, seg, gain):
    zz = zc * zc
    hi = zz.astype(BF16)
    lo = (zz - hi.astype(F32)).astype(BF16)
    ms = (_dot(hi, seg) + _dot(lo, seg)) * (1.0 / HEAD_DIM)
    return zc * lax.rsqrt(ms + EPS) * gain


def _nsa_inproj_kernel(x_ref, g_ref, w_ref, seg_ref, qg_ref, kg_ref, xg_ref,
                       qn_ref, rows_ref, win_ref, katt_ref, xqn_ref, gates_ref):
    xn = _rms_rows(x_ref[...], g_ref[...])
    z = _dot(xn.astype(BF16), w_ref[...])
    seg = seg_ref[...]
    for h in range(6):
        qn_ref[:, h * LANES:(h + 1) * LANES] = _seg_rms(z[:, h * LANES:(h + 1) * LANES], seg, qg_ref[...]).astype(BF16)
    kv = 768
    rows_ref[:, 0:256] = z[:, kv:kv + 256]
    ks = _seg_rms(z[:, kv + 256:kv + 384], seg, kg_ref[...])
    vs = z[:, kv + 384:kv + 512]
    kw = _seg_rms(z[:, kv + 512:kv + 640], seg, kg_ref[...])
    vw = z[:, kv + 640:kv + 768]
    rows_ref[:, 256:384] = ks
    rows_ref[:, 384:512] = vs
    win_ref[:, 0:128] = kw
    win_ref[:, 128:256] = vw
    katt_ref[:, 0:128] = ks.astype(BF16)
    katt_ref[:, 128:256] = vs.astype(BF16)
    katt_ref[:, 256:384] = kw.astype(BF16)
    katt_ref[:, 384:512] = vw.astype(BF16)
    for h in range(2):
        c0 = 1536 + h * LANES
        xqn_ref[:, h * LANES:(h + 1) * LANES] = _seg_rms(z[:, c0:c0 + LANES], seg, xg_ref[...]).astype(BF16)
    gates_ref[...] = jax.nn.sigmoid(z[:, 1792:1920])


def _nsa_inproj_t_kernel(x_ref, g_ref, w_ref, wt_ref, seg_ref, qg_ref, kg_ref, kgc_ref, xg_ref,
                         qn_ref, rowst_ref, wint_ref, katt_ref, xqn_ref, gates_ref):
    xn = _rms_rows(x_ref[...], g_ref[...]).astype(BF16)
    z = _dot(xn, w_ref[...])
    zt = _nt(wt_ref[...], xn)

    def norm_t(a):
        parts = []
        for gi in range(2):
            ag = a[gi * HEAD_DIM:(gi + 1) * HEAD_DIM]
            parts.append(ag * lax.rsqrt(jnp.mean(ag * ag, axis=0, keepdims=True) + EPS))
        return jnp.concatenate(parts, axis=0) * kgc_ref[...]

    rowst_ref[0, 0:256, :] = zt[0:256]
    rowst_ref[0, 256:384, :] = norm_t(zt[256:384])
    rowst_ref[0, 384:512, :] = zt[384:512]
    wint_ref[0, 0:128, :] = norm_t(zt[512:640])
    wint_ref[0, 128:256, :] = zt[640:768]
    seg = seg_ref[...]
    for h in range(6):
        qn_ref[:, h * LANES:(h + 1) * LANES] = _seg_rms(z[:, h * LANES:(h + 1) * LANES], seg, qg_ref[...]).astype(BF16)
    kv = 768
    katt_ref[:, 0:128] = _seg_rms(z[:, kv + 256:kv + 384], seg, kg_ref[...]).astype(BF16)
    katt_ref[:, 128:256] = z[:, kv + 384:kv + 512].astype(BF16)
    katt_ref[:, 256:384] = _seg_rms(z[:, kv + 512:kv + 640], seg, kg_ref[...]).astype(BF16)
    katt_ref[:, 384:512] = z[:, kv + 640:kv + 768].astype(BF16)
    for h in range(2):
        c0 = 1536 + h * LANES
        xqn_ref[:, h * LANES:(h + 1) * LANES] = _seg_rms(z[:, c0:c0 + LANES], seg, xg_ref[...]).astype(BF16)
    gates_ref[...] = jax.nn.sigmoid(z[:, 1792:1920])


def _nsa_inproj_t(x, g, w, wt, seg, qg, kg, kgc, xg, tm, seq_len):
    t = x.shape[0]
    tps = seq_len // tm
    full = lambda a: pl.BlockSpec(a.shape, lambda i: (0,) * a.ndim)
    row = lambda n: pl.BlockSpec((tm, n), lambda i: (i, 0))
    col = lambda n: pl.BlockSpec((1, n, tm), lambda i: (i // tps, 0, i % tps))
    return pl.pallas_call(
        _nsa_inproj_t_kernel,
        grid=(t // tm,),
        in_specs=[row(D_MODEL), full(g), full(w), full(wt), full(seg), full(qg), full(kg), full(kgc), full(xg)],
        out_specs=[row(768), col(512), col(256), row(512), row(256), row(128)],
        out_shape=[jax.ShapeDtypeStruct((t, 768), BF16), jax.ShapeDtypeStruct((t // seq_len, 512, seq_len), F32),
                   jax.ShapeDtypeStruct((t // seq_len, 256, seq_len), F32), jax.ShapeDtypeStruct((t, 512), BF16),
                   jax.ShapeDtypeStruct((t, 256), BF16), jax.ShapeDtypeStruct((t, 128), F32)],
        compiler_params=_cparams("parallel"),
    )(x, g, w, wt, seg, qg, kg, kgc, xg)


def _nsa_inproj(x, g, w, seg, qg, kg, xg, tm):
    t = x.shape[0]
    full = lambda a: pl.BlockSpec(a.shape, lambda i: (0,) * a.ndim)
    row = lambda n: pl.BlockSpec((tm, n), lambda i: (i, 0))
    return pl.pallas_call(
        _nsa_inproj_kernel,
        grid=(t // tm,),
        in_specs=[row(D_MODEL), full(g), full(w), full(seg), full(qg), full(kg), full(xg)],
        out_specs=[row(768), row(512), row(256), row(512), row(256), row(128)],
        out_shape=[jax.ShapeDtypeStruct((t, 768), BF16), jax.ShapeDtypeStruct((t, 512), F32),
                   jax.ShapeDtypeStruct((t, 256), F32), jax.ShapeDtypeStruct((t, 512), BF16),
                   jax.ShapeDtypeStruct((t, 256), BF16), jax.ShapeDtypeStruct((t, 128), F32)],
        compiler_params=_cparams("parallel"),
    )(x, g, w, seg, qg, kg, xg)


def _pool_inproj_kernel(x_ref, g_ref, w_ref, seg_ref, xg_ref, u_ref, xqn_ref):
    xn = _rms_rows(x_ref[...], g_ref[...])
    z = _dot(xn.astype(BF16), w_ref[...])
    u_ref[...] = z[:, 0:POOL_WIDTH]
    seg = seg_ref[...]
    for h in range(2):
        c0 = POOL_WIDTH + h * LANES
        xqn_ref[:, h * LANES:(h + 1) * LANES] = _seg_rms(z[:, c0:c0 + LANES], seg, xg_ref[...]).astype(BF16)


def _pool_inproj(x, g, w, seg, xg, tm):
    t = x.shape[0]
    full = lambda a: pl.BlockSpec(a.shape, lambda i: (0,) * a.ndim)
    row = lambda n: pl.BlockSpec((tm, n), lambda i: (i, 0))
    return pl.pallas_call(
        _pool_inproj_kernel,
        grid=(t // tm,),
        in_specs=[row(D_MODEL), full(g), full(w), full(seg), full(xg)],
        out_specs=[row(POOL_WIDTH), row(256)],
        out_shape=[jax.ShapeDtypeStruct((t, POOL_WIDTH), F32), jax.ShapeDtypeStruct((t, 256), BF16)],
        compiler_params=_cparams("parallel"),
    )(x, g, w, seg, xg)


def _memkv_kernel(x_ref, g_ref, w_ref, seg_ref, kg_ref, o_ref):
    xn = _rms_rows(x_ref[...], g_ref[...])
    z = _dot(xn.astype(BF16), w_ref[...])
    seg = seg_ref[...]
    for h in range(2):
        o_ref[:, h * LANES:(h + 1) * LANES] = _seg_rms(z[:, h * LANES:(h + 1) * LANES], seg, kg_ref[...])
    o_ref[:, 256:512] = z[:, 256:512]


def _memkv(x, g, w, seg, kg, tm):
    t = x.shape[0]
    full = lambda a: pl.BlockSpec(a.shape, lambda i: (0,) * a.ndim)
    row = lambda n: pl.BlockSpec((tm, n), lambda i: (i, 0))
    return pl.pallas_call(
        _memkv_kernel,
        grid=(t // tm,),
        in_specs=[row(D_MODEL), full(g), full(w), full(seg), full(kg)],
        out_specs=row(512),
        out_shape=jax.ShapeDtypeStruct((t, 512), F32),
        compiler_params=_cparams("parallel"),
    )(x, g, w, seg, kg)


def _compress_kernel(x_ref, pos_ref, w1_ref, b1_ref, w2_ref, b2_ref, kg_ref, seg_ref, o_ref, acc_ref):
    lc = pl.program_id(1)

    @pl.when(lc == 0)
    def _():
        acc_ref[...] = jnp.zeros_like(acc_ref)

    for comp in range(2):
        part = None
        for l in range(8):
            xl = x_ref[:, l, comp * LANES:(comp + 1) * LANES] + pos_ref[comp, l]
            d = _dot(xl.astype(BF16), w1_ref[comp, l])
            part = d if part is None else part + d
        acc_ref[:, comp * 256:(comp + 1) * 256] += part

    @pl.when(lc == pl.num_programs(1) - 1)
    def _():
        for comp in range(2):
            h = jax.nn.gelu(acc_ref[:, comp * 256:(comp + 1) * 256] + b1_ref[comp])
            o = _dot(h.astype(BF16), w2_ref[comp]) + b2_ref[comp]
            if comp == 0:
                o = _seg_rms(o, seg_ref[...], kg_ref[...])
            o_ref[:, comp * LANES:(comp + 1) * LANES] = o


def _compress(x3, pos_t, w1bd, b1t, w2bd, b2t, kg, seg, nbt):
    nb, _, w = x3.shape
    nbt = min(nbt, nb)
    assert nb % nbt == 0
    full = lambda a: pl.BlockSpec(a.shape, lambda j, l: (0,) * a.ndim)
    return pl.pallas_call(
        _compress_kernel,
        grid=(nb // nbt, 8),
        in_specs=[pl.BlockSpec((nbt, 8, 256), lambda j, l: (j, l, 0)),
                  pl.BlockSpec((2, 8, 1, LANES), lambda j, l: (0, l, 0, 0)),
                  pl.BlockSpec((2, 8, LANES, 256), lambda j, l: (0, l, 0, 0)),
                  full(b1t), full(w2bd), full(b2t), full(kg), full(seg)],
        out_specs=pl.BlockSpec((nbt, 256), lambda j, l: (j, 0)),
        out_shape=jax.ShapeDtypeStruct((nb, 256), F32),
        scratch_shapes=[pltpu.VMEM((nbt, 512), F32)],
        compiler_params=_cparams("parallel", "arbitrary"),
    )(x3, pos_t, w1bd, b1t, w2bd, b2t, kg, seg)


def _compress_cache_kernel(x_ref, pos_ref, w1_ref, b1_ref, w2_ref, b2_ref, kg_ref, seg_ref, o_ref, acc_ref):
    dc = pl.program_id(1)

    @pl.when(dc == 0)
    def _():
        acc_ref[...] = jnp.zeros_like(acc_ref)

    pt = x_ref.shape[0]
    x2 = x_ref.reshape(pt * 32, LANES)

    def feature_rows(comp, g, dd):
        return (x2[pl.ds((comp * 2 + g) * 8 + dd, pt, stride=32), :] + pos_ref[comp, dd]).astype(BF16)

    for comp in range(2):
        for g in range(2):
            part = None
            for dp in range(4):
                xd = jnp.concatenate([feature_rows(comp, g, 2 * dp), feature_rows(comp, g, 2 * dp + 1)], axis=1)
                d = _dot(xd, w1_ref[comp, dp])
                part = d if part is None else part + d
            c0 = (comp * 2 + g) * 256
            acc_ref[:, c0:c0 + 256] += part

    @pl.when(dc == pl.num_programs(1) - 1)
    def _():
        for comp in range(2):
            hid = [jax.nn.gelu(acc_ref[:, (comp * 2 + g) * 256:(comp * 2 + g + 1) * 256] + b1_ref[comp]).astype(BF16)
                   for g in range(2)]
            for blk in range(2):
                o = _dot(hid[0], w2_ref[comp, blk, 0]) + _dot(hid[1], w2_ref[comp, blk, 1]) + b2_ref[comp]
                if comp == 0:
                    o = _seg_rms(o, seg_ref[...], kg_ref[...])
                o_ref[:, blk * 256 + comp * LANES:blk * 256 + (comp + 1) * LANES] = o


def _compress_cache(xt, pos_t, w1t, b1t, w2sel, b2t, kg, seg, pt):
    n_phys = xt.shape[0]
    pt = min(pt, n_phys)
    assert n_phys % pt == 0
    full = lambda a: pl.BlockSpec(a.shape, lambda j, d: (0,) * a.ndim)
    return pl.pallas_call(
        _compress_cache_kernel,
        grid=(n_phys // pt, 8),
        in_specs=[pl.BlockSpec((pt, 2, 2, 8, LANES), lambda j, d: (j, 0, 0, d, 0)),
                  pl.BlockSpec((2, 8, 1, LANES), lambda j, d: (0, d, 0, 0)),
                  pl.BlockSpec((2, 4, 2 * LANES, 256), lambda j, d: (0, d, 0, 0)),
                  full(b1t), full(w2sel), full(b2t), full(kg), full(seg)],
        out_specs=pl.BlockSpec((pt, 512), lambda j, d: (j, 0)),
        out_shape=jax.ShapeDtypeStruct((n_phys, 512), F32),
        scratch_shapes=[pltpu.VMEM((pt, 1024), F32)],
        compiler_params=_cparams("parallel", "arbitrary"),
    )(xt, pos_t, w1t, b1t, w2sel, b2t, kg, seg)


def _bucket_np(d):
    n = np.maximum(d, 0)
    max_exact = REL_BUCKETS // 2
    nf = np.maximum(n, 1).astype(np.float32)
    large = max_exact + (np.log(nf / max_exact) / math.log(REL_MAX_DIST / max_exact)
                         * (REL_BUCKETS - max_exact)).astype(np.int32)
    large = np.minimum(large, REL_BUCKETS - 1)
    return np.where(n < max_exact, n, large).astype(np.int32)


def _bias_table_kernel(rb_ref, code_ref, o_ref):
    code = code_ref[...]
    for h in range(NSA_HEADS):
        far = rb_ref[REL_BUCKETS - 1, h]
        acc = jnp.full(code.shape, -BIG, F32)
        for k in range(REL_BUCKETS):
            acc = jnp.where(code == k, rb_ref[k, h] - far, acc)
        o_ref[h] = acc


def _bias_table(rel_bias, code):
    code = jnp.asarray(code, jnp.int32)
    return pl.pallas_call(
        _bias_table_kernel,
        in_specs=[pl.BlockSpec(memory_space=pltpu.SMEM), pl.BlockSpec(code.shape, lambda: (0, 0))],
        out_specs=pl.BlockSpec((NSA_HEADS,) + code.shape, lambda: (0, 0, 0)),
        out_shape=jax.ShapeDtypeStruct((NSA_HEADS,) + code.shape, F32),
    )(rel_bias, code)


def _window_codes():
    ql = np.arange(64)[:, None]
    j = np.arange(WIN_KEYS)[None, :]
    d = WIN_KEYS - 64 + ql - j
    return np.where((d >= 0) & (d <= WINDOW), _bucket_np(d), -1).astype(np.int32)


def _cmp_codes():
    delta = np.arange(8)[:, None]
    ql = np.arange(64)[None, :]
    d = delta * CMP_BLOCK + ql - (CMP_BLOCK - 1)
    return np.where(d >= 0, _bucket_np(d), -1).astype(np.int32)


FAR_TILE = 512


def _nsa_prompt_kernel(qn_ref, gates_ref, cmp_ref, kv_ref, oh_ref, wb_ref, nb_ref, ct_ref, amat_ref, o_ref, s_ref):
    i = pl.program_id(1)
    nc = cmp_ref.shape[1]
    rows = NSA_GROUP * 64
    q = qn_ref[0].astype(F32)
    gts = gates_ref[0]
    kc = cmp_ref[0, :, 0:128].astype(BF16)
    vc = cmp_ref[0, :, 128:256].astype(BF16)
    lane = lax.broadcasted_iota(jnp.int32, (64, LANES), 1)
    lane_r = lax.broadcasted_iota(jnp.int32, (rows, LANES), 1)
    half = [lane < 64, lane >= 64]
    qg = [jnp.concatenate([jnp.where(half[g], q[:, h * LANES:(h + 1) * LANES], 0.0) for h in range(NSA_GROUP)],
                          axis=0).astype(BF16) for g in range(2)]

    o_c, imp = [], []
    blk_r = lax.broadcasted_iota(jnp.int32, (nc, rows), 0)
    for g in range(2):
        ctg = ct_ref[g]
        bias = jnp.where(blk_r == i, ctg[0:1], jnp.where(blk_r == i - 1, ctg[1:2], jnp.where(
            blk_r == i - 2, ctg[2:3], jnp.where(blk_r > i, -BIG, 0.0))))
        lc = _nt(kc, qg[g]) + bias
        e = jnp.exp(lc - jnp.max(lc, axis=0, keepdims=True))
        p = jnp.where(bias > -0.5 * BIG, e / jnp.sum(e, axis=0, keepdims=True), 0.0)
        o_c.append(_tn(p.astype(BF16), vc))
        s3 = p[:, 0:128] + p[:, 128:256] + p[:, 256:384]
        imp.append(s3 + pltpu.roll(s3, 64, axis=1))
    lane_c = lax.broadcasted_iota(jnp.int32, (nc, LANES), 1)
    blk = lax.broadcasted_iota(jnp.int32, (nc, LANES), 0)
    impp = jnp.where(lane_c < 64, imp[0], imp[1])
    forced = (blk == 0) | (blk == i) | (blk == i - 1)
    score = jnp.where(blk <= i, jnp.where(forced, FORCED_SCORE, impp), -1.0)
    sc8 = [score[r * 8:(r + 1) * 8] for r in range(nc // 8)]
    blk8 = lax.broadcasted_iota(jnp.int32, (8, LANES), 0)
    rank8 = [jnp.zeros((8, LANES), F32) for _ in sc8]
    for cp in range(nc):
        row = score[cp:cp + 1, :]
        for r in range(nc // 8):
            if r * 8 > cp:
                beats = row >= sc8[r]
            elif r * 8 + 7 < cp:
                beats = row > sc8[r]
            else:
                beats = (row > sc8[r]) | ((row == sc8[r]) & (blk8 + r * 8 > cp))
            rank8[r] = rank8[r] + jnp.where(beats, 1.0, 0.0)
    rank = jnp.concatenate(rank8, axis=0)
    notsel =jnp.where((rank < float(min(SLC_TOPN, nc))) & (score >= 0.0), 0.0, 1.0)
    notsel_far = jnp.where(blk > i - 3, 1.0, notsel)
    tail = jnp.where(lax.broadcasted_iota(jnp.int32, (LANES - nc, LANES), 0) == 64 - nc, 1.0, 0.0)
    ns_near = jnp.concatenate([notsel, tail], axis=0).astype(BF16)
    ns_far = jnp.concatenate([notsel_far, tail], axis=0).astype(BF16)
    m_win = jnp.where(lane_r == 64, -BIG, 0.0).astype(BF16)

    lhs_far, lhs_near, lhs_win = [], [], []
    for g in range(2):
        a = amat_ref[g]
        lhs_near.append(jnp.concatenate([qg[g], (_nt(a, ns_near) * -BIG).astype(BF16)], axis=1))
        lhs_far.append(jnp.concatenate([qg[g], (_nt(a, ns_far) * -BIG).astype(BF16)], axis=1))
        lhs_win.append(jnp.concatenate([qg[g], m_win], axis=1))

    def keys_aug(r0, n, c0):
        return jnp.concatenate([kv_ref[0, pl.ds(r0, n), c0:c0 + LANES], oh_ref[pl.ds(r0, n), :]], axis=1)

    def fold(x, op):
        return functools.reduce(op, [x[:, c * LANES:(c + 1) * LANES] for c in range(x.shape[1] // LANES)])

    ntiles = (i + 5) // (FAR_TILE // 64)
    r0n = pl.multiple_of(PAD_ROWS + (i - 3) * 64, 64)
    kn = keys_aug(r0n, NEAR_KEYS, 0)
    s_near = [_nt(lhs_near[g], kn) + nb_ref[g] for g in range(2)]

    def pass_a(t, mrun):
        k = keys_aug(pl.multiple_of(PAD_ROWS + t * FAR_TILE, LANES), FAR_TILE, 0)
        out = []
        for g in range(2):
            s = _nt(lhs_far[g], k)
            s_ref[g, t] = s
            out.append(jnp.maximum(mrun[g], fold(s, jnp.maximum)))
        return tuple(out)

    mrun = lax.fori_loop(0, ntiles, pass_a, (jnp.full((rows, LANES), -BIG, F32),) * 2)
    m = [jnp.max(jnp.maximum(mrun[g], fold(s_near[g], jnp.maximum)), axis=1, keepdims=True) for g in range(2)]

    def pass_b(t, carry):
        v = kv_ref[0, pl.ds(pl.multiple_of(PAD_ROWS + t * FAR_TILE, LANES), FAR_TILE), 128:256]
        out = []
        for g in range(2):
            e = jnp.exp(s_ref[g, t] - m[g])
            out.append((carry[g][0] + fold(e, jnp.add), carry[g][1] + _dot(e.astype(BF16), v)))
        return tuple(out)

    zero = jnp.zeros((rows, LANES), F32)
    far = lax.fori_loop(0, ntiles, pass_b, ((zero, zero),) * 2)
    vn = kv_ref[0, pl.ds(r0n, NEAR_KEYS), 128:256]
    r0w = pl.multiple_of(PAD_ROWS + (i - 9) * 64, 64)
    kw = keys_aug(r0w, WIN_KEYS, 256)
    vw = kv_ref[0, pl.ds(r0w, WIN_KEYS), 384:512]

    outs = []
    for g in range(2):
        e = jnp.exp(s_near[g] - m[g])
        l = jnp.sum(far[g][0] + fold(e, jnp.add), axis=1, keepdims=True)
        o_s = (far[g][1] + _dot(e.astype(BF16), vn)) / l

        s = _nt(lhs_win[g], kw) + wb_ref[g]
        e = jnp.exp(s - jnp.max(fold(s, jnp.maximum), axis=1, keepdims=True))
        o_w = _dot(e.astype(BF16), vw) / jnp.sum(fold(e, jnp.add), axis=1, keepdims=True)

        per_head = []
        for h in range(NSA_GROUP):
            c0 = (g * NSA_GROUP + h) * 3
            r = slice(h * 64, (h + 1) * 64)
            per_head.append(gts[:, c0:c0 + 1] * o_c[g][r] + gts[:, c0 + 1:c0 + 2] * o_s[r]
                            + gts[:, c0 + 2:c0 + 3] * o_w[r])
        outs.append(per_head)
    low = lax.broadcasted_iota(jnp.int32, (64, LANES), 1) < 64
    for h in range(NSA_GROUP):
        o_ref[0, :, h * LANES:(h + 1) * LANES] = jnp.where(low, outs[0][h], outs[1][h]).astype(BF16)


def _nsa_prompt_attn(qn, gates, cmp, kvatt, oh, wb, nb, ct, amat):
    b, l, _ = qn.shape
    nc = l // 64
    full = lambda a: pl.BlockSpec(a.shape, lambda bi, i: (0,) * a.ndim)
    return pl.pallas_call(
        _nsa_prompt_kernel,
        grid=(b, nc),
        in_specs=[pl.BlockSpec((1, 64, 768), lambda bi, i: (bi, i, 0)),
                  pl.BlockSpec((1, 64, 128), lambda bi, i: (bi, i, 0)),
                  pl.BlockSpec((1, nc, 256), lambda bi, i: (bi, 0, 0)),
                  pl.BlockSpec((1, PAD_ROWS + l, 512), lambda bi, i: (bi, 0, 0)),
                  full(oh), full(wb), full(nb), full(ct), full(amat)],
        out_specs=pl.BlockSpec((1, 64, 768), lambda bi, i: (bi, i, 0)),
        out_shape=jax.ShapeDtypeStruct((b, l, 768), BF16),
        scratch_shapes=[pltpu.VMEM((2, (nc + 4) // (FAR_TILE // 64), NSA_GROUP * 64, FAR_TILE), F32)],
        compiler_params=_cparams("parallel", "arbitrary"),
    )(qn, gates, cmp, kvatt, oh, wb, nb, ct, amat)


def _mem_attn_kernel(q_ref, kv_ref, o_ref):
    q = q_ref[0].astype(F32)
    k = kv_ref[0, :, 0:256].astype(BF16)
    v = kv_ref[0, :, 256:512].astype(BF16)
    head = lax.broadcasted_iota(jnp.int32, q.shape, 1) // HEAD_DIM
    out = jnp.zeros(q.shape, F32)
    for h in range(XATTN_HEADS):
        qh = jnp.where(head == h, q, 0.0).astype(BF16)
        s = _nt(qh, k)
        e = jnp.exp(s - jnp.max(s, axis=1, keepdims=True))
        p = e / jnp.sum(e, axis=1, keepdims=True)
        out = out + jnp.where(head == h, _dot(p.astype(BF16), v), 0.0)
    o_ref[0] = out.astype(BF16)


def _mem_attn(xqn, mkv, tq):
    b, l, _ = xqn.shape
    nm = mkv.shape[1]
    return pl.pallas_call(
        _mem_attn_kernel,
        grid=(b, l // tq),
        in_specs=[pl.BlockSpec((1, tq, 256), lambda bi, i: (bi, i, 0)),
                  pl.BlockSpec((1, nm, 512), lambda bi, i: (bi, 0, 0))],
        out_specs=pl.BlockSpec((1, tq, 256), lambda bi, i: (bi, i, 0)),
        out_shape=jax.ShapeDtypeStruct((b, l, 256), BF16),
        compiler_params=_cparams("parallel", "parallel"),
    )(xqn, mkv)


def _pool_kernel(u_ref, prev_ref, w_ref, scale_ref, o_ref, carry_ref, *, pos0):
    i = pl.program_id(1)
    tp = u_ref.shape[1]

    @pl.when(i == 0)
    def _():
        carry_ref[...] = prev_ref[0]

    u = u_ref[0]
    ext = jnp.concatenate([carry_ref[...], u], axis=0)
    carry_ref[...] = ext[tp:tp + 16]
    s2 = ext + pltpu.roll(ext, 1, axis=0)
    s4 = s2 + pltpu.roll(s2, 2, axis=0)
    s8 = s4 + pltpu.roll(s4, 4, axis=0)
    s16 = s8 + pltpu.roll(s8, 8, axis=0)
    lane = lax.broadcasted_iota(jnp.int32, (tp, POOL_WIDTH), 1)
    pos1 = (pos0 + 1 + i * tp + lax.broadcasted_iota(jnp.int32, (tp, POOL_WIDTH), 0)).astype(F32)
    grp = lane // POOL_GROUP_DIM
    ssum = jnp.where(grp == 0, s2[16:], jnp.where(grp == 1, s4[16:], jnp.where(grp == 2, s8[16:], s16[16:])))
    win = jnp.where(grp == 0, 2.0, jnp.where(grp == 1, 4.0, jnp.where(grp == 2, 8.0, 16.0)))
    pooled = ssum / jnp.minimum(win, pos1) - u
    o_ref[0] = (_dot(pooled.astype(BF16), w_ref[...]) * scale_ref[...]).astype(BF16)


def _pool_mix(u, prev16, wbd, scale, tp, pos0):
    b, l, _ = u.shape
    full = lambda a: pl.BlockSpec(a.shape, lambda bi, i: (0,) * a.ndim)
    return pl.pallas_call(
        functools.partial(_pool_kernel, pos0=pos0),
        grid=(b, l // tp),
        in_specs=[pl.BlockSpec((1, tp, POOL_WIDTH), lambda bi, i: (bi, i, 0)),
                  pl.BlockSpec((1, 16, POOL_WIDTH), lambda bi, i: (bi, 0, 0)),
                  full(wbd), full(scale)],
        out_specs=pl.BlockSpec((1, tp, POOL_WIDTH), lambda bi, i: (bi, i, 0)),
        out_shape=jax.ShapeDtypeStruct((b, l, POOL_WIDTH), BF16),
        scratch_shapes=[pltpu.VMEM((16, POOL_WIDTH), F32)],
        compiler_params=_cparams("parallel", "arbitrary"),
    )(u, prev16, wbd, scale)


def _outproj_kernel(y_ref, mix_ref, att_ref, wm_ref, wa_ref, g_ref, o_ref, h_ref):
    y = y_ref[...] + _dot(mix_ref[...], wm_ref[...]) + _dot(att_ref[...], wa_ref[...])
    o_ref[...] = y
    h_ref[...] = _rms_rows(y, g_ref[...]).astype(BF16)


def _outproj(y, mix, att, wm, wa, g_ffn, tm):
    t = y.shape[0]
    full = lambda a: pl.BlockSpec(a.shape, lambda i: (0,) * a.ndim)
    row = lambda n: pl.BlockSpec((tm, n), lambda i: (i, 0))
    return pl.pallas_call(
        _outproj_kernel,
        grid=(t // tm,),
        in_specs=[row(D_MODEL), row(768), row(256), full(wm), full(wa), full(g_ffn)],
        out_specs=[row(D_MODEL), row(D_MODEL)],
        out_shape=[jax.ShapeDtypeStruct((t, D_MODEL), F32), jax.ShapeDtypeStruct((t, D_MODEL), BF16)],
        compiler_params=_cparams("parallel"),
    )(y, mix, att, wm, wa, g_ffn)


def _ffn_half(h, w_ref, cw_ref, cb_ref, ext_ref, prev8, fix):
    tm = h.shape[0]
    u = _dot(h, w_ref[...])
    ext_ref[0:8, :] = prev8
    ext_ref[8:8 + tm, :] = u
    s1 = ext_ref[7:7 + tm, :]
    s2 = ext_ref[6:6 + tm, :]
    if fix is not None:
        rowm, p1, p2 = fix
        s1 = jnp.where(rowm >= 1, s1, p1)
        s2 = jnp.where(rowm >= 2, s2, p2)
    cw = cw_ref[...]
    return u, cb_ref[...] + cw[0:1] * s2 + cw[1:2] * s1 + cw[2:3] * u


def _ffn_prompt_kernel(y_ref, h_ref, wa_ref, wv_ref, cwa_ref, cwv_ref, cba_ref, cbv_ref, wd_ref,
                       o_ref, sa_ref, sv_ref, acc_ref, ext_ref, carry_ref, *, tiles_per_seq):
    j = pl.program_id(0)
    c = pl.program_id(1)
    tm = y_ref.shape[0]

    @pl.when((j % tiles_per_seq) == 0)
    def _():
        carry_ref[c] = jnp.zeros(carry_ref.shape[1:], F32)

    h = h_ref[...]
    ua, ya = _ffn_half(h, wa_ref, cwa_ref, cba_ref, ext_ref, carry_ref[c, 0], None)
    carry_ref[c, 0] = ua[tm - 8:tm]
    sa_ref[0] = ua[tm - 8:tm]
    uv, yv = _ffn_half(h, wv_ref, cwv_ref, cbv_ref, ext_ref, carry_ref[c, 1], None)
    carry_ref[c, 1] = uv[tm - 8:tm]
    sv_ref[0] = uv[tm - 8:tm]
    d = _dot((jax.nn.silu(ya) * yv).astype(BF16), wd_ref[...])

    @pl.when(c == 0)
    def _():
        acc_ref[...] = d

    @pl.when(c == pl.num_programs(1) - 1)
    def _():
        o_ref[...] = y_ref[...] + acc_ref[...] + d


def _ffn_prompt(y, h, w_up, conv_w, conv_b, w_down, tm, seq_len):
    t = y.shape[0]
    nseq = t // seq_len
    tps = seq_len // tm
    fc = FF_CHUNK
    ncf = D_FF // fc
    full = lambda a: pl.BlockSpec(a.shape, lambda j, c: (0,) * a.ndim)
    outs = pl.pallas_call(
        functools.partial(_ffn_prompt_kernel, tiles_per_seq=tps),
        grid=(t // tm, ncf),
        in_specs=[pl.BlockSpec((tm, D_MODEL), lambda j, c: (j, 0)), pl.BlockSpec((tm, D_MODEL), lambda j, c: (j, 0)),
                  pl.BlockSpec((D_MODEL, fc), lambda j, c: (0, c)),
                  pl.BlockSpec((D_MODEL, fc), lambda j, c: (0, ncf + c)),
                  pl.BlockSpec((3, fc), lambda j, c: (0, c)),
                  pl.BlockSpec((3, fc), lambda j, c: (0, ncf + c)),
                  pl.BlockSpec((1, fc), lambda j, c: (0, c)),
                  pl.BlockSpec((1, fc), lambda j, c: (0, ncf + c)),
                  pl.BlockSpec((fc, D_MODEL), lambda j, c: (c, 0))],
        out_specs=[pl.BlockSpec((tm, D_MODEL), lambda j, c: (j, 0)),
                   pl.BlockSpec((1, 8, fc), lambda j, c: (j, 0, c)),
                   pl.BlockSpec((1, 8, fc), lambda j, c: (j, 0, c))],
        out_shape=[jax.ShapeDtypeStruct((t, D_MODEL), F32),
                   jax.ShapeDtypeStruct((t // tm, 8, D_FF), F32),
                   jax.ShapeDtypeStruct((t // tm, 8, D_FF), F32)],
        scratch_shapes=[pltpu.VMEM((tm, D_MODEL), F32),
                        pltpu.VMEM((tm + 8, fc), F32), pltpu.VMEM((ncf, 2, 8, fc), F32)],
        compiler_params=_cparams("arbitrary", "arbitrary"),
    )(y, h, w_up, w_up, conv_w, conv_w, conv_b, conv_b, w_down)
    y_out, sa, sv = outs
    last = lambda s: s.reshape(nseq, tps, 8, D_FF)[:, tps - 1, 6:8]
    state = jnp.concatenate([last(sa), last(sv)], axis=-1)
    return y_out, state


def _ffn_sample_kernel(y_ref, g_ref, wa_ref, wv_ref, cwa_ref, cwv_ref, cba_ref, cbv_ref, wd_ref,
                       p1a_ref, p1v_ref, p2a_ref, p2v_ref, o_ref, ua_ref, uv_ref, acc_ref, ext_ref):
    c = pl.program_id(0)
    h = _rms_rows(y_ref[...], g_ref[...]).astype(BF16)
    tm = h.shape[0]
    rowm = lax.broadcasted_iota(jnp.int32, (tm, wa_ref.shape[1]), 0) % 8
    zero8 = jnp.zeros((8, wa_ref.shape[1]), F32)
    ua, ya = _ffn_half(h, wa_ref, cwa_ref, cba_ref, ext_ref, zero8, (rowm, p1a_ref[...], p2a_ref[...]))
    uv, yv = _ffn_half(h, wv_ref, cwv_ref, cbv_ref, ext_ref, zero8, (rowm, p1v_ref[...], p2v_ref[...]))
    ua_ref[...] = ua
    uv_ref[...] = uv
    d = _dot((jax.nn.silu(ya) * yv).astype(BF16), wd_ref[...])

    @pl.when(c == 0)
    def _():
        acc_ref[...] = d

    @pl.when(c == pl.num_programs(0) - 1)
    def _():
        o_ref[...] = y_ref[...] + acc_ref[...] + d


def _ffn_sample(y, g, w_up, conv_w, conv_b, w_down, state):
    t = y.shape[0]
    nseq = t // 8
    fc = FF_CHUNK
    ncf = D_FF // fc
    z = jnp.zeros((nseq, 1, 2 * D_FF), F32)
    p1 = jnp.concatenate([state[:, 1:2]] + [z] * 7, axis=1).reshape(t, 2 * D_FF)
    p2 = jnp.concatenate([state[:, 0:1], state[:, 1:2]] + [z] * 6, axis=1).reshape(t, 2 * D_FF)
    full = lambda a: pl.BlockSpec(a.shape, lambda c: (0,) * a.ndim)
    ca = lambda r: pl.BlockSpec((r, fc), lambda c: (0, c))
    cv = lambda r: pl.BlockSpec((r, fc), lambda c: (0, ncf + c))
    y_out, ua, uv = pl.pallas_call(
        _ffn_sample_kernel,
        grid=(ncf,),
        in_specs=[full(y), full(g), ca(D_MODEL), cv(D_MODEL), ca(3), cv(3), ca(1), cv(1),
                  pl.BlockSpec((fc, D_MODEL), lambda c: (c, 0)), ca(t), cv(t), ca(t), cv(t)],
        out_specs=[pl.BlockSpec((t, D_MODEL), lambda c: (0, 0)), ca(t), ca(t)],
        out_shape=[jax.ShapeDtypeStruct((t, D_MODEL), F32), jax.ShapeDtypeStruct((t, D_FF), F32),
                   jax.ShapeDtypeStruct((t, D_FF), F32)],
        scratch_shapes=[pltpu.VMEM((t, D_MODEL), F32), pltpu.VMEM((t + 8, fc), F32)],
        compiler_params=_cparams("arbitrary"),
    )(y, g, w_up, w_up, conv_w, conv_w, conv_b, conv_b, w_down, p1, p1, p2, p2)
    u = jnp.concatenate([ua, uv], axis=-1).reshape(nseq, 8, 2 * D_FF)
    return y_out, u[:, 6:8]


KEY_CHUNK = 1024
NEW_PAD = 16


def _nsa_sample_kernel(pt_ref, qn_ref, gates_ref, rows_ref, winc_ref, winn_ref, scb_ref, sbt_ref, wbt_ref,
                       rmat_ref, eloc_ref, cmp_hbm, cache_hbm, o_ref, kvbuf, kcv, s_ref, m3_ref, sem, sem2):
    b = pl.program_id(0)
    npages = pt_ref.shape[1]
    ncs = 2 * npages
    nkeys = npages * PAGE_SIZE
    nch = nkeys // KEY_CHUNK
    ppc = KEY_CHUNK // PAGE_SIZE
    wbuf = winc_ref.shape[-1]

    def page_copy(p):
        return pltpu.make_async_copy(cache_hbm.at[pt_ref[b, p], pl.ds(2, 2)], kvbuf.at[p], sem)

    def cmp_copy(p):
        return pltpu.make_async_copy(cmp_hbm.at[pt_ref[b, p]], kcv.at[p], sem2)

    def start_all(p, c):
        cmp_copy(p).start()
        page_copy(p).start()
        return c

    lax.fori_loop(0, npages, start_all, 0)

    q = qn_ref[0].astype(F32)
    gts = gates_ref[0]
    lane8 = lax.broadcasted_iota(jnp.int32, (8, LANES), 1)
    half = [lane8 < 64, lane8 >= 64]
    pieces = [jnp.where(half[g], q[:, h * LANES:(h + 1) * LANES], 0.0) for g in range(2) for h in range(NSA_GROUP)]
    qrows = jnp.concatenate(pieces + [jnp.zeros((32, LANES), F32)], axis=0).astype(BF16)

    def wait_cmp(p, c):
        cmp_copy(p).wait()
        return c

    lax.fori_loop(0, npages, wait_cmp, 0)

    kc = jnp.concatenate([kcv[:, 0, 0:128], kcv[:, 1, 0:128]], axis=0).astype(BF16)
    vc = jnp.concatenate([kcv[:, 0, 128:256], kcv[:, 1, 128:256]], axis=0).astype(BF16)
    lc = _nt(kc, qrows) + scb_ref[...]
    e = jnp.exp(lc - jnp.max(lc, axis=0, keepdims=True))
    p_c = e / jnp.sum(e, axis=0, keepdims=True)
    o_c = _tn(p_c.astype(BF16), vc)
    hi = p_c.astype(BF16)
    lo = (p_c - hi.astype(F32)).astype(BF16)
    imp = _dot(hi, rmat_ref[...]) + _dot(lo, rmat_ref[...])

    def blk_of(r):
        return jnp.where(r < npages, 2 * r, jnp.where(r < ncs, 2 * (r - npages) + 1, r))

    blk = blk_of(lax.broadcasted_iota(jnp.int32, (ncs + 8, LANES), 0))
    impx = jnp.concatenate([imp, jnp.zeros((8, LANES), F32)], axis=0)
    forced = (blk == 0) | (blk == ncs) | (blk == ncs - 1)
    score = jnp.where(blk <= ncs, jnp.where(forced, FORCED_SCORE, impx), -1.0)
    blkf = blk.astype(F32)
    work, picked = score, jnp.zeros((ncs + 8, LANES), F32)
    for _ in range(SLC_TOPN):
        top = jnp.max(work, axis=0, keepdims=True)
        first = jnp.min(jnp.where(work == top, blkf, float(2 * ncs + 16)), axis=0, keepdims=True)
        hit = blkf == first
        picked = jnp.where(hit, 1.0, picked)
        work = jnp.where(hit, -2.0, work)
    madd = jnp.where((picked > 0.0) & (score >= 0.0), 0.0, -BIG)
    eye = jnp.where(lax.broadcasted_iota(jnp.int32, (LANES, LANES), 0)
                    == lax.broadcasted_iota(jnp.int32, (LANES, LANES), 1), 1.0, 0.0).astype(BF16)
    hb = ppc
    for t in range(nch):
        mt = jnp.concatenate([madd[t * hb:(t + 1) * hb], madd[npages + t * hb:npages + (t + 1) * hb],
                              jnp.zeros((LANES - 2 * hb, LANES), F32)], axis=0).astype(BF16)
        m3_ref[t] = _nt(eye, mt).astype(BF16)

    def wait_page(p, c):
        page_copy(p).wait()
        return c

    lax.fori_loop(0, npages, wait_page, 0)

    eloc = eloc_ref[...]
    pad8 = jnp.zeros((NEW_PAD - 8, LANES), F32)
    knew = jnp.concatenate([rows_ref[0, :, 256:384], pad8], axis=0).astype(BF16)
    vnew = jnp.concatenate([rows_ref[0, :, 384:512], pad8], axis=0).astype(BF16)
    kwn = jnp.concatenate([winn_ref[0, :, 0:128], pad8], axis=0).astype(BF16)
    vwn = jnp.concatenate([winn_ref[0, :, 128:256], pad8], axis=0).astype(BF16)
    kwt = winc_ref[0, 0].reshape(LANES, wbuf).astype(BF16)
    vwt = winc_ref[0, 1].reshape(LANES, wbuf).astype(BF16)
    near_pad = jnp.concatenate([jnp.zeros((LANES, KEY_CHUNK - PAGE_SIZE), F32), sbt_ref[:, 0:PAGE_SIZE]], axis=1)
    low = lax.broadcasted_iota(jnp.int32, (8, LANES), 1) < 64

    def page_rows(t, comp):
        tiles = [kvbuf[t * ppc + pp, comp].reshape(LANES, PAGE_SIZE) for pp in range(ppc)]
        return jnp.concatenate(tiles, axis=1).astype(BF16)

    def s_body(t, m):
        lhs = jnp.concatenate([m3_ref[t], qrows], axis=1)
        s = _dot(lhs, jnp.concatenate([eloc, page_rows(t, 0)], axis=0))
        s = s + jnp.where(t == nch - 1, 1.0, 0.0) * near_pad
        s_ref[t] = s
        return jnp.maximum(m, jnp.max(s, axis=1, keepdims=True))

    s_new = _nt(qrows, knew) + sbt_ref[:, PAGE_SIZE:PAGE_SIZE + NEW_PAD]
    m = lax.fori_loop(0, nch, s_body, jnp.max(s_new, axis=1, keepdims=True))
    e_new = jnp.exp(s_new - m)

    def pv_body(t, carry):
        l, acc = carry
        et = jnp.exp(s_ref[t] - m)
        return l + jnp.sum(et, axis=1, keepdims=True), acc + _nt(et.astype(BF16), page_rows(t, 1))

    l, acc = lax.fori_loop(0, nch, pv_body, (jnp.sum(e_new, axis=1, keepdims=True),
                                             _dot(e_new.astype(BF16), vnew)))
    o_s = acc / l

    sw = _dot(qrows, kwt) + wbt_ref[:, 0:wbuf]
    swn = _nt(qrows, kwn) + wbt_ref[:, wbuf:wbuf + NEW_PAD]
    mw = jnp.maximum(jnp.max(sw, axis=1, keepdims=True), jnp.max(swn, axis=1, keepdims=True))
    ew = jnp.exp(sw - mw)
    ewn = jnp.exp(swn - mw)
    o_w = (_nt(ew.astype(BF16), vwt) + _dot(ewn.astype(BF16), vwn)) / (
        jnp.sum(ew, axis=1, keepdims=True) + jnp.sum(ewn, axis=1, keepdims=True))

    def gate_col(br):
        cols = [gts[:, hh * 3 + br:hh * 3 + br + 1] for hh in range(NSA_HEADS)]
        return jnp.concatenate(cols + [jnp.zeros((32, 1), F32)], axis=0)

    o = gate_col(0) * o_c + gate_col(1) * o_s + gate_col(2) * o_w
    for h in range(NSA_GROUP):
        r0, r1 = h * 8, (NSA_GROUP + h) * 8
        o_ref[0, :, h * LANES:(h + 1) * LANES] = jnp.where(low, o[r0:r0 + 8], o[r1:r1 + 8]).astype(BF16)


def _nsa_sample_attn(page_table, qn, gates, rows, winc, winn, scb, sbt, wbt, rmat, eloc, cmp_phys, cache):
    b, npages = page_table.shape
    ncs = 2 * npages
    nkeys = npages * PAGE_SIZE
    assert nkeys % KEY_CHUNK == 0
    nch = nkeys // KEY_CHUNK
    full = lambda a: pl.BlockSpec(a.shape, lambda bi, pt: (0,) * a.ndim)
    per_b = lambda a: pl.BlockSpec((1,) + a.shape[1:], lambda bi, pt: (bi,) + (0,) * (a.ndim - 1))
    grid_spec = pltpu.PrefetchScalarGridSpec(
        num_scalar_prefetch=1,
        grid=(b,),
        in_specs=[per_b(qn), per_b(gates), per_b(rows), per_b(winc), per_b(winn),
                  full(scb), full(sbt), full(wbt), full(rmat), full(eloc),
                  pl.BlockSpec(memory_space=pl.ANY), pl.BlockSpec(memory_space=pl.ANY)],
        out_specs=pl.BlockSpec((1, 8, 768), lambda bi, pt: (bi, 0, 0)),
        scratch_shapes=[pltpu.VMEM((npages, 2, 2, 64, PAGE_SIZE), F32), pltpu.VMEM((npages, 2, 256), F32),
                        pltpu.VMEM((nch, LANES, KEY_CHUNK), F32), pltpu.VMEM((nch, LANES, LANES), BF16),
                        pltpu.SemaphoreType.DMA(()), pltpu.SemaphoreType.DMA(())],
    )
    return pl.pallas_call(
        _nsa_sample_kernel,
        grid_spec=grid_spec,
        out_shape=jax.ShapeDtypeStruct((b, 8, 768), BF16),
        compiler_params=_cparams("arbitrary"),
    )(page_table, qn, gates, rows, winc, winn, scb, sbt, wbt, rmat, eloc, cmp_phys, cache)


def _block_diag2(w):
    z = jnp.zeros_like(w)
    return jnp.concatenate([jnp.concatenate([w, z], axis=-1), jnp.concatenate([z, w], axis=-1)], axis=-2)


def _q_slot_perm():
    return np.concatenate([np.r_[h * 64:(h + 1) * 64, (NSA_GROUP + h) * 64:(NSA_GROUP + h + 1) * 64]
                           for h in range(NSA_GROUP)])


def _sample_codes(ncs, wbuf):
    ql = np.arange(8)[None, :]
    c = np.arange(ncs)[:, None]
    scb = _bucket_np((ncs - c) * CMP_BLOCK + ql - (CMP_BLOCK - 1))
    kap = np.arange(PAGE_SIZE + NEW_PAD)[:, None]
    d = np.where(kap < PAGE_SIZE, PAGE_SIZE + ql - kap, ql - (kap - PAGE_SIZE))
    sbt = np.where((d >= 0) & (kap < PAGE_SIZE + 8), _bucket_np(d), -1)
    j = np.arange(wbuf + NEW_PAD)[:, None]
    d = wbuf + ql - j
    wbt = np.where((d >= 0) & (d <= WINDOW) & (j < wbuf + 8), _bucket_np(d), -1)
    return np.concatenate([scb, sbt, wbt], axis=0).astype(np.int32)


def kernel(x_prompt, x_sample, cache_nsa_kv, cache_nsa_win, cache_mem_kv, state_pool, state_ffn_conv,
           page_table, mem_prompt, rel_bias, g_mix, g_mem, g_ffn, w_in_nsa, q_gain_nsa, k_gain_nsa,
           cmp_pos, cmp_w1, cmp_b1, cmp_w2, cmp_b2, w_in_pool, w_pool_grp, pool_scale, w_mem_kv,
           xq_gain, xk_gain, w_out, w_up, conv_w, conv_b, w_down):
    bp, lp, _ = x_prompt.shape
    bs, ls, _ = x_sample.shape
    assert ls == 8 and lp % 512 == 0 and lp // 64 <= 64
    npages = page_table.shape[1]
    past_len = npages * PAGE_SIZE
    n_phys = cache_nsa_kv.shape[1]
    wbuf = cache_nsa_win.shape[2]
    nm = mem_prompt.shape[1]
    tp_, ts_ = bp * lp, bs * ls
    nc = lp // 64
    ncs = 2 * npages
    scale = HEAD_DIM ** -0.5
    tile2 = lambda v: jnp.tile(v, 2)[None, :]

    a = np.arange(LANES)
    seg = jnp.asarray(a[:, None] // 64 == a[None, :] // 64, BF16)
    perm = _q_slot_perm()

    wb = _bias_table(rel_bias, _window_codes()).reshape(2, NSA_GROUP * 64, WIN_KEYS)
    nbias = jnp.concatenate([jnp.full((2, NSA_GROUP * 64, 64), -BIG, F32), wb[:, :, WIN_KEYS - 192:]], axis=2)
    ct = _bias_table(rel_bias, _cmp_codes()).reshape(2, NSA_GROUP, 8, 64)
    ct = ct.transpose(0, 2, 1, 3).reshape(2, 8, NSA_GROUP * 64)
    amat = np.zeros((2, NSA_GROUP * 64, LANES), np.float32)
    for g in range(2):
        for h in range(NSA_GROUP):
            amat[g, h * 64 + np.arange(64), g * 64 + np.arange(64)] = 1.0
    amat = jnp.asarray(amat, BF16)
    r = np.arange(PAD_ROWS + lp)
    oh_np = np.zeros((PAD_ROWS + lp, LANES), np.float32)
    oh_np[r, np.where(r < PAD_ROWS, 64, (r - PAD_ROWS) // 64)] = 1.0
    oh = jnp.asarray(oh_np, BF16)
    stab = _bias_table(rel_bias, _sample_codes(ncs, wbuf))
    stab = jnp.pad(stab.transpose(1, 0, 2).reshape(-1, NSA_HEADS * 8), ((0, 0), (0, LANES - NSA_HEADS * 8)))
    ns = PAGE_SIZE + NEW_PAD
    scb, sbt, wbt = stab[:ncs], stab[ncs:ncs + ns].T, stab[ncs + ns:].T
    scb = jnp.concatenate([scb[0::2], scb[1::2]], axis=0)
    ppc = KEY_CHUNK // PAGE_SIZE
    eloc_np = np.zeros((LANES, KEY_CHUNK), np.float32)
    for pp in range(ppc):
        eloc_np[pp, pp * PAGE_SIZE:pp * PAGE_SIZE + 64] = 1.0
        eloc_np[ppc + pp, pp * PAGE_SIZE + 64:(pp + 1) * PAGE_SIZE] = 1.0
    eloc = jnp.asarray(eloc_np, BF16)
    lam = np.arange(LANES)
    rm = (lam[:, None] < 96) & (lam[None, :] < 96) & (lam[:, None] // 48 == lam[None, :] // 48) \
        & (lam[:, None] % 8 == lam[None, :] % 8)
    rmat = jnp.asarray(rm, BF16)

    yp = x_prompt.reshape(tp_, D_MODEL)
    ys = x_sample.reshape(ts_, D_MODEL)
    tm = 512
    outs = {k: [] for k in ("kv_p", "kv_s", "win_p", "win_s", "mem_p", "pool_p", "pool_s", "conv_p", "conv_s")}
    depth = g_mix.shape[0]
    for i in range(depth):
        j = i // 2
        wo = w_out[i]
        xg = tile2(xq_gain[i]) * scale
        mkv_p = _memkv(mem_prompt.reshape(bp * nm, D_MODEL), g_mem[i][None], w_mem_kv[i].astype(BF16), seg,
                       tile2(xk_gain[i]), min(tm, bp * nm)).reshape(bp, nm, 2 * XATTN_WIDTH)
        outs["mem_p"].append(mkv_p.reshape(bp, nm, 2, XATTN_HEADS, HEAD_DIM))
        mkv_s = cache_mem_kv[i].reshape(bs, nm, 2 * XATTN_WIDTH)
        if i % 2 == 0:
            w = w_in_nsa[j]
            w = jnp.concatenate([w[:, perm], w[:, 768:1536], w[:, 1572:1828], w[:, 1536:1572],
                                 jnp.zeros((D_MODEL, NSA_IN_COLS - 1828), F32)], axis=1).astype(BF16)
            qg = tile2(q_gain_nsa[j]) * scale
            kg = tile2(k_gain_nsa[j])
            cmp_args = (jnp.tile(cmp_pos[j], (1, 1, 2))[:, :, None, :], _block_diag2(cmp_w1[j]).astype(BF16),
                        jnp.tile(cmp_b1[j], (1, 2))[:, None, :], _block_diag2(cmp_w2[j]).astype(BF16),
                        jnp.tile(cmp_b2[j], (1, 2))[:, None, :], kg, seg)
            wo_mix = wo[:768][perm].astype(BF16)
            w2 = cmp_w2[j]
            w2sel = jnp.zeros((2, 2, 2, 2 * CMP_HIDDEN, LANES), F32)
            for blk in range(2):
                for g in range(2):
                    w2sel = w2sel.at[:, blk, g, blk * CMP_HIDDEN:(blk + 1) * CMP_HIDDEN,
                                     g * HEAD_DIM:(g + 1) * HEAD_DIM].set(w2)
            cc_args = (jnp.tile(cmp_pos[j].transpose(0, 2, 1), (1, 1, 2))[:, :, None, :],
                       _block_diag2(cmp_w1[j].transpose(0, 2, 1, 3)).astype(BF16).reshape(2, 32, 2 * LANES, 256),
                       cmp_args[2], w2sel.astype(BF16), cmp_args[4], kg, seg, 512)
            wt = w_in_nsa[j][:, 768:1536].T.astype(BF16)
            qn, rows_t, win_t, katt, xqn_p, gates = _nsa_inproj_t(
                yp, g_mix[i][None], w, wt, seg, qg, kg, jnp.tile(k_gain_nsa[j], 2)[:, None], xg, tm, lp)
            pages_t = rows_t[:, :256].reshape(bp, 2, 2, HEAD_DIM, lp // PAGE_SIZE, PAGE_SIZE)
            pages_t = pages_t.transpose(0, 4, 1, 2, 3, 5).reshape(bp * lp // PAGE_SIZE, 2, 2, HEAD_DIM, PAGE_SIZE)
            cmp_p = _compress_cache(pages_t, *cc_args).reshape(bp, nc, 256)
            kvatt = jnp.pad(katt.reshape(bp, lp, 512), ((0, 0), (PAD_ROWS, 0), (0, 0)))
            mix_p = _nsa_prompt_attn(qn.reshape(bp, lp, 768), gates.reshape(bp, lp, 128), cmp_p, kvatt,
                                     oh, wb, nbias, ct, amat).reshape(tp_, 768)
            outs["kv_p"].append(rows_t.reshape(bp, 4, NSA_KV_HEADS, HEAD_DIM, lp).transpose(0, 4, 1, 2, 3))
            wlen = min(WINDOW, lp)
            outs["win_p"].append(win_t[:, :, lp - wlen:].reshape(bp, 2, NSA_KV_HEADS, HEAD_DIM, wlen)
                                 .transpose(0, 4, 1, 2, 3))
            qn, rows, win, _, xqn_s, gates = _nsa_inproj(ys, g_mix[i][None], w, seg, qg, kg, xg, ts_)
            cache_t = cache_nsa_kv[j].transpose(0, 2, 3, 4, 1)
            cmp_phys = _compress_cache(cache_t, *cc_args)
            winc = cache_nsa_win[j].transpose(0, 2, 3, 4, 1)
            mix_s = _nsa_sample_attn(page_table, qn.reshape(bs, ls, 768), gates.reshape(bs, ls, 128),
                                     rows.reshape(bs, ls, 512), winc, win.reshape(bs, ls, 256),
                                     scb, sbt, wbt, rmat, eloc, cmp_phys.reshape(n_phys, 2, 256),
                                     cache_t).reshape(ts_, 768)
            outs["kv_s"].append(rows.reshape(bs, ls, 4, NSA_KV_HEADS, HEAD_DIM))
            wall = jnp.concatenate([cache_nsa_win[j], win.reshape(bs, ls, 2, NSA_KV_HEADS, HEAD_DIM)], axis=1)
            outs["win_s"].append(wall[:, -wbuf:])
        else:
            w = w_in_pool[j].astype(BF16)
            wbd = jnp.zeros((POOL_WIDTH, POOL_WIDTH), F32)
            for gi in range(len(POOL_WINDOWS)):
                sl = slice(gi * POOL_GROUP_DIM, (gi + 1) * POOL_GROUP_DIM)
                wbd = wbd.at[sl, sl].set(w_pool_grp[j, gi])
            wbd = wbd.astype(BF16)
            psc = pool_scale[j][None]
            wo_mix = wo[:768].astype(BF16)
            u_p, xqn_p = _pool_inproj(yp, g_mix[i][None], w, seg, xg, tm)
            u_p3 = u_p.reshape(bp, lp, POOL_WIDTH)
            mix_p = _pool_mix(u_p3, jnp.zeros((bp, 16, POOL_WIDTH), F32), wbd, psc, tm, 0).reshape(tp_, 768)
            outs["pool_p"].append(u_p3[:, -POOL_STATE:])
            u_s, xqn_s = _pool_inproj(ys, g_mix[i][None], w, seg, xg, ts_)
            u_s3 = u_s.reshape(bs, ls, POOL_WIDTH)
            prev16 = jnp.concatenate([jnp.zeros((bs, 1, POOL_WIDTH), F32), state_pool[j]], axis=1)
            mix_s = _pool_mix(u_s3, prev16, wbd, psc, ls, past_len).reshape(ts_, 768)
            outs["pool_s"].append(jnp.concatenate([state_pool[j], u_s3], axis=1)[:, -POOL_STATE:])
        wo_att = wo[768:].astype(BF16)
        att_p = _mem_attn(xqn_p.reshape(bp, lp, 256), mkv_p, tm).reshape(tp_, 256)
        att_s = _mem_attn(xqn_s.reshape(bs, ls, 256), mkv_s, ls).reshape(ts_, 256)
        yp, hp = _outproj(yp, mix_p, att_p, wo_mix, wo_att, g_ffn[i][None], tm)
        ys, _ = _outproj(ys, mix_s, att_s, wo_mix, wo_att, g_ffn[i][None], ts_)
        wu, wd = w_up[i].astype(BF16), w_down[i].astype(BF16)
        yp, c_p = _ffn_prompt(yp, hp, wu, conv_w[i], conv_b[i][None], wd, tm, lp)
        ys, c_s = _ffn_sample(ys, g_ffn[i][None], wu, conv_w[i], conv_b[i][None], wd, state_ffn_conv[i])
        outs["conv_p"].append(c_p)
        outs["conv_s"].append(c_s)
    st = lambda k: jnp.stack(outs[k])
    return (yp.reshape(bp, lp, D_MODEL), ys.reshape(bs, ls, D_MODEL), st("kv_p"), st("kv_s"), st("win_p"),
            st("win_s"), st("mem_p"), st("pool_p"), st("pool_s"), st("conv_p"), st("conv_s"))
```

```python
import functools
import math

import numpy as np
import jax
import jax.numpy as jnp
from jax import lax
from jax.experimental import pallas as pl
from jax.experimental.pallas import tpu as pltpu

F32 = jnp.float32
BF16 = jnp.bfloat16

D_MODEL = 1024
PAGE_SIZE = 128
HEAD_DIM = 64
NSA_HEADS = 12
NSA_KV_HEADS = 2
NSA_GROUP = NSA_HEADS // NSA_KV_HEADS
NSA_WIDTH = NSA_HEADS * HEAD_DIM
NSA_KV_WIDTH = NSA_KV_HEADS * HEAD_DIM
CMP_BLOCK = 64
CMP_HIDDEN = 128
SLC_TOPN = 16
WINDOW = 512
FORCED_SCORE = 1e4
XATTN_HEADS = 4
XATTN_WIDTH = XATTN_HEADS * HEAD_DIM
POOL_WINDOWS = (2, 4, 8, 16)
POOL_GROUP_DIM = 192
POOL_WIDTH = len(POOL_WINDOWS) * POOL_GROUP_DIM
POOL_STATE = max(POOL_WINDOWS) - 1
D_FF = 2816
REL_BUCKETS = 32
REL_MAX_DIST = 128
EPS = 1e-6
BIG = 1e30

LANES = 128
WIN_KEYS = WINDOW + 2 * 64
NEAR_KEYS = 4 * 64
PAD_ROWS = WIN_KEYS
NSA_IN_COLS = 1920
FF_CHUNK = D_FF // 2
VMEM_LIMIT = 56 * 1024 * 1024


def _cparams(*sem):
    return pltpu.CompilerParams(dimension_semantics=sem, vmem_limit_bytes=VMEM_LIMIT)


def _nt(a, b):
    return lax.dot_general(a, b, (((1,), (1,)), ((), ())), preferred_element_type=F32)


def _tn(a, b):
    return lax.dot_general(a, b, (((0,), (0,)), ((), ())), preferred_element_type=F32)


def _dot(a, b):
    return jnp.dot(a, b, preferred_element_type=F32)


def _rms_rows(x, g):
    return x * lax.rsqrt(jnp.mean(x * x, axis=-1, keepdims=True) + EPS) * g


def _seg_rms(zc, seg, gain):
    zz = zc * zc
    hi = zz.astype(BF16)
    lo = (zz - hi.astype(F32)).astype(BF16)
    ms = (_dot(hi, seg) + _dot(lo, seg)) * (1.0 / HEAD_DIM)
    return zc * lax.rsqrt(ms + EPS) * gain


def _nsa_inproj_kernel(x_ref, g_ref, w_ref, seg_ref, qg_ref, kg_ref, xg_ref,
                       qn_ref, rows_ref, win_ref, katt_ref, xqn_ref, gates_ref):
    xn = _rms_rows(x_ref[...], g_ref[...])
    z = _dot(xn.astype(BF16), w_ref[...])
    seg = seg_ref[...]
    for h in range(6):
        qn_ref[:, h * LANES:(h + 1) * LANES] = _seg_rms(z[:, h * LANES:(h + 1) * LANES], seg, qg_ref[...]).astype(BF16)
    kv = 768
    rows_ref[:, 0:256] = z[:, kv:kv + 256]
    ks = _seg_rms(z[:, kv + 256:kv + 384], seg, kg_ref[...])
    vs = z[:, kv + 384:kv + 512]
    kw = _seg_rms(z[:, kv + 512:kv + 640], seg, kg_ref[...])
    vw = z[:, kv + 640:kv + 768]
    rows_ref[:, 256:384] = ks
    rows_ref[:, 384:512] = vs
    win_ref[:, 0:128] = kw
    win_ref[:, 128:256] = vw
    katt_ref[:, 0:128] = ks.astype(BF16)
    katt_ref[:, 128:256] = vs.astype(BF16)
    katt_ref[:, 256:384] = kw.astype(BF16)
    katt_ref[:, 384:512] = vw.astype(BF16)
    for h in range(2):
        c0 = 1536 + h * LANES
        xqn_ref[:, h * LANES:(h + 1) * LANES] = _seg_rms(z[:, c0:c0 + LANES], seg, xg_ref[...]).astype(BF16)
    gates_ref[...] = jax.nn.sigmoid(z[:, 1792:1920])


def _nsa_inproj_t_kernel(x_ref, g_ref, w_ref, wt_ref, seg_ref, qg_ref, kg_ref, kgc_ref, xg_ref,
                         qn_ref, rowst_ref, wint_ref, katt_ref, xqn_ref, gates_ref):
    xn = _rms_rows(x_ref[...], g_ref[...]).astype(BF16)
    z = _dot(xn, w_ref[...])
    zt = _nt(wt_ref[...], xn)

    def norm_t(a):
        parts = []
        for gi in range(2):
            ag = a[gi * HEAD_DIM:(gi + 1) * HEAD_DIM]
            parts.append(ag * lax.rsqrt(jnp.mean(ag * ag, axis=0, keepdims=True) + EPS))
        return jnp.concatenate(parts, axis=0) * kgc_ref[...]

    rowst_ref[0, 0:256, :] = zt[0:256]
    rowst_ref[0, 256:384, :] = norm_t(zt[256:384])
    rowst_ref[0, 384:512, :] = zt[384:512]
    wint_ref[0, 0:128, :] = norm_t(zt[512:640])
    wint_ref[0, 128:256, :] = zt[640:768]
    seg = seg_ref[...]
    for h in range(6):
        qn_ref[:, h * LANES:(h + 1) * LANES] = _seg_rms(z[:, h * LANES:(h + 1) * LANES], seg, qg_ref[...]).astype(BF16)
    kv = 768
    katt_ref[:, 0:128] = _seg_rms(z[:, kv + 256:kv + 384], seg, kg_ref[...]).astype(BF16)
    katt_ref[:, 128:256] = z[:, kv + 384:kv + 512].astype(BF16)
    katt_ref[:, 256:384] = _seg_rms(z[:, kv + 512:kv + 640], seg, kg_ref[...]).astype(BF16)
    katt_ref[:, 384:512] = z[:, kv + 640:kv + 768].astype(BF16)
    for h in range(2):
        c0 = 1536 + h * LANES
        xqn_ref[:, h * LANES:(h + 1) * LANES] = _seg_rms(z[:, c0:c0 + LANES], seg, xg_ref[...]).astype(BF16)
    gates_ref[...] = jax.nn.sigmoid(z[:, 1792:1920])


def _nsa_inproj_t(x, g, w, wt, seg, qg, kg, kgc, xg, tm, seq_len):
    t = x.shape[0]
    tps = seq_len // tm
    full = lambda a: pl.BlockSpec(a.shape, lambda i: (0,) * a.ndim)
    row = lambda n: pl.BlockSpec((tm, n), lambda i: (i, 0))
    col = lambda n: pl.BlockSpec((1, n, tm), lambda i: (i // tps, 0, i % tps))
    return pl.pallas_call(
        _nsa_inproj_t_kernel,
        grid=(t // tm,),
        in_specs=[row(D_MODEL), full(g), full(w), full(wt), full(seg), full(qg), full(kg), full(kgc), full(xg)],
        out_specs=[row(768), col(512), col(256), row(512), row(256), row(128)],
        out_shape=[jax.ShapeDtypeStruct((t, 768), BF16), jax.ShapeDtypeStruct((t // seq_len, 512, seq_len), F32),
                   jax.ShapeDtypeStruct((t // seq_len, 256, seq_len), F32), jax.ShapeDtypeStruct((t, 512), BF16),
                   jax.ShapeDtypeStruct((t, 256), BF16), jax.ShapeDtypeStruct((t, 128), F32)],
        compiler_params=_cparams("parallel"),
    )(x, g, w, wt, seg, qg, kg, kgc, xg)


def _nsa_inproj(x, g, w, seg, qg, kg, xg, tm):
    t = x.shape[0]
    full = lambda a: pl.BlockSpec(a.shape, lambda i: (0,) * a.ndim)
    row = lambda n: pl.BlockSpec((tm, n), lambda i: (i, 0))
    return pl.pallas_call(
        _nsa_inproj_kernel,
        grid=(t // tm,),
        in_specs=[row(D_MODEL), full(g), full(w), full(seg), full(qg), full(kg), full(xg)],
        out_specs=[row(768), row(512), row(256), row(512), row(256), row(128)],
        out_shape=[jax.ShapeDtypeStruct((t, 768), BF16), jax.ShapeDtypeStruct((t, 512), F32),
                   jax.ShapeDtypeStruct((t, 256), F32), jax.ShapeDtypeStruct((t, 512), BF16),
                   jax.ShapeDtypeStruct((t, 256), BF16), jax.ShapeDtypeStruct((t, 128), F32)],
        compiler_params=_cparams("parallel"),
    )(x, g, w, seg, qg, kg, xg)


def _pool_inproj_kernel(x_ref, g_ref, w_ref, seg_ref, xg_ref, u_ref, xqn_ref):
    xn = _rms_rows(x_ref[...], g_ref[...])
    z = _dot(xn.astype(BF16), w_ref[...])
    u_ref[...] = z[:, 0:POOL_WIDTH]
    seg = seg_ref[...]
    for h in range(2):
        c0 = POOL_WIDTH + h * LANES
        xqn_ref[:, h * LANES:(h + 1) * LANES] = _seg_rms(z[:, c0:c0 + LANES], seg, xg_ref[...]).astype(BF16)


def _pool_inproj(x, g, w, seg, xg, tm):
    t = x.shape[0]
    full = lambda a: pl.BlockSpec(a.shape, lambda i: (0,) * a.ndim)
    row = lambda n: pl.BlockSpec((tm, n), lambda i: (i, 0))
    return pl.pallas_call(
        _pool_inproj_kernel,
        grid=(t // tm,),
        in_specs=[row(D_MODEL), full(g), full(w), full(seg), full(xg)],
        out_specs=[row(POOL_WIDTH), row(256)],
        out_shape=[jax.ShapeDtypeStruct((t, POOL_WIDTH), F32), jax.ShapeDtypeStruct((t, 256), BF16)],
        compiler_params=_cparams("parallel"),
    )(x, g, w, seg, xg)


def _memkv_kernel(x_ref, g_ref, w_ref, seg_ref, kg_ref, o_ref):
    xn = _rms_rows(x_ref[...], g_ref[...])
    z = _dot(xn.astype(BF16), w_ref[...])
    seg = seg_ref[...]
    for h in range(2):
        o_ref[:, h * LANES:(h + 1) * LANES] = _seg_rms(z[:, h * LANES:(h + 1) * LANES], seg, kg_ref[...])
    o_ref[:, 256:512] = z[:, 256:512]


def _memkv(x, g, w, seg, kg, tm):
    t = x.shape[0]
    full = lambda a: pl.BlockSpec(a.shape, lambda i: (0,) * a.ndim)
    row = lambda n: pl.BlockSpec((tm, n), lambda i: (i, 0))
    return pl.pallas_call(
        _memkv_kernel,
        grid=(t // tm,),
        in_specs=[row(D_MODEL), full(g), full(w), full(seg), full(kg)],
        out_specs=row(512),
        out_shape=jax.ShapeDtypeStruct((t, 512), F32),
        compiler_params=_cparams("parallel"),
    )(x, g, w, seg, kg)


def _compress_kernel(x_ref, pos_ref, w1_ref, b1_ref, w2_ref, b2_ref, kg_ref, seg_ref, o_ref, acc_ref):
    lc = pl.program_id(1)

    @pl.when(lc == 0)
    def _():
        acc_ref[...] = jnp.zeros_like(acc_ref)

    for comp in range(2):
        part = None
        for l in range(8):
            xl = x_ref[:, l, comp * LANES:(comp + 1) * LANES] + pos_ref[comp, l]
            d = _dot(xl.astype(BF16), w1_ref[comp, l])
            part = d if part is None else part + d
        acc_ref[:, comp * 256:(comp + 1) * 256] += part

    @pl.when(lc == pl.num_programs(1) - 1)
    def _():
        for comp in range(2):
            h = jax.nn.gelu(acc_ref[:, comp * 256:(comp + 1) * 256] + b1_ref[comp])
            o = _dot(h.astype(BF16), w2_ref[comp]) + b2_ref[comp]
            if comp == 0:
                o = _seg_rms(o, seg_ref[...], kg_ref[...])
            o_ref[:, comp * LANES:(comp + 1) * LANES] = o


def _compress(x3, pos_t, w1bd, b1t, w2bd, b2t, kg, seg, nbt):
    nb, _, w = x3.shape
    nbt = min(nbt, nb)
    assert nb % nbt == 0
    full = lambda a: pl.BlockSpec(a.shape, lambda j, l: (0,) * a.ndim)
    return pl.pallas_call(
        _compress_kernel,
        grid=(nb // nbt, 8),
        in_specs=[pl.BlockSpec((nbt, 8, 256), lambda j, l: (j, l, 0)),
                  pl.BlockSpec((2, 8, 1, LANES), lambda j, l: (0, l, 0, 0)),
                  pl.BlockSpec((2, 8, LANES, 256), lambda j, l: (0, l, 0, 0)),
                  full(b1t), full(w2bd), full(b2t), full(kg), full(seg)],
        out_specs=pl.BlockSpec((nbt, 256), lambda j, l: (j, 0)),
        out_shape=jax.ShapeDtypeStruct((nb, 256), F32),
        scratch_shapes=[pltpu.VMEM((nbt, 512), F32)],
        compiler_params=_cparams("parallel", "arbitrary"),
    )(x3, pos_t, w1bd, b1t, w2bd, b2t, kg, seg)


def _compress_cache_kernel(x_ref, pos_ref, w1_ref, b1_ref, w2_ref, b2_ref, kg_ref, seg_ref, o_ref, acc_ref):
    dc = pl.program_id(1)

    @pl.when(dc == 0)
    def _():
        acc_ref[...] = jnp.zeros_like(acc_ref)

    pt = x_ref.shape[0]
    x2 = x_ref.reshape(pt * 32, LANES)

    def feature_rows(comp, g, dd):
        return (x2[pl.ds((comp * 2 + g) * 8 + dd, pt, stride=32), :] + pos_ref[comp, dd]).astype(BF16)

    for comp in range(2):
        for g in range(2):
            part = None
            for dp in range(4):
                xd = jnp.concatenate([feature_rows(comp, g, 2 * dp), feature_rows(comp, g, 2 * dp + 1)], axis=1)
                d = _dot(xd, w1_ref[comp, dp])
                part = d if part is None else part + d
            c0 = (comp * 2 + g) * 256
            acc_ref[:, c0:c0 + 256] += part

    @pl.when(dc == pl.num_programs(1) - 1)
    def _():
        for comp in range(2):
            hid = [jax.nn.gelu(acc_ref[:, (comp * 2 + g) * 256:(comp * 2 + g + 1) * 256] + b1_ref[comp]).astype(BF16)
                   for g in range(2)]
            for blk in range(2):
                o = _dot(hid[0], w2_ref[comp, blk, 0]) + _dot(hid[1], w2_ref[comp, blk, 1]) + b2_ref[comp]
                if comp == 0:
                    o = _seg_rms(o, seg_ref[...], kg_ref[...])
                o_ref[:, blk * 256 + comp * LANES:blk * 256 + (comp + 1) * LANES] = o


def _compress_cache(xt, pos_t, w1t, b1t, w2sel, b2t, kg, seg, pt):
    n_phys = xt.shape[0]
    pt = min(pt, n_phys)
    assert n_phys % pt == 0
    full = lambda a: pl.BlockSpec(a.shape, lambda j, d: (0,) * a.ndim)
    return pl.pallas_call(
        _compress_cache_kernel,
        grid=(n_phys // pt, 8),
        in_specs=[pl.BlockSpec((pt, 2, 2, 8, LANES), lambda j, d: (j, 0, 0, d, 0)),
                  pl.BlockSpec((2, 8, 1, LANES), lambda j, d: (0, d, 0, 0)),
                  pl.BlockSpec((2, 4, 2 * LANES, 256), lambda j, d: (0, d, 0, 0)),
                  full(b1t), full(w2sel), full(b2t), full(kg), full(seg)],
        out_specs=pl.BlockSpec((pt, 512), lambda j, d: (j, 0)),
        out_shape=jax.ShapeDtypeStruct((n_phys, 512), F32),
        scratch_shapes=[pltpu.VMEM((pt, 1024), F32)],
        compiler_params=_cparams("parallel", "arbitrary"),
    )(xt, pos_t, w1t, b1t, w2sel, b2t, kg, seg)


def _bucket_np(d):
    n = np.maximum(d, 0)
    max_exact = REL_BUCKETS // 2
    nf = np.maximum(n, 1).astype(np.float32)
    large = max_exact + (np.log(nf / max_exact) / math.log(REL_MAX_DIST / max_exact)
                         * (REL_BUCKETS - max_exact)).astype(np.int32)
    large = np.minimum(large, REL_BUCKETS - 1)
    return np.where(n < max_exact, n, large).astype(np.int32)


def _bias_table_kernel(rb_ref, code_ref, o_ref):
    code = code_ref[...]
    for h in range(NSA_HEADS):
        far = rb_ref[REL_BUCKETS - 1, h]
        acc = jnp.full(code.shape, -BIG, F32)
        for k in range(REL_BUCKETS):
            acc = jnp.where(code == k, rb_ref[k, h] - far, acc)
        o_ref[h] = acc


def _bias_table(rel_bias, code):
    code = jnp.asarray(code, jnp.int32)
    return pl.pallas_call(
        _bias_table_kernel,
        in_specs=[pl.BlockSpec(memory_space=pltpu.SMEM), pl.BlockSpec(code.shape, lambda: (0, 0))],
        out_specs=pl.BlockSpec((NSA_HEADS,) + code.shape, lambda: (0, 0, 0)),
        out_shape=jax.ShapeDtypeStruct((NSA_HEADS,) + code.shape, F32),
    )(rel_bias, code)


def _window_codes():
    ql = np.arange(64)[:, None]
    j = np.arange(WIN_KEYS)[None, :]
    d = WIN_KEYS - 64 + ql - j
    return np.where((d >= 0) & (d <= WINDOW), _bucket_np(d), -1).astype(np.int32)


def _cmp_codes():
    delta = np.arange(8)[:, None]
    ql = np.arange(64)[None, :]
    d = delta * CMP_BLOCK + ql - (CMP_BLOCK - 1)
    return np.where(d >= 0, _bucket_np(d), -1).astype(np.int32)


FAR_TILE = 512


def _nsa_prompt_kernel(qn_ref, gates_ref, cmp_ref, kv_ref, oh_ref, wb_ref, nb_ref, ct_ref, amat_ref, o_ref):
    i = pl.program_id(1)
    nc = cmp_ref.shape[1]
    gr = NSA_GROUP * 64
    rows = 2 * gr
    q = qn_ref[0].astype(F32)
    gts = gates_ref[0]
    kc = cmp_ref[0, :, 0:128].astype(BF16)
    vc = cmp_ref[0, :, 128:256].astype(BF16)
    lane = lax.broadcasted_iota(jnp.int32, (64, LANES), 1)
    half = [lane < 64, lane >= 64]
    qall = jnp.concatenate([jnp.where(half[g], q[:, h * LANES:(h + 1) * LANES], 0.0)
                            for g in range(2) for h in range(NSA_GROUP)], axis=0).astype(BF16)

    blk_r = lax.broadcasted_iota(jnp.int32, (nc, rows), 0)
    ct = ct_ref[...]
    bias = jnp.where(blk_r == i, ct[0:1], jnp.where(blk_r == i - 1, ct[1:2], jnp.where(
        blk_r == i - 2, ct[2:3], jnp.where(blk_r > i, -BIG, 0.0))))
    lc = _nt(kc, qall) + bias
    e = jnp.exp(lc - jnp.max(lc, axis=0, keepdims=True))
    p = jnp.where(bias > -0.5 * BIG, e / jnp.sum(e, axis=0, keepdims=True), 0.0)
    o_c = _tn(p.astype(BF16), vc)
    imp = []
    for g in range(2):
        s3 = p[:, g * gr:g * gr + 128] + p[:, g * gr + 128:g * gr + 256] + p[:, g * gr + 256:g * gr + 384]
        imp.append(s3 + pltpu.roll(s3, 64, axis=1))
    lane_c = lax.broadcasted_iota(jnp.int32, (nc, LANES), 1)
    blk = lax.broadcasted_iota(jnp.int32, (nc, LANES), 0)
    impp = jnp.where(lane_c < 64, imp[0], imp[1])
    forced = (blk == 0) | (blk == i) | (blk == i - 1)
    score = jnp.where(blk <= i, jnp.where(forced, FORCED_SCORE, impp), -1.0)
    sc8 = [score[r * 8:(r + 1) * 8] for r in range(nc // 8)]
    blk8 = lax.broadcasted_iota(jnp.int32, (8, LANES), 0)
    rank8 = [jnp.zeros((8, LANES), F32) for _ in sc8]
    for cp in range(nc):
        row = score[cp:cp + 1, :]
        for r in range(nc // 8):
            if r * 8 > cp:
                beats = row >= sc8[r]
            elif r * 8 + 7 < cp:
                beats = row > sc8[r]
            else:
                beats = (row > sc8[r]) | ((row == sc8[r]) & (blk8 + r * 8 > cp))
            rank8[r] = rank8[r] + jnp.where(beats, 1.0, 0.0)
    rank = jnp.concatenate(rank8, axis=0)
    notsel =jnp.where((rank < float(min(SLC_TOPN, nc))) & (score >= 0.0), 0.0, 1.0)
    notsel_far = jnp.where(blk > i - 3, 1.0, notsel)
    tail = jnp.where(lax.broadcasted_iota(jnp.int32, (LANES - nc, LANES), 0) == 64 - nc, 1.0, 0.0)
    ns_near = jnp.concatenate([notsel, tail], axis=0).astype(BF16)
    ns_far = jnp.concatenate([notsel_far, tail], axis=0).astype(BF16)
    a = amat_ref[...]
    lhs_near = jnp.concatenate([qall, (_nt(a, ns_near) * -BIG).astype(BF16)], axis=1)
    lhs_far = jnp.concatenate([qall, (_nt(a, ns_far) * -BIG).astype(BF16)], axis=1)
    lane_r = lax.broadcasted_iota(jnp.int32, (rows, LANES), 1)
    lhs_win = jnp.concatenate([qall, jnp.where(lane_r == 64, -BIG, 0.0).astype(BF16)], axis=1)

    def keys_aug(r0, n, c0):
        return jnp.concatenate([kv_ref[0, pl.ds(r0, n), c0:c0 + LANES], oh_ref[pl.ds(r0, n), :]], axis=1)

    def fold(x, op):
        return functools.reduce(op, [x[:, c * LANES:(c + 1) * LANES] for c in range(x.shape[1] // LANES)])

    ntiles = (i + 5) // (FAR_TILE // 64)
    r0n = pl.multiple_of(PAD_ROWS + (i - 3) * 64, 64)
    s_near = _nt(lhs_near, keys_aug(r0n, NEAR_KEYS, 0)) + nb_ref[...]

    def tile_row(t):
        return pl.multiple_of(PAD_ROWS + t * FAR_TILE, LANES)

    def far_logits(t):
        return _nt(lhs_far, keys_aug(tile_row(t), FAR_TILE, 0))

    def absorb(s, v, m, lrun, acc):
        m_new = jnp.maximum(m, jnp.max(fold(s, jnp.maximum), axis=1, keepdims=True))
        alpha = jnp.exp(m - m_new)
        e = jnp.exp(s - m_new)
        return m_new, alpha * lrun + fold(e, jnp.add), alpha * acc + _dot(e.astype(BF16), v)

    assert ((nc + 4) // (FAR_TILE // 64)) % 2 == 0

    def far_pair(u, carry):
        sa, sb = far_logits(2 * u), far_logits(2 * u + 1)
        carry = absorb(sa, kv_ref[0, pl.ds(tile_row(2 * u), FAR_TILE), 128:256], *carry)
        return absorb(sb, kv_ref[0, pl.ds(tile_row(2 * u + 1), FAR_TILE), 128:256], *carry)

    zero = jnp.zeros((rows, LANES), F32)
    carry = lax.fori_loop(0, (ntiles + 1) // 2, far_pair, (jnp.full((rows, 1), -BIG, F32), zero, zero))
    _, lrun, acc = absorb(s_near, kv_ref[0, pl.ds(r0n, NEAR_KEYS), 128:256], *carry)
    o_s = acc / jnp.sum(lrun, axis=1, keepdims=True)

    r0w = pl.multiple_of(PAD_ROWS + (i - 9) * 64, 64)
    s = _nt(lhs_win, keys_aug(r0w, WIN_KEYS, 256)) + wb_ref[...]
    e = jnp.exp(s - jnp.max(fold(s, jnp.maximum), axis=1, keepdims=True))
    o_w = _dot(e.astype(BF16), kv_ref[0, pl.ds(r0w, WIN_KEYS), 384:512]) / jnp.sum(fold(e, jnp.add), axis=1,
                                                                                 keepdims=True)

    low = lax.broadcasted_iota(jnp.int32, (64, LANES), 1) < 64
    for h in range(NSA_GROUP):
        per_group = []
        for g in range(2):
            c0 = (g * NSA_GROUP + h) * 3
            r = slice(g * gr + h * 64, g * gr + (h + 1) * 64)
            per_group.append(gts[:, c0:c0 + 1] * o_c[r] + gts[:, c0 + 1:c0 + 2] * o_s[r]
                             + gts[:, c0 + 2:c0 + 3] * o_w[r])
        o_ref[0, :, h * LANES:(h + 1) * LANES] = jnp.where(low, per_group[0], per_group[1]).astype(BF16)


def _nsa_prompt_attn(qn, gates, cmp, kvatt, oh, wb, nb, ct, amat):
    b, l, _ = qn.shape
    nc = l // 64
    full = lambda a: pl.BlockSpec(a.shape, lambda bi, i: (0,) * a.ndim)
    return pl.pallas_call(
        _nsa_prompt_kernel,
        grid=(b, nc),
        in_specs=[pl.BlockSpec((1, 64, 768), lambda bi, i: (bi, i, 0)),
                  pl.BlockSpec((1, 64, 128), lambda bi, i: (bi, i, 0)),
                  pl.BlockSpec((1, nc, 256), lambda bi, i: (bi, 0, 0)),
                  pl.BlockSpec((1, PAD_ROWS + l, 512), lambda bi, i: (bi, 0, 0)),
                  full(oh), full(wb), full(nb), full(ct), full(amat)],
        out_specs=pl.BlockSpec((1, 64, 768), lambda bi, i: (bi, i, 0)),
        out_shape=jax.ShapeDtypeStruct((b, l, 768), BF16),
        compiler_params=_cparams("parallel", "arbitrary"),
    )(qn, gates, cmp, kvatt, oh, wb, nb, ct, amat)


def _mem_attn_kernel(q_ref, kv_ref, o_ref):
    q = q_ref[0].astype(F32)
    k = kv_ref[0, :, 0:256].astype(BF16)
    v = kv_ref[0, :, 256:512].astype(BF16)
    head = lax.broadcasted_iota(jnp.int32, q.shape, 1) // HEAD_DIM
    out = jnp.zeros(q.shape, F32)
    for h in range(XATTN_HEADS):
        qh = jnp.where(head == h, q, 0.0).astype(BF16)
        s = _nt(qh, k)
        e = jnp.exp(s - jnp.max(s, axis=1, keepdims=True))
        p = e / jnp.sum(e, axis=1, keepdims=True)
        out = out + jnp.where(head == h, _dot(p.astype(BF16), v), 0.0)
    o_ref[0] = out.astype(BF16)


def _mem_attn(xqn, mkv, tq):
    b, l, _ = xqn.shape
    nm = mkv.shape[1]
    return pl.pallas_call(
        _mem_attn_kernel,
        grid=(b, l // tq),
        in_specs=[pl.BlockSpec((1, tq, 256), lambda bi, i: (bi, i, 0)),
                  pl.BlockSpec((1, nm, 512), lambda bi, i: (bi, 0, 0))],
        out_specs=pl.BlockSpec((1, tq, 256), lambda bi, i: (bi, i, 0)),
        out_shape=jax.ShapeDtypeStruct((b, l, 256), BF16),
        compiler_params=_cparams("parallel", "parallel"),
    )(xqn, mkv)


def _pool_kernel(u_ref, prev_ref, w_ref, scale_ref, o_ref, carry_ref, *, pos0):
    i = pl.program_id(1)
    tp = u_ref.shape[1]

    @pl.when(i == 0)
    def _():
        carry_ref[...] = prev_ref[0]

    u = u_ref[0]
    ext = jnp.concatenate([carry_ref[...], u], axis=0)
    carry_ref[...] = ext[tp:tp + 16]
    s2 = ext + pltpu.roll(ext, 1, axis=0)
    s4 = s2 + pltpu.roll(s2, 2, axis=0)
    s8 = s4 + pltpu.roll(s4, 4, axis=0)
    s16 = s8 + pltpu.roll(s8, 8, axis=0)
    lane = lax.broadcasted_iota(jnp.int32, (tp, POOL_WIDTH), 1)
    pos1 = (pos0 + 1 + i * tp + lax.broadcasted_iota(jnp.int32, (tp, POOL_WIDTH), 0)).astype(F32)
    grp = lane // POOL_GROUP_DIM
    ssum = jnp.where(grp == 0, s2[16:], jnp.where(grp == 1, s4[16:], jnp.where(grp == 2, s8[16:], s16[16:])))
    win = jnp.where(grp == 0, 2.0, jnp.where(grp == 1, 4.0, jnp.where(grp == 2, 8.0, 16.0)))
    pooled = ssum / jnp.minimum(win, pos1) - u
    o_ref[0] = (_dot(pooled.astype(BF16), w_ref[...]) * scale_ref[...]).astype(BF16)


def _pool_mix(u, prev16, wbd, scale, tp, pos0):
    b, l, _ = u.shape
    full = lambda a: pl.BlockSpec(a.shape, lambda bi, i: (0,) * a.ndim)
    return pl.pallas_call(
        functools.partial(_pool_kernel, pos0=pos0),
        grid=(b, l // tp),
        in_specs=[pl.BlockSpec((1, tp, POOL_WIDTH), lambda bi, i: (bi, i, 0)),
                  pl.BlockSpec((1, 16, POOL_WIDTH), lambda bi, i: (bi, 0, 0)),
                  full(wbd), full(scale)],
        out_specs=pl.BlockSpec((1, tp, POOL_WIDTH), lambda bi, i: (bi, i, 0)),
        out_shape=jax.ShapeDtypeStruct((b, l, POOL_WIDTH), BF16),
        scratch_shapes=[pltpu.VMEM((16, POOL_WIDTH), F32)],
        compiler_params=_cparams("parallel", "arbitrary"),
    )(u, prev16, wbd, scale)


def _outproj_kernel(y_ref, mix_ref, att_ref, wm_ref, wa_ref, g_ref, o_ref, h_ref):
    y = y_ref[...] + _dot(mix_ref[...], wm_ref[...]) + _dot(att_ref[...], wa_ref[...])
    o_ref[...] = y
    h_ref[...] = _rms_rows(y, g_ref[...]).astype(BF16)


def _outproj(y, mix, att, wm, wa, g_ffn, tm):
    t = y.shape[0]
    full = lambda a: pl.BlockSpec(a.shape, lambda i: (0,) * a.ndim)
    row = lambda n: pl.BlockSpec((tm, n), lambda i: (i, 0))
    return pl.pallas_call(
        _outproj_kernel,
        grid=(t // tm,),
        in_specs=[row(D_MODEL), row(768), row(256), full(wm), full(wa), full(g_ffn)],
        out_specs=[row(D_MODEL), row(D_MODEL)],
        out_shape=[jax.ShapeDtypeStruct((t, D_MODEL), F32), jax.ShapeDtypeStruct((t, D_MODEL), BF16)],
        compiler_params=_cparams("parallel"),
    )(y, mix, att, wm, wa, g_ffn)


def _ffn_half(h, w_ref, cw_ref, cb_ref, ext_ref, prev8, fix):
    tm = h.shape[0]
    u = _dot(h, w_ref[...])
    ext_ref[0:8, :] = prev8
    ext_ref[8:8 + tm, :] = u
    s1 = ext_ref[7:7 + tm, :]
    s2 = ext_ref[6:6 + tm, :]
    if fix is not None:
        rowm, p1, p2 = fix
        s1 = jnp.where(rowm >= 1, s1, p1)
        s2 = jnp.where(rowm >= 2, s2, p2)
    cw = cw_ref[...]
    return u, cb_ref[...] + cw[0:1] * s2 + cw[1:2] * s1 + cw[2:3] * u


def _ffn_prompt_kernel(y_ref, h_ref, wa_ref, wv_ref, cwa_ref, cwv_ref, cba_ref, cbv_ref, wd_ref,
                       o_ref, sa_ref, sv_ref, acc_ref, ext_ref, carry_ref, *, tiles_per_seq):
    j = pl.program_id(0)
    c = pl.program_id(1)
    tm = y_ref.shape[0]

    @pl.when((j % tiles_per_seq) == 0)
    def _():
        carry_ref[c] = jnp.zeros(carry_ref.shape[1:], F32)

    h = h_ref[...]
    ua, ya = _ffn_half(h, wa_ref, cwa_ref, cba_ref, ext_ref, carry_ref[c, 0], None)
    carry_ref[c, 0] = ua[tm - 8:tm]
    sa_ref[0] = ua[tm - 8:tm]
    uv, yv = _ffn_half(h, wv_ref, cwv_ref, cbv_ref, ext_ref, carry_ref[c, 1], None)
    carry_ref[c, 1] = uv[tm - 8:tm]
    sv_ref[0] = uv[tm - 8:tm]
    d = _dot((jax.nn.silu(ya) * yv).astype(BF16), wd_ref[...])

    @pl.when(c == 0)
    def _():
        acc_ref[...] = d

    @pl.when(c == pl.num_programs(1) - 1)
    def _():
        o_ref[...] = y_ref[...] + acc_ref[...] + d


def _ffn_prompt(y, h, w_up, conv_w, conv_b, w_down, tm, seq_len):
    t = y.shape[0]
    nseq = t // seq_len
    tps = seq_len // tm
    fc = FF_CHUNK
    ncf = D_FF // fc
    full = lambda a: pl.BlockSpec(a.shape, lambda j, c: (0,) * a.ndim)
    outs = pl.pallas_call(
        functools.partial(_ffn_prompt_kernel, tiles_per_seq=tps),
        grid=(t // tm, ncf),
        in_specs=[pl.BlockSpec((tm, D_MODEL), lambda j, c: (j, 0)), pl.BlockSpec((tm, D_MODEL), lambda j, c: (j, 0)),
                  pl.BlockSpec((D_MODEL, fc), lambda j, c: (0, c)),
                  pl.BlockSpec((D_MODEL, fc), lambda j, c: (0, ncf + c)),
                  pl.BlockSpec((3, fc), lambda j, c: (0, c)),
                  pl.BlockSpec((3, fc), lambda j, c: (0, ncf + c)),
                  pl.BlockSpec((1, fc), lambda j, c: (0, c)),
                  pl.BlockSpec((1, fc), lambda j, c: (0, ncf + c)),
                  pl.BlockSpec((fc, D_MODEL), lambda j, c: (c, 0))],
        out_specs=[pl.BlockSpec((tm, D_MODEL), lambda j, c: (j, 0)),
                   pl.BlockSpec((1, 8, fc), lambda j, c: (j, 0, c)),
                   pl.BlockSpec((1, 8, fc), lambda j, c: (j, 0, c))],
        out_shape=[jax.ShapeDtypeStruct((t, D_MODEL), F32),
                   jax.ShapeDtypeStruct((t // tm, 8, D_FF), F32),
                   jax.ShapeDtypeStruct((t // tm, 8, D_FF), F32)],
        scratch_shapes=[pltpu.VMEM((tm, D_MODEL), F32),
                        pltpu.VMEM((tm + 8, fc), F32), pltpu.VMEM((ncf, 2, 8, fc), F32)],
        compiler_params=_cparams("arbitrary", "arbitrary"),
    )(y, h, w_up, w_up, conv_w, conv_w, conv_b, conv_b, w_down)
    y_out, sa, sv = outs
    last = lambda s: s.reshape(nseq, tps, 8, D_FF)[:, tps - 1, 6:8]
    state = jnp.concatenate([last(sa), last(sv)], axis=-1)
    return y_out, state


def _ffn_sample_kernel(y_ref, g_ref, wa_ref, wv_ref, cwa_ref, cwv_ref, cba_ref, cbv_ref, wd_ref,
                       p1a_ref, p1v_ref, p2a_ref, p2v_ref, o_ref, ua_ref, uv_ref, acc_ref, ext_ref):
    c = pl.program_id(0)
    h = _rms_rows(y_ref[...], g_ref[...]).astype(BF16)
    tm = h.shape[0]
    rowm = lax.broadcasted_iota(jnp.int32, (tm, wa_ref.shape[1]), 0) % 8
    zero8 = jnp.zeros((8, wa_ref.shape[1]), F32)
    ua, ya = _ffn_half(h, wa_ref, cwa_ref, cba_ref, ext_ref, zero8, (rowm, p1a_ref[...], p2a_ref[...]))
    uv, yv = _ffn_half(h, wv_ref, cwv_ref, cbv_ref, ext_ref, zero8, (rowm, p1v_ref[...], p2v_ref[...]))
    ua_ref[...] = ua
    uv_ref[...] = uv
    d = _dot((jax.nn.silu(ya) * yv).astype(BF16), wd_ref[...])

    @pl.when(c == 0)
    def _():
        acc_ref[...] = d

    @pl.when(c == pl.num_programs(0) - 1)
    def _():
        o_ref[...] = y_ref[...] + acc_ref[...] + d


def _ffn_sample(y, g, w_up, conv_w, conv_b, w_down, state):
    t = y.shape[0]
    nseq = t // 8
    fc = FF_CHUNK
    ncf = D_FF // fc
    z = jnp.zeros((nseq, 1, 2 * D_FF), F32)
    p1 = jnp.concatenate([state[:, 1:2]] + [z] * 7, axis=1).reshape(t, 2 * D_FF)
    p2 = jnp.concatenate([state[:, 0:1], state[:, 1:2]] + [z] * 6, axis=1).reshape(t, 2 * D_FF)
    full = lambda a: pl.BlockSpec(a.shape, lambda c: (0,) * a.ndim)
    ca = lambda r: pl.BlockSpec((r, fc), lambda c: (0, c))
    cv = lambda r: pl.BlockSpec((r, fc), lambda c: (0, ncf + c))
    y_out, ua, uv = pl.pallas_call(
        _ffn_sample_kernel,
        grid=(ncf,),
        in_specs=[full(y), full(g), ca(D_MODEL), cv(D_MODEL), ca(3), cv(3), ca(1), cv(1),
                  pl.BlockSpec((fc, D_MODEL), lambda c: (c, 0)), ca(t), cv(t), ca(t), cv(t)],
        out_specs=[pl.BlockSpec((t, D_MODEL), lambda c: (0, 0)), ca(t), ca(t)],
        out_shape=[jax.ShapeDtypeStruct((t, D_MODEL), F32), jax.ShapeDtypeStruct((t, D_FF), F32),
                   jax.ShapeDtypeStruct((t, D_FF), F32)],
        scratch_shapes=[pltpu.VMEM((t, D_MODEL), F32), pltpu.VMEM((t + 8, fc), F32)],
        compiler_params=_cparams("arbitrary"),
    )(y, g, w_up, w_up, conv_w, conv_w, conv_b, conv_b, w_down, p1, p1, p2, p2)
    u = jnp.concatenate([ua, uv], axis=-1).reshape(nseq, 8, 2 * D_FF)
    return y_out, u[:, 6:8]


KEY_CHUNK = 1024
NEW_PAD = 16


def _nsa_sample_kernel(pt_ref, qn_ref, gates_ref, rows_ref, winc_ref, winn_ref, scb_ref, sbt_ref, wbt_ref,
                       rmat_ref, eloc_ref, cmp_hbm, cache_hbm, o_ref, kvbuf, kcv, s_ref, m3_ref, sem, sem2):
    b = pl.program_id(0)
    npages = pt_ref.shape[1]
    ncs = 2 * npages
    nkeys = npages * PAGE_SIZE
    nch = nkeys // KEY_CHUNK
    ppc = KEY_CHUNK // PAGE_SIZE
    wbuf = winc_ref.shape[-1]

    slot = b % 2

    def page_copy(seq, p, sl):
        return pltpu.make_async_copy(cache_hbm.at[pt_ref[seq, p], pl.ds(2, 2)], kvbuf.at[sl, p], sem.at[sl])

    def cmp_copy(p):
        return pltpu.make_async_copy(cmp_hbm.at[pt_ref[b, p]], kcv.at[p], sem2)

    def start_cmp(p, c):
        cmp_copy(p).start()
        return c

    lax.fori_loop(0, npages, start_cmp, 0)

    @pl.when(b == 0)
    def _():
        def start_first(p, c):
            page_copy(0, p, 0).start()
            return c

        lax.fori_loop(0, npages, start_first, 0)

    @pl.when(b + 1 < pl.num_programs(0))
    def _():
        def start_next(p, c):
            page_copy(b + 1, p, 1 - slot).start()
            return c

        lax.fori_loop(0, npages, start_next, 0)

    q = qn_ref[0].astype(F32)
    gts = gates_ref[0]
    lane8 = lax.broadcasted_iota(jnp.int32, (8, LANES), 1)
    half = [lane8 < 64, lane8 >= 64]
    pieces = [jnp.where(half[g], q[:, h * LANES:(h + 1) * LANES], 0.0) for g in range(2) for h in range(NSA_GROUP)]
    qrows = jnp.concatenate(pieces + [jnp.zeros((32, LANES), F32)], axis=0).astype(BF16)

    def wait_cmp(p, c):
        cmp_copy(p).wait()
        return c

    lax.fori_loop(0, npages, wait_cmp, 0)

    kc = jnp.concatenate([kcv[:, 0, 0:128], kcv[:, 1, 0:128]], axis=0).astype(BF16)
    vc = jnp.concatenate([kcv[:, 0, 128:256], kcv[:, 1, 128:256]], axis=0).astype(BF16)
    lc = _nt(kc, qrows) + scb_ref[...]
    e = jnp.exp(lc - jnp.max(lc, axis=0, keepdims=True))
    p_c = e / jnp.sum(e, axis=0, keepdims=True)
    o_c = _tn(p_c.astype(BF16), vc)
    hi = p_c.astype(BF16)
    lo = (p_c - hi.astype(F32)).astype(BF16)
    imp = _dot(hi, rmat_ref[...]) + _dot(lo, rmat_ref[...])

    def blk_of(r):
        return jnp.where(r < npages, 2 * r, jnp.where(r < ncs, 2 * (r - npages) + 1, r))

    blk = blk_of(lax.broadcasted_iota(jnp.int32, (ncs + 8, LANES), 0))
    impx = jnp.concatenate([imp, jnp.zeros((8, LANES), F32)], axis=0)
    forced = (blk == 0) | (blk == ncs) | (blk == ncs - 1)
    score = jnp.where(blk <= ncs, jnp.where(forced, FORCED_SCORE, impx), -1.0)
    blkf = blk.astype(F32)
    work, picked = score, jnp.zeros((ncs + 8, LANES), F32)
    for _ in range(SLC_TOPN):
        top = jnp.max(work, axis=0, keepdims=True)
        first = jnp.min(jnp.where(work == top, blkf, float(2 * ncs + 16)), axis=0, keepdims=True)
        hit = blkf == first
        picked = jnp.where(hit, 1.0, picked)
        work = jnp.where(hit, -2.0, work)
    madd = jnp.where((picked > 0.0) & (score >= 0.0), 0.0, -BIG)
    eye = jnp.where(lax.broadcasted_iota(jnp.int32, (LANES, LANES), 0)
                    == lax.broadcasted_iota(jnp.int32, (LANES, LANES), 1), 1.0, 0.0).astype(BF16)
    hb = ppc
    for t in range(nch):
        mt = jnp.concatenate([madd[t * hb:(t + 1) * hb], madd[npages + t * hb:npages + (t + 1) * hb],
                              jnp.zeros((LANES - 2 * hb, LANES), F32)], axis=0).astype(BF16)
        m3_ref[t] = _nt(eye, mt).astype(BF16)

    def wait_page(p, c):
        page_copy(b, p, slot).wait()
        return c

    lax.fori_loop(0, npages, wait_page, 0)

    eloc = eloc_ref[...]
    pad8 = jnp.zeros((NEW_PAD - 8, LANES), F32)
    knew = jnp.concatenate([rows_ref[0, :, 256:384], pad8], axis=0).astype(BF16)
    vnew = jnp.concatenate([rows_ref[0, :, 384:512], pad8], axis=0).astype(BF16)
    kwn = jnp.concatenate([winn_ref[0, :, 0:128], pad8], axis=0).astype(BF16)
    vwn = jnp.concatenate([winn_ref[0, :, 128:256], pad8], axis=0).astype(BF16)
    kwt = winc_ref[0, 0].reshape(LANES, wbuf).astype(BF16)
    vwt = winc_ref[0, 1].reshape(LANES, wbuf).astype(BF16)
    near_pad = jnp.concatenate([jnp.zeros((LANES, KEY_CHUNK - PAGE_SIZE), F32), sbt_ref[:, 0:PAGE_SIZE]], axis=1)
    low = lax.broadcasted_iota(jnp.int32, (8, LANES), 1) < 64

    def page_rows(t, comp):
        tiles = [kvbuf[slot, t * ppc + pp, comp].reshape(LANES, PAGE_SIZE) for pp in range(ppc)]
        return jnp.concatenate(tiles, axis=1).astype(BF16)

    def s_body(t, m):
        lhs = jnp.concatenate([m3_ref[t], qrows], axis=1)
        s = _dot(lhs, jnp.concatenate([eloc, page_rows(t, 0)], axis=0))
        s = s + jnp.where(t == nch - 1, 1.0, 0.0) * near_pad
        s_ref[t] = s
        return jnp.maximum(m, jnp.max(s, axis=1, keepdims=True))

    s_new = _nt(qrows, knew) + sbt_ref[:, PAGE_SIZE:PAGE_SIZE + NEW_PAD]
    unroll = math.gcd(nch, 4)
    m = lax.fori_loop(0, nch, s_body, jnp.max(s_new, axis=1, keepdims=True), unroll=unroll)
    e_new = jnp.exp(s_new - m)

    def pv_body(t, carry):
        l, acc = carry
        et = jnp.exp(s_ref[t] - m)
        return l + jnp.sum(et, axis=1, keepdims=True), acc + _nt(et.astype(BF16), page_rows(t, 1))

    l, acc = lax.fori_loop(0, nch, pv_body, (jnp.sum(e_new, axis=1, keepdims=True),
                                             _dot(e_new.astype(BF16), vnew)), unroll=unroll)
    o_s = acc / l

    sw = _dot(qrows, kwt) + wbt_ref[:, 0:wbuf]
    swn = _nt(qrows, kwn) + wbt_ref[:, wbuf:wbuf + NEW_PAD]
    mw = jnp.maximum(jnp.max(sw, axis=1, keepdims=True), jnp.max(swn, axis=1, keepdims=True))
    ew = jnp.exp(sw - mw)
    ewn = jnp.exp(swn - mw)
    o_w = (_nt(ew.astype(BF16), vwt) + _dot(ewn.astype(BF16), vwn)) / (
        jnp.sum(ew, axis=1, keepdims=True) + jnp.sum(ewn, axis=1, keepdims=True))

    def gate_col(br):
        cols = [gts[:, hh * 3 + br:hh * 3 + br + 1] for hh in range(NSA_HEADS)]
        return jnp.concatenate(cols + [jnp.zeros((32, 1), F32)], axis=0)

    o = gate_col(0) * o_c + gate_col(1) * o_s + gate_col(2) * o_w
    for h in range(NSA_GROUP):
        r0, r1 = h * 8, (NSA_GROUP + h) * 8
        o_ref[0, :, h * LANES:(h + 1) * LANES] = jnp.where(low, o[r0:r0 + 8], o[r1:r1 + 8]).astype(BF16)


def _nsa_sample_attn(page_table, qn, gates, rows, winc, winn, scb, sbt, wbt, rmat, eloc, cmp_phys, cache):
    b, npages = page_table.shape
    ncs = 2 * npages
    nkeys = npages * PAGE_SIZE
    assert nkeys % KEY_CHUNK == 0
    nch = nkeys // KEY_CHUNK
    full = lambda a: pl.BlockSpec(a.shape, lambda bi, pt: (0,) * a.ndim)
    per_b = lambda a: pl.BlockSpec((1,) + a.shape[1:], lambda bi, pt: (bi,) + (0,) * (a.ndim - 1))
    grid_spec = pltpu.PrefetchScalarGridSpec(
        num_scalar_prefetch=1,
        grid=(b,),
        in_specs=[per_b(qn), per_b(gates), per_b(rows), per_b(winc), per_b(winn),
                  full(scb), full(sbt), full(wbt), full(rmat), full(eloc),
                  pl.BlockSpec(memory_space=pl.ANY), pl.BlockSpec(memory_space=pl.ANY)],
        out_specs=pl.BlockSpec((1, 8, 768), lambda bi, pt: (bi, 0, 0)),
        scratch_shapes=[pltpu.VMEM((2, npages, 2, 2, 64, PAGE_SIZE), F32), pltpu.VMEM((npages, 2, 256), F32),
                        pltpu.VMEM((nch, LANES, KEY_CHUNK), F32), pltpu.VMEM((nch, LANES, LANES), BF16),
                        pltpu.SemaphoreType.DMA((2,)), pltpu.SemaphoreType.DMA(())],
    )
    return pl.pallas_call(
        _nsa_sample_kernel,
        grid_spec=grid_spec,
        out_shape=jax.ShapeDtypeStruct((b, 8, 768), BF16),
        compiler_params=_cparams("arbitrary"),
    )(page_table, qn, gates, rows, winc, winn, scb, sbt, wbt, rmat, eloc, cmp_phys, cache)


def _block_diag2(w):
    z = jnp.zeros_like(w)
    return jnp.concatenate([jnp.concatenate([w, z], axis=-1), jnp.concatenate([z, w], axis=-1)], axis=-2)


def _q_slot_perm():
    return np.concatenate([np.r_[h * 64:(h + 1) * 64, (NSA_GROUP + h) * 64:(NSA_GROUP + h + 1) * 64]
                           for h in range(NSA_GROUP)])


def _sample_codes(ncs, wbuf):
    ql = np.arange(8)[None, :]
    c = np.arange(ncs)[:, None]
    scb = _bucket_np((ncs - c) * CMP_BLOCK + ql - (CMP_BLOCK - 1))
    kap = np.arange(PAGE_SIZE + NEW_PAD)[:, None]
    d = np.where(kap < PAGE_SIZE, PAGE_SIZE + ql - kap, ql - (kap - PAGE_SIZE))
    sbt = np.where((d >= 0) & (kap < PAGE_SIZE + 8), _bucket_np(d), -1)
    j = np.arange(wbuf + NEW_PAD)[:, None]
    d = wbuf + ql - j
    wbt = np.where((d >= 0) & (d <= WINDOW) & (j < wbuf + 8), _bucket_np(d), -1)
    return np.concatenate([scb, sbt, wbt], axis=0).astype(np.int32)


def kernel(x_prompt, x_sample, cache_nsa_kv, cache_nsa_win, cache_mem_kv, state_pool, state_ffn_conv,
           page_table, mem_prompt, rel_bias, g_mix, g_mem, g_ffn, w_in_nsa, q_gain_nsa, k_gain_nsa,
           cmp_pos, cmp_w1, cmp_b1, cmp_w2, cmp_b2, w_in_pool, w_pool_grp, pool_scale, w_mem_kv,
           xq_gain, xk_gain, w_out, w_up, conv_w, conv_b, w_down):
    bp, lp, _ = x_prompt.shape
    bs, ls, _ = x_sample.shape
    assert ls == 8 and lp % 512 == 0 and lp // 64 <= 64
    npages = page_table.shape[1]
    past_len = npages * PAGE_SIZE
    n_phys = cache_nsa_kv.shape[1]
    wbuf = cache_nsa_win.shape[2]
    nm = mem_prompt.shape[1]
    tp_, ts_ = bp * lp, bs * ls
    nc = lp // 64
    ncs = 2 * npages
    scale = HEAD_DIM ** -0.5
    tile2 = lambda v: jnp.tile(v, 2)[None, :]

    a = np.arange(LANES)
    seg = jnp.asarray(a[:, None] // 64 == a[None, :] // 64, BF16)
    perm = _q_slot_perm()

    wb = _bias_table(rel_bias, _window_codes()).reshape(NSA_HEADS * 64, WIN_KEYS)
    nbias = jnp.concatenate([jnp.full((NSA_HEADS * 64, 64), -BIG, F32), wb[:, WIN_KEYS - 192:]], axis=1)
    ct = _bias_table(rel_bias, _cmp_codes()).transpose(1, 0, 2).reshape(8, NSA_HEADS * 64)
    amat = np.zeros((2, NSA_GROUP * 64, LANES), np.float32)
    for g in range(2):
        for h in range(NSA_GROUP):
            amat[g, h * 64 + np.arange(64), g * 64 + np.arange(64)] = 1.0
    amat = jnp.asarray(amat.reshape(NSA_HEADS * 64, LANES), BF16)
    r = np.arange(PAD_ROWS + lp)
    oh_np = np.zeros((PAD_ROWS + lp, LANES), np.float32)
    oh_np[r, np.where(r < PAD_ROWS, 64, (r - PAD_ROWS) // 64)] = 1.0
    oh = jnp.asarray(oh_np, BF16)
    stab = _bias_table(rel_bias, _sample_codes(ncs, wbuf))
    stab = jnp.pad(stab.transpose(1, 0, 2).reshape(-1, NSA_HEADS * 8), ((0, 0), (0, LANES - NSA_HEADS * 8)))
    ns = PAGE_SIZE + NEW_PAD
    scb, sbt, wbt = stab[:ncs], stab[ncs:ncs + ns].T, stab[ncs + ns:].T
    scb = jnp.concatenate([scb[0::2], scb[1::2]], axis=0)
    ppc = KEY_CHUNK // PAGE_SIZE
    eloc_np = np.zeros((LANES, KEY_CHUNK), np.float32)
    for pp in range(ppc):
        eloc_np[pp, pp * PAGE_SIZE:pp * PAGE_SIZE + 64] = 1.0
        eloc_np[ppc + pp, pp * PAGE_SIZE + 64:(pp + 1) * PAGE_SIZE] = 1.0
    eloc = jnp.asarray(eloc_np, BF16)
    lam = np.arange(LANES)
    rm = (lam[:, None] < 96) & (lam[None, :] < 96) & (lam[:, None] // 48 == lam[None, :] // 48) \
        & (lam[:, None] % 8 == lam[None, :] % 8)
    rmat = jnp.asarray(rm, BF16)

    yp = x_prompt.reshape(tp_, D_MODEL)
    ys = x_sample.reshape(ts_, D_MODEL)
    tm = 512
    outs = {k: [] for k in ("kv_p", "kv_s", "win_p", "win_s", "mem_p", "pool_p", "pool_s", "conv_p", "conv_s")}
    depth = g_mix.shape[0]
    for i in range(depth):
        j = i // 2
        wo = w_out[i]
        xg = tile2(xq_gain[i]) * scale
        mkv_p = _memkv(mem_prompt.reshape(bp * nm, D_MODEL), g_mem[i][None], w_mem_kv[i].astype(BF16), seg,
                       tile2(xk_gain[i]), min(tm, bp * nm)).reshape(bp, nm, 2 * XATTN_WIDTH)
        outs["mem_p"].append(mkv_p.reshape(bp, nm, 2, XATTN_HEADS, HEAD_DIM))
        mkv_s = cache_mem_kv[i].reshape(bs, nm, 2 * XATTN_WIDTH)
        if i % 2 == 0:
            w = w_in_nsa[j]
            w = jnp.concatenate([w[:, perm], w[:, 768:1536], w[:, 1572:1828], w[:, 1536:1572],
                                 jnp.zeros((D_MODEL, NSA_IN_COLS - 1828), F32)], axis=1).astype(BF16)
            qg = tile2(q_gain_nsa[j]) * scale
            kg = tile2(k_gain_nsa[j])
            cmp_args = (jnp.tile(cmp_pos[j], (1, 1, 2))[:, :, None, :], _block_diag2(cmp_w1[j]).astype(BF16),
                        jnp.tile(cmp_b1[j], (1, 2))[:, None, :], _block_diag2(cmp_w2[j]).astype(BF16),
                        jnp.tile(cmp_b2[j], (1, 2))[:, None, :], kg, seg)
            wo_mix = wo[:768][perm].astype(BF16)
            w2 = cmp_w2[j]
            w2sel = jnp.zeros((2, 2, 2, 2 * CMP_HIDDEN, LANES), F32)
            for blk in range(2):
                for g in range(2):
                    w2sel = w2sel.at[:, blk, g, blk * CMP_HIDDEN:(blk + 1) * CMP_HIDDEN,
                                     g * HEAD_DIM:(g + 1) * HEAD_DIM].set(w2)
            cc_args = (jnp.tile(cmp_pos[j].transpose(0, 2, 1), (1, 1, 2))[:, :, None, :],
                       _block_diag2(cmp_w1[j].transpose(0, 2, 1, 3)).astype(BF16).reshape(2, 32, 2 * LANES, 256),
                       cmp_args[2], w2sel.astype(BF16), cmp_args[4], kg, seg, 512)
            wt = w_in_nsa[j][:, 768:1536].T.astype(BF16)
            qn, rows_t, win_t, katt, xqn_p, gates = _nsa_inproj_t(
                yp, g_mix[i][None], w, wt, seg, qg, kg, jnp.tile(k_gain_nsa[j], 2)[:, None], xg, tm, lp)
            pages_t = rows_t[:, :256].reshape(bp, 2, 2, HEAD_DIM, lp // PAGE_SIZE, PAGE_SIZE)
            pages_t = pages_t.transpose(0, 4, 1, 2, 3, 5).reshape(bp * lp // PAGE_SIZE, 2, 2, HEAD_DIM, PAGE_SIZE)
            cmp_p = _compress_cache(pages_t, *cc_args).reshape(bp, nc, 256)
            kvatt = jnp.pad(katt.reshape(bp, lp, 512), ((0, 0), (PAD_ROWS, 0), (0, 0)))
            mix_p = _nsa_prompt_attn(qn.reshape(bp, lp, 768), gates.reshape(bp, lp, 128), cmp_p, kvatt,
                                     oh, wb, nbias, ct, amat).reshape(tp_, 768)
            outs["kv_p"].append(rows_t.reshape(bp, 4, NSA_KV_HEADS, HEAD_DIM, lp).transpose(0, 4, 1, 2, 3))
            wlen = min(WINDOW, lp)
            outs["win_p"].append(win_t[:, :, lp - wlen:].reshape(bp, 2, NSA_KV_HEADS, HEAD_DIM, wlen)
                                 .transpose(0, 4, 1, 2, 3))
            qn, rows, win, _, xqn_s, gates = _nsa_inproj(ys, g_mix[i][None], w, seg, qg, kg, xg, ts_)
            cache_t = cache_nsa_kv[j].transpose(0, 2, 3, 4, 1)
            cmp_phys = _compress_cache(cache_t, *cc_args)
            winc = cache_nsa_win[j].transpose(0, 2, 3, 4, 1)
            mix_s = _nsa_sample_attn(page_table, qn.reshape(bs, ls, 768), gates.reshape(bs, ls, 128),
                                     rows.reshape(bs, ls, 512), winc, win.reshape(bs, ls, 256),
                                     scb, sbt, wbt, rmat, eloc, cmp_phys.reshape(n_phys, 2, 256),
                                     cache_t).reshape(ts_, 768)
            outs["kv_s"].append(rows.reshape(bs, ls, 4, NSA_KV_HEADS, HEAD_DIM))
            wall = jnp.concatenate([cache_nsa_win[j], win.reshape(bs, ls, 2, NSA_KV_HEADS, HEAD_DIM)], axis=1)
            outs["win_s"].append(wall[:, -wbuf:])
        else:
            w = w_in_pool[j].astype(BF16)
            wbd = jnp.zeros((POOL_WIDTH, POOL_WIDTH), F32)
            for gi in range(len(POOL_WINDOWS)):
                sl = slice(gi * POOL_GROUP_DIM, (gi + 1) * POOL_GROUP_DIM)
                wbd = wbd.at[sl, sl].set(w_pool_grp[j, gi])
            wbd = wbd.astype(BF16)
            psc = pool_scale[j][None]
            wo_mix = wo[:768].astype(BF16)
            u_p, xqn_p = _pool_inproj(yp, g_mix[i][None], w, seg, xg, tm)
            u_p3 = u_p.reshape(bp, lp, POOL_WIDTH)
            mix_p = _pool_mix(u_p3, jnp.zeros((bp, 16, POOL_WIDTH), F32), wbd, psc, tm, 0).reshape(tp_, 768)
            outs["pool_p"].append(u_p3[:, -POOL_STATE:])
            u_s, xqn_s = _pool_inproj(ys, g_mix[i][None], w, seg, xg, ts_)
            u_s3 = u_s.reshape(bs, ls, POOL_WIDTH)
            prev16 = jnp.concatenate([jnp.zeros((bs, 1, POOL_WIDTH), F32), state_pool[j]], axis=1)
            mix_s = _pool_mix(u_s3, prev16, wbd, psc, ls, past_len).reshape(ts_, 768)
            outs["pool_s"].append(jnp.concatenate([state_pool[j], u_s3], axis=1)[:, -POOL_STATE:])
        wo_att = wo[768:].astype(BF16)
        att_p = _mem_attn(xqn_p.reshape(bp, lp, 256), mkv_p, tm).reshape(tp_, 256)
        att_s = _mem_attn(xqn_s.reshape(bs, ls, 256), mkv_s, ls).reshape(ts_, 256)
        yp, hp = _outproj(yp, mix_p, att_p, wo_mix, wo_att, g_ffn[i][None], tm)
        ys, _ = _outproj(ys, mix_s, att_s, wo_mix, wo_att, g_ffn[i][None], ts_)
        wu, wd = w_up[i].astype(BF16), w_down[i].astype(BF16)
        yp, c_p = _ffn_prompt(yp, hp, wu, conv_w[i], conv_b[i][None], wd, tm, lp)
        ys, c_s = _ffn_sample(ys, g_ffn[i][None], wu, conv_w[i], conv_b[i][None], wd, state_ffn_conv[i])
        outs["conv_p"].append(c_p)
        outs["conv_s"].append(c_s)
    st = lambda k: jnp.stack(outs[k])
    return (yp.reshape(bp, lp, D_MODEL), ys.reshape(bs, ls, D_MODEL), st("kv_p"), st("kv_s"), st("win_p"),
            st("win_s"), st("mem_p"), st("pool_p"), st("pool_s"), st("conv_p"), st("conv_s"))
```

```python
import functools
import math

import numpy as np
import jax
import jax.numpy as jnp
from jax import lax
from jax.experimental import pallas as pl
from jax.experimental.pallas import tpu as pltpu

F32 = jnp.float32
BF16 = jnp.bfloat16

D_MODEL = 1024
PAGE_SIZE = 128
HEAD_DIM = 64
NSA_HEADS = 12
NSA_KV_HEADS = 2
NSA_GROUP = NSA_HEADS // NSA_KV_HEADS
NSA_WIDTH = NSA_HEADS * HEAD_DIM
NSA_KV_WIDTH = NSA_KV_HEADS * HEAD_DIM
CMP_BLOCK = 64
CMP_HIDDEN = 128
SLC_TOPN = 16
WINDOW = 512
FORCED_SCORE = 1e4
XATTN_HEADS = 4
XATTN_WIDTH = XATTN_HEADS * HEAD_DIM
POOL_WINDOWS = (2, 4, 8, 16)
POOL_GROUP_DIM = 192
POOL_WIDTH = len(POOL_WINDOWS) * POOL_GROUP_DIM
POOL_STATE = max(POOL_WINDOWS) - 1
D_FF = 2816
REL_BUCKETS = 32
REL_MAX_DIST = 128
EPS = 1e-6
BIG = 1e30

LANES = 128
WIN_KEYS = WINDOW + 2 * 64
NEAR_KEYS = 4 * 64
PAD_ROWS = WIN_KEYS
NSA_IN_COLS = 1920
FF_CHUNK = D_FF // 2
VMEM_LIMIT = 56 * 1024 * 1024


def _cparams(*sem):
    return pltpu.CompilerParams(dimension_semantics=sem, vmem_limit_bytes=VMEM_LIMIT)


def _nt(a, b):
    return lax.dot_general(a, b, (((1,), (1,)), ((), ())), preferred_element_type=F32)


def _tn(a, b):
    return lax.dot_general(a, b, (((0,), (0,)), ((), ())), preferred_element_type=F32)


def _dot(a, b):
    return jnp.dot(a, b, preferred_element_type=F32)


def _rms_rows(x, g):
    return x * lax.rsqrt(jnp.mean(x * x, axis=-1, keepdims=True) + EPS) * g


def _seg_rms(zc, seg, gain):
    zz = zc * zc
    hi = zz.astype(BF16)
    lo = (zz - hi.astype(F32)).astype(BF16)
    ms = (_dot(hi, seg) + _dot(lo, seg)) * (1.0 / HEAD_DIM)
    return zc * lax.rsqrt(ms + EPS) * gain


def _nsa_inproj_kernel(x_ref, g_ref, w_ref, seg_ref, qg_ref, kg_ref, xg_ref,
                       qn_ref, rows_ref, win_ref, katt_ref, xqn_ref, gates_ref):
    xn = _rms_rows(x_ref[...], g_ref[...])
    z = _dot(xn.astype(BF16), w_ref[...])
    seg = seg_ref[...]
    for h in range(6):
        qn_ref[:, h * LANES:(h + 1) * LANES] = _seg_rms(z[:, h * LANES:(h + 1) * LANES], seg, qg_ref[...]).astype(BF16)
    kv = 768
    rows_ref[:, 0:256] = z[:, kv:kv + 256]
    ks = _seg_rms(z[:, kv + 256:kv + 384], seg, kg_ref[...])
    vs = z[:, kv + 384:kv + 512]
    kw = _seg_rms(z[:, kv + 512:kv + 640], seg, kg_ref[...])
    vw = z[:, kv + 640:kv + 768]
    rows_ref[:, 256:384] = ks
    rows_ref[:, 384:512] = vs
    win_ref[:, 0:128] = kw
    win_ref[:, 128:256] = vw
    katt_ref[:, 0:128] = ks.astype(BF16)
    katt_ref[:, 128:256] = vs.astype(BF16)
    katt_ref[:, 256:384] = kw.astype(BF16)
    katt_ref[:, 384:512] = vw.astype(BF16)
    for h in range(2):
        c0 = 1536 + h * LANES
        xqn_ref[:, h * LANES:(h + 1) * LANES] = _seg_rms(z[:, c0:c0 + LANES], seg, xg_ref[...]).astype(BF16)
    gates_ref[...] = jax.nn.sigmoid(z[:, 1792:1920])


def _nsa_inproj_t_kernel(x_ref, g_ref, w_ref, wt_ref, seg_ref, qg_ref, kg_ref, kgc_ref, xg_ref,
                         qn_ref, rowst_ref, wint_ref, katt_ref, xqn_ref, gates_ref):
    xn = _rms_rows(x_ref[...], g_ref[...]).astype(BF16)
    z = _dot(xn, w_ref[...])
    zt = _nt(wt_ref[...], xn)

    def norm_t(a):
        parts = []
        for gi in range(2):
            ag = a[gi * HEAD_DIM:(gi + 1) * HEAD_DIM]
            parts.append(ag * lax.rsqrt(jnp.mean(ag * ag, axis=0, keepdims=True) + EPS))
        return jnp.concatenate(parts, axis=0) * kgc_ref[...]

    rowst_ref[0, 0:256, :] = zt[0:256]
    rowst_ref[0, 256:384, :] = norm_t(zt[256:384])
    rowst_ref[0, 384:512, :] = zt[384:512]
    wint_ref[0, 0:128, :] = norm_t(zt[512:640])
    wint_ref[0, 128:256, :] = zt[640:768]
    seg = seg_ref[...]
    for h in range(6):
        qn_ref[:, h * LANES:(h + 1) * LANES] = _seg_rms(z[:, h * LANES:(h + 1) * LANES], seg, qg_ref[...]).astype(BF16)
    kv = 768
    katt_ref[:, 0:128] = _seg_rms(z[:, kv + 256:kv + 384], seg, kg_ref[...]).astype(BF16)
    katt_ref[:, 128:256] = z[:, kv + 384:kv + 512].astype(BF16)
    katt_ref[:, 256:384] = _seg_rms(z[:, kv + 512:kv + 640], seg, kg_ref[...]).astype(BF16)
    katt_ref[:, 384:512] = z[:, kv + 640:kv + 768].astype(BF16)
    for h in range(2):
        c0 = 1536 + h * LANES
        xqn_ref[:, h * LANES:(h + 1) * LANES] = _seg_rms(z[:, c0:c0 + LANES], seg, xg_ref[...]).astype(BF16)
    gates_ref[...] = jax.nn.sigmoid(z[:, 1792:1920])


def _nsa_inproj_t(x, g, w, wt, seg, qg, kg, kgc, xg, tm, seq_len):
    t = x.shape[0]
    tps = seq_len // tm
    full = lambda a: pl.BlockSpec(a.shape, lambda i: (0,) * a.ndim)
    row = lambda n: pl.BlockSpec((tm, n), lambda i: (i, 0))
    col = lambda n: pl.BlockSpec((1, n, tm), lambda i: (i // tps, 0, i % tps))
    return pl.pallas_call(
        _nsa_inproj_t_kernel,
        grid=(t // tm,),
        in_specs=[row(D_MODEL), full(g), full(w), full(wt), full(seg), full(qg), full(kg), full(kgc), full(xg)],
        out_specs=[row(768), col(512), col(256), row(512), row(256), row(128)],
        out_shape=[jax.ShapeDtypeStruct((t, 768), BF16), jax.ShapeDtypeStruct((t // seq_len, 512, seq_len), F32),
                   jax.ShapeDtypeStruct((t // seq_len, 256, seq_len), F32), jax.ShapeDtypeStruct((t, 512), BF16),
                   jax.ShapeDtypeStruct((t, 256), BF16), jax.ShapeDtypeStruct((t, 128), F32)],
        compiler_params=_cparams("parallel"),
    )(x, g, w, wt, seg, qg, kg, kgc, xg)


def _nsa_inproj(x, g, w, seg, qg, kg, xg, tm):
    t = x.shape[0]
    full = lambda a: pl.BlockSpec(a.shape, lambda i: (0,) * a.ndim)
    row = lambda n: pl.BlockSpec((tm, n), lambda i: (i, 0))
    return pl.pallas_call(
        _nsa_inproj_kernel,
        grid=(t // tm,),
        in_specs=[row(D_MODEL), full(g), full(w), full(seg), full(qg), full(kg), full(xg)],
        out_specs=[row(768), row(512), row(256), row(512), row(256), row(128)],
        out_shape=[jax.ShapeDtypeStruct((t, 768), BF16), jax.ShapeDtypeStruct((t, 512), F32),
                   jax.ShapeDtypeStruct((t, 256), F32), jax.ShapeDtypeStruct((t, 512), BF16),
                   jax.ShapeDtypeStruct((t, 256), BF16), jax.ShapeDtypeStruct((t, 128), F32)],
        compiler_params=_cparams("parallel"),
    )(x, g, w, seg, qg, kg, xg)


def _pool_inproj_kernel(x_ref, g_ref, w_ref, seg_ref, xg_ref, u_ref, xqn_ref):
    xn = _rms_rows(x_ref[...], g_ref[...])
    z = _dot(xn.astype(BF16), w_ref[...])
    u_ref[...] = z[:, 0:POOL_WIDTH]
    seg = seg_ref[...]
    for h in range(2):
        c0 = POOL_WIDTH + h * LANES
        xqn_ref[:, h * LANES:(h + 1) * LANES] = _seg_rms(z[:, c0:c0 + LANES], seg, xg_ref[...]).astype(BF16)


def _pool_inproj(x, g, w, seg, xg, tm):
    t = x.shape[0]
    full = lambda a: pl.BlockSpec(a.shape, lambda i: (0,) * a.ndim)
    row = lambda n: pl.BlockSpec((tm, n), lambda i: (i, 0))
    return pl.pallas_call(
        _pool_inproj_kernel,
        grid=(t // tm,),
        in_specs=[row(D_MODEL), full(g), full(w), full(seg), full(xg)],
        out_specs=[row(POOL_WIDTH), row(256)],
        out_shape=[jax.ShapeDtypeStruct((t, POOL_WIDTH), F32), jax.ShapeDtypeStruct((t, 256), BF16)],
        compiler_params=_cparams("parallel"),
    )(x, g, w, seg, xg)


def _memkv_kernel(x_ref, g_ref, w_ref, seg_ref, kg_ref, o_ref):
    xn = _rms_rows(x_ref[...], g_ref[...])
    z = _dot(xn.astype(BF16), w_ref[...])
    seg = seg_ref[...]
    for h in range(2):
        o_ref[:, h * LANES:(h + 1) * LANES] = _seg_rms(z[:, h * LANES:(h + 1) * LANES], seg, kg_ref[...])
    o_ref[:, 256:512] = z[:, 256:512]


def _memkv(x, g, w, seg, kg, tm):
    t = x.shape[0]
    full = lambda a: pl.BlockSpec(a.shape, lambda i: (0,) * a.ndim)
    row = lambda n: pl.BlockSpec((tm, n), lambda i: (i, 0))
    return pl.pallas_call(
        _memkv_kernel,
        grid=(t // tm,),
        in_specs=[row(D_MODEL), full(g), full(w), full(seg), full(kg)],
        out_specs=row(512),
        out_shape=jax.ShapeDtypeStruct((t, 512), F32),
        compiler_params=_cparams("parallel"),
    )(x, g, w, seg, kg)


def _compress_kernel(x_ref, pos_ref, w1_ref, b1_ref, w2_ref, b2_ref, kg_ref, seg_ref, o_ref, acc_ref):
    lc = pl.program_id(1)

    @pl.when(lc == 0)
    def _():
        acc_ref[...] = jnp.zeros_like(acc_ref)

    for comp in range(2):
        part = None
        for l in range(8):
            xl = x_ref[:, l, comp * LANES:(comp + 1) * LANES] + pos_ref[comp, l]
            d = _dot(xl.astype(BF16), w1_ref[comp, l])
            part = d if part is None else part + d
        acc_ref[:, comp * 256:(comp + 1) * 256] += part

    @pl.when(lc == pl.num_programs(1) - 1)
    def _():
        for comp in range(2):
            h = jax.nn.gelu(acc_ref[:, comp * 256:(comp + 1) * 256] + b1_ref[comp])
            o = _dot(h.astype(BF16), w2_ref[comp]) + b2_ref[comp]
            if comp == 0:
                o = _seg_rms(o, seg_ref[...], kg_ref[...])
            o_ref[:, comp * LANES:(comp + 1) * LANES] = o


def _compress(x3, pos_t, w1bd, b1t, w2bd, b2t, kg, seg, nbt):
    nb, _, w = x3.shape
    nbt = min(nbt, nb)
    assert nb % nbt == 0
    full = lambda a: pl.BlockSpec(a.shape, lambda j, l: (0,) * a.ndim)
    return pl.pallas_call(
        _compress_kernel,
        grid=(nb // nbt, 8),
        in_specs=[pl.BlockSpec((nbt, 8, 256), lambda j, l: (j, l, 0)),
                  pl.BlockSpec((2, 8, 1, LANES), lambda j, l: (0, l, 0, 0)),
                  pl.BlockSpec((2, 8, LANES, 256), lambda j, l: (0, l, 0, 0)),
                  full(b1t), full(w2bd), full(b2t), full(kg), full(seg)],
        out_specs=pl.BlockSpec((nbt, 256), lambda j, l: (j, 0)),
        out_shape=jax.ShapeDtypeStruct((nb, 256), F32),
        scratch_shapes=[pltpu.VMEM((nbt, 512), F32)],
        compiler_params=_cparams("parallel", "arbitrary"),
    )(x3, pos_t, w1bd, b1t, w2bd, b2t, kg, seg)


def _compress_cache_kernel(x_ref, pos_ref, w1_ref, b1_ref, w2_ref, b2_ref, kg_ref, seg_ref, o_ref, acc_ref):
    dc = pl.program_id(1)

    @pl.when(dc == 0)
    def _():
        acc_ref[...] = jnp.zeros_like(acc_ref)

    pt = x_ref.shape[0]
    x2 = x_ref.reshape(pt * 32, LANES)

    def feature_rows(comp, g, dd):
        return (x2[pl.ds((comp * 2 + g) * 8 + dd, pt, stride=32), :] + pos_ref[comp, dd]).astype(BF16)

    for comp in range(2):
        for g in range(2):
            part = None
            for dp in range(4):
                xd = jnp.concatenate([feature_rows(comp, g, 2 * dp), feature_rows(comp, g, 2 * dp + 1)], axis=1)
                d = _dot(xd, w1_ref[comp, dp])
                part = d if part is None else part + d
            c0 = (comp * 2 + g) * 256
            acc_ref[:, c0:c0 + 256] += part

    @pl.when(dc == pl.num_programs(1) - 1)
    def _():
        for comp in range(2):
            hid = [jax.nn.gelu(acc_ref[:, (comp * 2 + g) * 256:(comp * 2 + g + 1) * 256] + b1_ref[comp]).astype(BF16)
                   for g in range(2)]
            for blk in range(2):
                o = _dot(hid[0], w2_ref[comp, blk, 0]) + _dot(hid[1], w2_ref[comp, blk, 1]) + b2_ref[comp]
                if comp == 0:
                    o = _seg_rms(o, seg_ref[...], kg_ref[...])
                o_ref[:, blk * 256 + comp * LANES:blk * 256 + (comp + 1) * LANES] = o


def _compress_cache(xt, pos_t, w1t, b1t, w2sel, b2t, kg, seg, pt):
    n_phys = xt.shape[0]
    pt = min(pt, n_phys)
    assert n_phys % pt == 0
    full = lambda a: pl.BlockSpec(a.shape, lambda j, d: (0,) * a.ndim)
    return pl.pallas_call(
        _compress_cache_kernel,
        grid=(n_phys // pt, 8),
        in_specs=[pl.BlockSpec((pt, 2, 2, 8, LANES), lambda j, d: (j, 0, 0, d, 0)),
                  pl.BlockSpec((2, 8, 1, LANES), lambda j, d: (0, d, 0, 0)),
                  pl.BlockSpec((2, 4, 2 * LANES, 256), lambda j, d: (0, d, 0, 0)),
                  full(b1t), full(w2sel), full(b2t), full(kg), full(seg)],
        out_specs=pl.BlockSpec((pt, 512), lambda j, d: (j, 0)),
        out_shape=jax.ShapeDtypeStruct((n_phys, 512), F32),
        scratch_shapes=[pltpu.VMEM((pt, 1024), F32)],
        compiler_params=_cparams("parallel", "arbitrary"),
    )(xt, pos_t, w1t, b1t, w2sel, b2t, kg, seg)


def _bucket_np(d):
    n = np.maximum(d, 0)
    max_exact = REL_BUCKETS // 2
    nf = np.maximum(n, 1).astype(np.float32)
    large = max_exact + (np.log(nf / max_exact) / math.log(REL_MAX_DIST / max_exact)
                         * (REL_BUCKETS - max_exact)).astype(np.int32)
    large = np.minimum(large, REL_BUCKETS - 1)
    return np.where(n < max_exact, n, large).astype(np.int32)


def _bias_table_kernel(rb_ref, code_ref, o_ref):
    code = code_ref[...]
    for h in range(NSA_HEADS):
        far = rb_ref[REL_BUCKETS - 1, h]
        acc = jnp.full(code.shape, -BIG, F32)
        for k in range(REL_BUCKETS):
            acc = jnp.where(code == k, rb_ref[k, h] - far, acc)
        o_ref[h] = acc


def _bias_table(rel_bias, code):
    code = jnp.asarray(code, jnp.int32)
    return pl.pallas_call(
        _bias_table_kernel,
        in_specs=[pl.BlockSpec(memory_space=pltpu.SMEM), pl.BlockSpec(code.shape, lambda: (0, 0))],
        out_specs=pl.BlockSpec((NSA_HEADS,) + code.shape, lambda: (0, 0, 0)),
        out_shape=jax.ShapeDtypeStruct((NSA_HEADS,) + code.shape, F32),
    )(rel_bias, code)


def _window_codes():
    ql = np.arange(64)[:, None]
    j = np.arange(WIN_KEYS)[None, :]
    d = WIN_KEYS - 64 + ql - j
    return np.where((d >= 0) & (d <= WINDOW), _bucket_np(d), -1).astype(np.int32)


def _cmp_codes():
    delta = np.arange(8)[:, None]
    ql = np.arange(64)[None, :]
    d = delta * CMP_BLOCK + ql - (CMP_BLOCK - 1)
    return np.where(d >= 0, _bucket_np(d), -1).astype(np.int32)


FAR_TILE = 512


def _nsa_prompt_kernel(qn_ref, gates_ref, cmp_ref, kv_ref, oh_ref, wb_ref, nb_ref, ct_ref, amat_ref, o_ref):
    i = pl.program_id(1)
    nc = cmp_ref.shape[1]
    gr = NSA_GROUP * 64
    rows = 2 * gr
    q = qn_ref[0].astype(F32)
    gts = gates_ref[0]
    kc = cmp_ref[0, :, 0:128].astype(BF16)
    vc = cmp_ref[0, :, 128:256].astype(BF16)
    lane = lax.broadcasted_iota(jnp.int32, (64, LANES), 1)
    half = [lane < 64, lane >= 64]
    qall = jnp.concatenate([jnp.where(half[g], q[:, h * LANES:(h + 1) * LANES], 0.0)
                            for g in range(2) for h in range(NSA_GROUP)], axis=0).astype(BF16)

    blk_r = lax.broadcasted_iota(jnp.int32, (nc, rows), 0)
    ct = ct_ref[...]
    bias = jnp.where(blk_r == i, ct[0:1], jnp.where(blk_r == i - 1, ct[1:2], jnp.where(
        blk_r == i - 2, ct[2:3], jnp.where(blk_r > i, -BIG, 0.0))))
    lc = _nt(kc, qall) + bias
    e = jnp.exp(lc - jnp.max(lc, axis=0, keepdims=True))
    p = jnp.where(bias > -0.5 * BIG, e / jnp.sum(e, axis=0, keepdims=True), 0.0)
    o_c = _tn(p.astype(BF16), vc)
    imp = []
    for g in range(2):
        s3 = p[:, g * gr:g * gr + 128] + p[:, g * gr + 128:g * gr + 256] + p[:, g * gr + 256:g * gr + 384]
        imp.append(s3 + pltpu.roll(s3, 64, axis=1))
    lane_c = lax.broadcasted_iota(jnp.int32, (nc, LANES), 1)
    blk = lax.broadcasted_iota(jnp.int32, (nc, LANES), 0)
    impp = jnp.where(lane_c < 64, imp[0], imp[1])
    forced = (blk == 0) | (blk == i) | (blk == i - 1)
    score = jnp.where(blk <= i, jnp.where(forced, FORCED_SCORE, impp), -1.0)
    sc8 = [score[r * 8:(r + 1) * 8] for r in range(nc // 8)]
    blk8 = lax.broadcasted_iota(jnp.int32, (8, LANES), 0)
    rank8 = [jnp.zeros((8, LANES), F32) for _ in sc8]
    for cp in range(nc):
        row = score[cp:cp + 1, :]
        for r in range(nc // 8):
            if r * 8 > cp:
                beats = row >= sc8[r]
            elif r * 8 + 7 < cp:
                beats = row > sc8[r]
            else:
                beats = (row > sc8[r]) | ((row == sc8[r]) & (blk8 + r * 8 > cp))
            rank8[r] = rank8[r] + jnp.where(beats, 1.0, 0.0)
    rank = jnp.concatenate(rank8, axis=0)
    notsel =jnp.where((rank < float(min(SLC_TOPN, nc))) & (score >= 0.0), 0.0, 1.0)
    notsel_far = jnp.where(blk > i - 3, 1.0, notsel)
    tail = jnp.where(lax.broadcasted_iota(jnp.int32, (LANES - nc, LANES), 0) == 64 - nc, 1.0, 0.0)
    ns_near = jnp.concatenate([notsel, tail], axis=0).astype(BF16)
    ns_far = jnp.concatenate([notsel_far, tail], axis=0).astype(BF16)
    a = amat_ref[...]
    lhs_near = jnp.concatenate([qall, (_nt(a, ns_near) * -BIG).astype(BF16)], axis=1)
    lhs_far = jnp.concatenate([qall, (_nt(a, ns_far) * -BIG).astype(BF16)], axis=1)
    lane_r = lax.broadcasted_iota(jnp.int32, (rows, LANES), 1)
    lhs_win = jnp.concatenate([qall, jnp.where(lane_r == 64, -BIG, 0.0).astype(BF16)], axis=1)

    def keys_aug(r0, n, c0):
        return jnp.concatenate([kv_ref[0, pl.ds(r0, n), c0:c0 + LANES], oh_ref[pl.ds(r0, n), :]], axis=1)

    def fold(x, op):
        return functools.reduce(op, [x[:, c * LANES:(c + 1) * LANES] for c in range(x.shape[1] // LANES)])

    ntiles = (i + 5) // (FAR_TILE // 64)
    r0n = pl.multiple_of(PAD_ROWS + (i - 3) * 64, 64)
    s_near = _nt(lhs_near, keys_aug(r0n, NEAR_KEYS, 0)) + nb_ref[...]

    def tile_row(t):
        return pl.multiple_of(PAD_ROWS + t * FAR_TILE, LANES)

    def far_logits(t):
        return _nt(lhs_far, keys_aug(tile_row(t), FAR_TILE, 0))

    def absorb(s, v, m, lrun, acc):
        m_new = jnp.maximum(m, jnp.max(fold(s, jnp.maximum), axis=1, keepdims=True))
        alpha = jnp.exp(m - m_new)
        e = jnp.exp(s - m_new)
        return m_new, alpha * lrun + fold(e, jnp.add), alpha * acc + _dot(e.astype(BF16), v)

    assert ((nc + 4) // (FAR_TILE // 64)) % 2 == 0

    def far_pair(u, carry):
        sa, sb = far_logits(2 * u), far_logits(2 * u + 1)
        carry = absorb(sa, kv_ref[0, pl.ds(tile_row(2 * u), FAR_TILE), 128:256], *carry)
        return absorb(sb, kv_ref[0, pl.ds(tile_row(2 * u + 1), FAR_TILE), 128:256], *carry)

    zero = jnp.zeros((rows, LANES), F32)
    carry = lax.fori_loop(0, (ntiles + 1) // 2, far_pair, (jnp.full((rows, 1), -BIG, F32), zero, zero))
    _, lrun, acc = absorb(s_near, kv_ref[0, pl.ds(r0n, NEAR_KEYS), 128:256], *carry)
    o_s = acc / jnp.sum(lrun, axis=1, keepdims=True)

    r0w = pl.multiple_of(PAD_ROWS + (i - 9) * 64, 64)
    s = _nt(lhs_win, keys_aug(r0w, WIN_KEYS, 256)) + wb_ref[...]
    e = jnp.exp(s - jnp.max(fold(s, jnp.maximum), axis=1, keepdims=True))
    o_w = _dot(e.astype(BF16), kv_ref[0, pl.ds(r0w, WIN_KEYS), 384:512]) / jnp.sum(fold(e, jnp.add), axis=1,
                                                                                 keepdims=True)

    low = lax.broadcasted_iota(jnp.int32, (64, LANES), 1) < 64
    for h in range(NSA_GROUP):
        per_group = []
        for g in range(2):
            c0 = (g * NSA_GROUP + h) * 3
            r = slice(g * gr + h * 64, g * gr + (h + 1) * 64)
            per_group.append(gts[:, c0:c0 + 1] * o_c[r] + gts[:, c0 + 1:c0 + 2] * o_s[r]
                             + gts[:, c0 + 2:c0 + 3] * o_w[r])
        o_ref[0, :, h * LANES:(h + 1) * LANES] = jnp.where(low, per_group[0], per_group[1]).astype(BF16)


def _nsa_prompt_attn(qn, gates, cmp, kvatt, oh, wb, nb, ct, amat):
    b, l, _ = qn.shape
    nc = l // 64
    full = lambda a: pl.BlockSpec(a.shape, lambda bi, i: (0,) * a.ndim)
    return pl.pallas_call(
        _nsa_prompt_kernel,
        grid=(b, nc),
        in_specs=[pl.BlockSpec((1, 64, 768), lambda bi, i: (bi, i, 0)),
                  pl.BlockSpec((1, 64, 128), lambda bi, i: (bi, i, 0)),
                  pl.BlockSpec((1, nc, 256), lambda bi, i: (bi, 0, 0)),
                  pl.BlockSpec((1, PAD_ROWS + l, 512), lambda bi, i: (bi, 0, 0)),
                  full(oh), full(wb), full(nb), full(ct), full(amat)],
        out_specs=pl.BlockSpec((1, 64, 768), lambda bi, i: (bi, i, 0)),
        out_shape=jax.ShapeDtypeStruct((b, l, 768), BF16),
        compiler_params=_cparams("parallel", "arbitrary"),
    )(qn, gates, cmp, kvatt, oh, wb, nb, ct, amat)


def _mem_attn_kernel(q_ref, kv_ref, o_ref):
    q = q_ref[0].astype(F32)
    k = kv_ref[0, :, 0:256].astype(BF16)
    v = kv_ref[0, :, 256:512].astype(BF16)
    head = lax.broadcasted_iota(jnp.int32, q.shape, 1) // HEAD_DIM
    out = jnp.zeros(q.shape, F32)
    for h in range(XATTN_HEADS):
        qh = jnp.where(head == h, q, 0.0).astype(BF16)
        s = _nt(qh, k)
        e = jnp.exp(s - jnp.max(s, axis=1, keepdims=True))
        p = e / jnp.sum(e, axis=1, keepdims=True)
        out = out + jnp.where(head == h, _dot(p.astype(BF16), v), 0.0)
    o_ref[0] = out.astype(BF16)


def _mem_attn(xqn, mkv, tq):
    b, l, _ = xqn.shape
    nm = mkv.shape[1]
    return pl.pallas_call(
        _mem_attn_kernel,
        grid=(b, l // tq),
        in_specs=[pl.BlockSpec((1, tq, 256), lambda bi, i: (bi, i, 0)),
                  pl.BlockSpec((1, nm, 512), lambda bi, i: (bi, 0, 0))],
        out_specs=pl.BlockSpec((1, tq, 256), lambda bi, i: (bi, i, 0)),
        out_shape=jax.ShapeDtypeStruct((b, l, 256), BF16),
        compiler_params=_cparams("parallel", "parallel"),
    )(xqn, mkv)


def _pool_kernel(u_ref, prev_ref, w_ref, scale_ref, o_ref, carry_ref, *, pos0):
    i = pl.program_id(1)
    tp = u_ref.shape[1]

    @pl.when(i == 0)
    def _():
        carry_ref[...] = prev_ref[0]

    u = u_ref[0]
    ext = jnp.concatenate([carry_ref[...], u], axis=0)
    carry_ref[...] = ext[tp:tp + 16]
    s2 = ext + pltpu.roll(ext, 1, axis=0)
    s4 = s2 + pltpu.roll(s2, 2, axis=0)
    s8 = s4 + pltpu.roll(s4, 4, axis=0)
    s16 = s8 + pltpu.roll(s8, 8, axis=0)
    lane = lax.broadcasted_iota(jnp.int32, (tp, POOL_WIDTH), 1)
    pos1 = (pos0 + 1 + i * tp + lax.broadcasted_iota(jnp.int32, (tp, POOL_WIDTH), 0)).astype(F32)
    grp = lane // POOL_GROUP_DIM
    ssum = jnp.where(grp == 0, s2[16:], jnp.where(grp == 1, s4[16:], jnp.where(grp == 2, s8[16:], s16[16:])))
    win = jnp.where(grp == 0, 2.0, jnp.where(grp == 1, 4.0, jnp.where(grp == 2, 8.0, 16.0)))
    pooled = ssum / jnp.minimum(win, pos1) - u
    o_ref[0] = (_dot(pooled.astype(BF16), w_ref[...]) * scale_ref[...]).astype(BF16)


def _pool_mix(u, prev16, wbd, scale, tp, pos0):
    b, l, _ = u.shape
    full = lambda a: pl.BlockSpec(a.shape, lambda bi, i: (0,) * a.ndim)
    return pl.pallas_call(
        functools.partial(_pool_kernel, pos0=pos0),
        grid=(b, l // tp),
        in_specs=[pl.BlockSpec((1, tp, POOL_WIDTH), lambda bi, i: (bi, i, 0)),
                  pl.BlockSpec((1, 16, POOL_WIDTH), lambda bi, i: (bi, 0, 0)),
                  full(wbd), full(scale)],
        out_specs=pl.BlockSpec((1, tp, POOL_WIDTH), lambda bi, i: (bi, i, 0)),
        out_shape=jax.ShapeDtypeStruct((b, l, POOL_WIDTH), BF16),
        scratch_shapes=[pltpu.VMEM((16, POOL_WIDTH), F32)],
        compiler_params=_cparams("parallel", "arbitrary"),
    )(u, prev16, wbd, scale)


def _outproj_kernel(y_ref, mix_ref, att_ref, wm_ref, wa_ref, g_ref, o_ref, h_ref):
    y = y_ref[...] + _dot(mix_ref[...], wm_ref[...]) + _dot(att_ref[...], wa_ref[...])
    o_ref[...] = y
    h_ref[...] = _rms_rows(y, g_ref[...]).astype(BF16)


def _outproj(y, mix, att, wm, wa, g_ffn, tm):
    t = y.shape[0]
    full = lambda a: pl.BlockSpec(a.shape, lambda i: (0,) * a.ndim)
    row = lambda n: pl.BlockSpec((tm, n), lambda i: (i, 0))
    return pl.pallas_call(
        _outproj_kernel,
        grid=(t // tm,),
        in_specs=[row(D_MODEL), row(768), row(256), full(wm), full(wa), full(g_ffn)],
        out_specs=[row(D_MODEL), row(D_MODEL)],
        out_shape=[jax.ShapeDtypeStruct((t, D_MODEL), F32), jax.ShapeDtypeStruct((t, D_MODEL), BF16)],
        compiler_params=_cparams("parallel"),
    )(y, mix, att, wm, wa, g_ffn)


def _ffn_half(h, w_ref, cw_ref, cb_ref, ext_ref, prev8, fix):
    tm = h.shape[0]
    u = _dot(h, w_ref[...])
    ext_ref[0:8, :] = prev8
    ext_ref[8:8 + tm, :] = u
    s1 = ext_ref[7:7 + tm, :]
    s2 = ext_ref[6:6 + tm, :]
    if fix is not None:
        rowm, p1, p2 = fix
        s1 = jnp.where(rowm >= 1, s1, p1)
        s2 = jnp.where(rowm >= 2, s2, p2)
    cw = cw_ref[...]
    return u, cb_ref[...] + cw[0:1] * s2 + cw[1:2] * s1 + cw[2:3] * u


FF_SUB = (0, 1024, 2048, D_FF)


def _mix_ffn_kernel(y_ref, mix_ref, att_ref, wm_ref, wo_ref, g_ref, cw_ref, cb_ref, wup_hbm, wd_hbm,
                    o_ref, sa_ref, sv_ref, carry_ref, ext_ref, wup_ref, wd_ref, sem, *, tiles_per_seq):
    j = pl.program_id(0)
    tm = y_ref.shape[0]

    @pl.when(j == 0)
    def _():
        up = pltpu.make_async_copy(wup_hbm, wup_ref, sem.at[0])
        down = pltpu.make_async_copy(wd_hbm, wd_ref, sem.at[1])
        up.start()
        down.start()
        up.wait()
        down.wait()

    @pl.when((j % tiles_per_seq) == 0)
    def _():
        carry_ref[...] = jnp.zeros_like(carry_ref)

    y1 = y_ref[...] + _dot(mix_ref[...], wm_ref[...]) + _dot(att_ref[...], wo_ref[...])
    h = _rms_rows(y1, g_ref[...]).astype(BF16)
    ffn = None
    for c0, c1 in zip(FF_SUB[:-1], FF_SUB[1:]):
        w = c1 - c0
        conv = []
        for half, st_ref in ((0, sa_ref), (1, sv_ref)):
            cols = slice(half * D_FF + c0, half * D_FF + c1)
            u = _dot(h, wup_ref[:, cols])
            ext_ref[half, 0:8, 0:w] = carry_ref[:, cols]
            ext_ref[half, 8:8 + tm, 0:w] = u
            s1 = ext_ref[half, 7:7 + tm, 0:w]
            s2 = ext_ref[half, 6:6 + tm, 0:w]
            cw = cw_ref[:, cols]
            conv.append(cb_ref[:, cols] + cw[0:1] * s2 + cw[1:2] * s1 + cw[2:3] * u)
            carry_ref[:, cols] = u[tm - 8:tm]
            st_ref[0, :, c0:c1] = u[tm - 8:tm]
        d = _dot((jax.nn.silu(conv[0]) * conv[1]).astype(BF16), wd_ref[c0:c1, :])
        ffn = d if ffn is None else ffn + d
    o_ref[...] = y1 + ffn


def _mix_ffn_prompt(y, mix, att, wm, wo, g, w_up, conv_w, conv_b, w_down, tm, seq_len):
    t = y.shape[0]
    nseq = t // seq_len
    tps = seq_len // tm
    full = lambda a: pl.BlockSpec(a.shape, lambda j: (0,) * a.ndim)
    row = lambda n: pl.BlockSpec((tm, n), lambda j: (j, 0))
    hbm = pl.BlockSpec(memory_space=pl.ANY)
    y_out, sa, sv = pl.pallas_call(
        functools.partial(_mix_ffn_kernel, tiles_per_seq=tps),
        grid=(t // tm,),
        in_specs=[row(D_MODEL), row(768), row(256), full(wm), full(wo), full(g), full(conv_w), full(conv_b),
                  hbm, hbm],
        out_specs=[row(D_MODEL), pl.BlockSpec((1, 8, D_FF), lambda j: (j, 0, 0)),
                   pl.BlockSpec((1, 8, D_FF), lambda j: (j, 0, 0))],
        out_shape=[jax.ShapeDtypeStruct((t, D_MODEL), F32), jax.ShapeDtypeStruct((t // tm, 8, D_FF), F32),
                   jax.ShapeDtypeStruct((t // tm, 8, D_FF), F32)],
        scratch_shapes=[pltpu.VMEM((8, 2 * D_FF), F32), pltpu.VMEM((2, tm + 8, max(b - a for a, b in zip(FF_SUB[:-1], FF_SUB[1:]))), F32),
                        pltpu.VMEM(w_up.shape, BF16), pltpu.VMEM(w_down.shape, BF16),
                        pltpu.SemaphoreType.DMA((2,))],
        compiler_params=_cparams("arbitrary"),
    )(y, mix, att, wm, wo, g, conv_w, conv_b, w_up, w_down)
    last = lambda s: s.reshape(nseq, tps, 8, D_FF)[:, tps - 1, 6:8]
    return y_out, jnp.concatenate([last(sa), last(sv)], axis=-1)


def _ffn_sample_kernel(y_ref, g_ref, wa_ref, wv_ref, cwa_ref, cwv_ref, cba_ref, cbv_ref, wd_ref,
                       p1a_ref, p1v_ref, p2a_ref, p2v_ref, o_ref, ua_ref, uv_ref, acc_ref, ext_ref):
    c = pl.program_id(0)
    h = _rms_rows(y_ref[...], g_ref[...]).astype(BF16)
    tm = h.shape[0]
    rowm = lax.broadcasted_iota(jnp.int32, (tm, wa_ref.shape[1]), 0) % 8
    zero8 = jnp.zeros((8, wa_ref.shape[1]), F32)
    ua, ya = _ffn_half(h, wa_ref, cwa_ref, cba_ref, ext_ref, zero8, (rowm, p1a_ref[...], p2a_ref[...]))
    uv, yv = _ffn_half(h, wv_ref, cwv_ref, cbv_ref, ext_ref, zero8, (rowm, p1v_ref[...], p2v_ref[...]))
    ua_ref[...] = ua
    uv_ref[...] = uv
    d = _dot((jax.nn.silu(ya) * yv).astype(BF16), wd_ref[...])

    @pl.when(c == 0)
    def _():
        acc_ref[...] = d

    @pl.when(c == pl.num_programs(0) - 1)
    def _():
        o_ref[...] = y_ref[...] + acc_ref[...] + d


def _ffn_sample(y, g, w_up, conv_w, conv_b, w_down, state):
    t = y.shape[0]
    nseq = t // 8
    fc = FF_CHUNK
    ncf = D_FF // fc
    z = jnp.zeros((nseq, 1, 2 * D_FF), F32)
    p1 = jnp.concatenate([state[:, 1:2]] + [z] * 7, axis=1).reshape(t, 2 * D_FF)
    p2 = jnp.concatenate([state[:, 0:1], state[:, 1:2]] + [z] * 6, axis=1).reshape(t, 2 * D_FF)
    full = lambda a: pl.BlockSpec(a.shape, lambda c: (0,) * a.ndim)
    ca = lambda r: pl.BlockSpec((r, fc), lambda c: (0, c))
    cv = lambda r: pl.BlockSpec((r, fc), lambda c: (0, ncf + c))
    y_out, ua, uv = pl.pallas_call(
        _ffn_sample_kernel,
        grid=(ncf,),
        in_specs=[full(y), full(g), ca(D_MODEL), cv(D_MODEL), ca(3), cv(3), ca(1), cv(1),
                  pl.BlockSpec((fc, D_MODEL), lambda c: (c, 0)), ca(t), cv(t), ca(t), cv(t)],
        out_specs=[pl.BlockSpec((t, D_MODEL), lambda c: (0, 0)), ca(t), ca(t)],
        out_shape=[jax.ShapeDtypeStruct((t, D_MODEL), F32), jax.ShapeDtypeStruct((t, D_FF), F32),
                   jax.ShapeDtypeStruct((t, D_FF), F32)],
        scratch_shapes=[pltpu.VMEM((t, D_MODEL), F32), pltpu.VMEM((t + 8, fc), F32)],
        compiler_params=_cparams("arbitrary"),
    )(y, g, w_up, w_up, conv_w, conv_w, conv_b, conv_b, w_down, p1, p1, p2, p2)
    u = jnp.concatenate([ua, uv], axis=-1).reshape(nseq, 8, 2 * D_FF)
    return y_out, u[:, 6:8]


KEY_CHUNK = 1024
NEW_PAD = 16


def _nsa_sample_kernel(pt_ref, qn_ref, gates_ref, rows_ref, winc_ref, winn_ref, scb_ref, sbt_ref, wbt_ref,
                       rmat_ref, eloc_ref, cmp_hbm, cache_hbm, o_ref, kvbuf, kcv, s_ref, m3_ref, sem, sem2):
    b = pl.program_id(0)
    npages = pt_ref.shape[1]
    ncs = 2 * npages
    nkeys = npages * PAGE_SIZE
    nch = nkeys // KEY_CHUNK
    ppc = KEY_CHUNK // PAGE_SIZE
    wbuf = winc_ref.shape[-1]

    slot = b % 2

    def page_copy(seq, p, sl):
        return pltpu.make_async_copy(cache_hbm.at[pt_ref[seq, p], pl.ds(2, 2)], kvbuf.at[sl, p], sem.at[sl])

    def cmp_copy(p):
        return pltpu.make_async_copy(cmp_hbm.at[pt_ref[b, p]], kcv.at[p], sem2)

    def start_cmp(p, c):
        cmp_copy(p).start()
        return c

    lax.fori_loop(0, npages, start_cmp, 0)

    @pl.when(b == 0)
    def _():
        def start_first(p, c):
            page_copy(0, p, 0).start()
            return c

        lax.fori_loop(0, npages, start_first, 0)

    @pl.when(b + 1 < pl.num_programs(0))
    def _():
        def start_next(p, c):
            page_copy(b + 1, p, 1 - slot).start()
            return c

        lax.fori_loop(0, npages, start_next, 0)

    q = qn_ref[0].astype(F32)
    gts = gates_ref[0]
    lane8 = lax.broadcasted_iota(jnp.int32, (8, LANES), 1)
    half = [lane8 < 64, lane8 >= 64]
    pieces = [jnp.where(half[g], q[:, h * LANES:(h + 1) * LANES], 0.0) for g in range(2) for h in range(NSA_GROUP)]
    qrows = jnp.concatenate(pieces + [jnp.zeros((32, LANES), F32)], axis=0).astype(BF16)

    def wait_cmp(p, c):
        cmp_copy(p).wait()
        return c

    lax.fori_loop(0, npages, wait_cmp, 0)

    kc = jnp.concatenate([kcv[:, 0, 0:128], kcv[:, 1, 0:128]], axis=0).astype(BF16)
    vc = jnp.concatenate([kcv[:, 0, 128:256], kcv[:, 1, 128:256]], axis=0).astype(BF16)
    lc = _nt(kc, qrows) + scb_ref[...]
    e = jnp.exp(lc - jnp.max(lc, axis=0, keepdims=True))
    p_c = e / jnp.sum(e, axis=0, keepdims=True)
    o_c = _tn(p_c.astype(BF16), vc)
    hi = p_c.astype(BF16)
    lo = (p_c - hi.astype(F32)).astype(BF16)
    imp = _dot(hi, rmat_ref[...]) + _dot(lo, rmat_ref[...])

    def blk_of(r):
        return jnp.where(r < npages, 2 * r, jnp.where(r < ncs, 2 * (r - npages) + 1, r))

    blk = blk_of(lax.broadcasted_iota(jnp.int32, (ncs + 8, LANES), 0))
    impx = jnp.concatenate([imp, jnp.zeros((8, LANES), F32)], axis=0)
    forced = (blk == 0) | (blk == ncs) | (blk == ncs - 1)
    score = jnp.where(blk <= ncs, jnp.where(forced, FORCED_SCORE, impx), -1.0)
    blkf = blk.astype(F32)
    work, picked = score, jnp.zeros((ncs + 8, LANES), F32)
    for _ in range(SLC_TOPN):
        top = jnp.max(work, axis=0, keepdims=True)
        first = jnp.min(jnp.where(work == top, blkf, float(2 * ncs + 16)), axis=0, keepdims=True)
        hit = blkf == first
        picked = jnp.where(hit, 1.0, picked)
        work = jnp.where(hit, -2.0, work)
    madd = jnp.where((picked > 0.0) & (score >= 0.0), 0.0, -BIG)
    eye = jnp.where(lax.broadcasted_iota(jnp.int32, (LANES, LANES), 0)
                    == lax.broadcasted_iota(jnp.int32, (LANES, LANES), 1), 1.0, 0.0).astype(BF16)
    hb = ppc
    for t in range(nch):
        mt = jnp.concatenate([madd[t * hb:(t + 1) * hb], madd[npages + t * hb:npages + (t + 1) * hb],
                              jnp.zeros((LANES - 2 * hb, LANES), F32)], axis=0).astype(BF16)
        m3_ref[t] = _nt(eye, mt).astype(BF16)

    def wait_page(p, c):
        page_copy(b, p, slot).wait()
        return c

    lax.fori_loop(0, npages, wait_page, 0)

    eloc = eloc_ref[...]
    pad8 = jnp.zeros((NEW_PAD - 8, LANES), F32)
    knew = jnp.concatenate([rows_ref[0, :, 256:384], pad8], axis=0).astype(BF16)
    vnew = jnp.concatenate([rows_ref[0, :, 384:512], pad8], axis=0).astype(BF16)
    kwn = jnp.concatenate([winn_ref[0, :, 0:128], pad8], axis=0).astype(BF16)
    vwn = jnp.concatenate([winn_ref[0, :, 128:256], pad8], axis=0).astype(BF16)
    kwt = winc_ref[0, 0].reshape(LANES, wbuf).astype(BF16)
    vwt = winc_ref[0, 1].reshape(LANES, wbuf).astype(BF16)
    near_pad = jnp.concatenate([jnp.zeros((LANES, KEY_CHUNK - PAGE_SIZE), F32), sbt_ref[:, 0:PAGE_SIZE]], axis=1)
    low = lax.broadcasted_iota(jnp.int32, (8, LANES), 1) < 64

    def page_rows(t, comp):
        tiles = [kvbuf[slot, t * ppc + pp, comp].reshape(LANES, PAGE_SIZE) for pp in range(ppc)]
        return jnp.concatenate(tiles, axis=1).astype(BF16)

    def s_body(t, m):
        lhs = jnp.concatenate([m3_ref[t], qrows], axis=1)
        s = _dot(lhs, jnp.concatenate([eloc, page_rows(t, 0)], axis=0))
        s = s + jnp.where(t == nch - 1, 1.0, 0.0) * near_pad
        s_ref[t] = s
        return jnp.maximum(m, jnp.max(s, axis=1, keepdims=True))

    s_new = _nt(qrows, knew) + sbt_ref[:, PAGE_SIZE:PAGE_SIZE + NEW_PAD]
    unroll = math.gcd(nch, 4)
    m = lax.fori_loop(0, nch, s_body, jnp.max(s_new, axis=1, keepdims=True), unroll=unroll)
    e_new = jnp.exp(s_new - m)

    def pv_body(t, carry):
        l, acc = carry
        et = jnp.exp(s_ref[t] - m)
        return l + jnp.sum(et, axis=1, keepdims=True), acc + _nt(et.astype(BF16), page_rows(t, 1))

    l, acc = lax.fori_loop(0, nch, pv_body, (jnp.sum(e_new, axis=1, keepdims=True),
                                             _dot(e_new.astype(BF16), vnew)), unroll=unroll)
    o_s = acc / l

    sw = _dot(qrows, kwt) + wbt_ref[:, 0:wbuf]
    swn = _nt(qrows, kwn) + wbt_ref[:, wbuf:wbuf + NEW_PAD]
    mw = jnp.maximum(jnp.max(sw, axis=1, keepdims=True), jnp.max(swn, axis=1, keepdims=True))
    ew = jnp.exp(sw - mw)
    ewn = jnp.exp(swn - mw)
    o_w = (_nt(ew.astype(BF16), vwt) + _dot(ewn.astype(BF16), vwn)) / (
        jnp.sum(ew, axis=1, keepdims=True) + jnp.sum(ewn, axis=1, keepdims=True))

    def gate_col(br):
        cols = [gts[:, hh * 3 + br:hh * 3 + br + 1] for hh in range(NSA_HEADS)]
        return jnp.concatenate(cols + [jnp.zeros((32, 1), F32)], axis=0)

    o = gate_col(0) * o_c + gate_col(1) * o_s + gate_col(2) * o_w
    for h in range(NSA_GROUP):
        r0, r1 = h * 8, (NSA_GROUP + h) * 8
        o_ref[0, :, h * LANES:(h + 1) * LANES] = jnp.where(low, o[r0:r0 + 8], o[r1:r1 + 8]).astype(BF16)


def _nsa_sample_attn(page_table, qn, gates, rows, winc, winn, scb, sbt, wbt, rmat, eloc, cmp_phys, cache):
    b, npages = page_table.shape
    ncs = 2 * npages
    nkeys = npages * PAGE_SIZE
    assert nkeys % KEY_CHUNK == 0
    nch = nkeys // KEY_CHUNK
    full = lambda a: pl.BlockSpec(a.shape, lambda bi, pt: (0,) * a.ndim)
    per_b = lambda a: pl.BlockSpec((1,) + a.shape[1:], lambda bi, pt: (bi,) + (0,) * (a.ndim - 1))
    grid_spec = pltpu.PrefetchScalarGridSpec(
        num_scalar_prefetch=1,
        grid=(b,),
        in_specs=[per_b(qn), per_b(gates), per_b(rows), per_b(winc), per_b(winn),
                  full(scb), full(sbt), full(wbt), full(rmat), full(eloc),
                  pl.BlockSpec(memory_space=pl.ANY), pl.BlockSpec(memory_space=pl.ANY)],
        out_specs=pl.BlockSpec((1, 8, 768), lambda bi, pt: (bi, 0, 0)),
        scratch_shapes=[pltpu.VMEM((2, npages, 2, 2, 64, PAGE_SIZE), F32), pltpu.VMEM((npages, 2, 256), F32),
                        pltpu.VMEM((nch, LANES, KEY_CHUNK), F32), pltpu.VMEM((nch, LANES, LANES), BF16),
                        pltpu.SemaphoreType.DMA((2,)), pltpu.SemaphoreType.DMA(())],
    )
    return pl.pallas_call(
        _nsa_sample_kernel,
        grid_spec=grid_spec,
        out_shape=jax.ShapeDtypeStruct((b, 8, 768), BF16),
        compiler_params=_cparams("arbitrary"),
    )(page_table, qn, gates, rows, winc, winn, scb, sbt, wbt, rmat, eloc, cmp_phys, cache)


def _block_diag2(w):
    z = jnp.zeros_like(w)
    return jnp.concatenate([jnp.concatenate([w, z], axis=-1), jnp.concatenate([z, w], axis=-1)], axis=-2)


def _q_slot_perm():
    return np.concatenate([np.r_[h * 64:(h + 1) * 64, (NSA_GROUP + h) * 64:(NSA_GROUP + h + 1) * 64]
                           for h in range(NSA_GROUP)])


def _sample_codes(ncs, wbuf):
    ql = np.arange(8)[None, :]
    c = np.arange(ncs)[:, None]
    scb = _bucket_np((ncs - c) * CMP_BLOCK + ql - (CMP_BLOCK - 1))
    kap = np.arange(PAGE_SIZE + NEW_PAD)[:, None]
    d = np.where(kap < PAGE_SIZE, PAGE_SIZE + ql - kap, ql - (kap - PAGE_SIZE))
    sbt = np.where((d >= 0) & (kap < PAGE_SIZE + 8), _bucket_np(d), -1)
    j = np.arange(wbuf + NEW_PAD)[:, None]
    d = wbuf + ql - j
    wbt = np.where((d >= 0) & (d <= WINDOW) & (j < wbuf + 8), _bucket_np(d), -1)
    return np.concatenate([scb, sbt, wbt], axis=0).astype(np.int32)


def kernel(x_prompt, x_sample, cache_nsa_kv, cache_nsa_win, cache_mem_kv, state_pool, state_ffn_conv,
           page_table, mem_prompt, rel_bias, g_mix, g_mem, g_ffn, w_in_nsa, q_gain_nsa, k_gain_nsa,
           cmp_pos, cmp_w1, cmp_b1, cmp_w2, cmp_b2, w_in_pool, w_pool_grp, pool_scale, w_mem_kv,
           xq_gain, xk_gain, w_out, w_up, conv_w, conv_b, w_down):
    bp, lp, _ = x_prompt.shape
    bs, ls, _ = x_sample.shape
    assert ls == 8 and lp % 512 == 0 and lp // 64 <= 64
    npages = page_table.shape[1]
    past_len = npages * PAGE_SIZE
    n_phys = cache_nsa_kv.shape[1]
    wbuf = cache_nsa_win.shape[2]
    nm = mem_prompt.shape[1]
    tp_, ts_ = bp * lp, bs * ls
    nc = lp // 64
    ncs = 2 * npages
    scale = HEAD_DIM ** -0.5
    tile2 = lambda v: jnp.tile(v, 2)[None, :]

    a = np.arange(LANES)
    seg = jnp.asarray(a[:, None] // 64 == a[None, :] // 64, BF16)
    perm = _q_slot_perm()

    wb = _bias_table(rel_bias, _window_codes()).reshape(NSA_HEADS * 64, WIN_KEYS)
    nbias = jnp.concatenate([jnp.full((NSA_HEADS * 64, 64), -BIG, F32), wb[:, WIN_KEYS - 192:]], axis=1)
    ct = _bias_table(rel_bias, _cmp_codes()).transpose(1, 0, 2).reshape(8, NSA_HEADS * 64)
    amat = np.zeros((2, NSA_GROUP * 64, LANES), np.float32)
    for g in range(2):
        for h in range(NSA_GROUP):
            amat[g, h * 64 + np.arange(64), g * 64 + np.arange(64)] = 1.0
    amat = jnp.asarray(amat.reshape(NSA_HEADS * 64, LANES), BF16)
    r = np.arange(PAD_ROWS + lp)
    oh_np = np.zeros((PAD_ROWS + lp, LANES), np.float32)
    oh_np[r, np.where(r < PAD_ROWS, 64, (r - PAD_ROWS) // 64)] = 1.0
    oh = jnp.asarray(oh_np, BF16)
    stab = _bias_table(rel_bias, _sample_codes(ncs, wbuf))
    stab = jnp.pad(stab.transpose(1, 0, 2).reshape(-1, NSA_HEADS * 8), ((0, 0), (0, LANES - NSA_HEADS * 8)))
    ns = PAGE_SIZE + NEW_PAD
    scb, sbt, wbt = stab[:ncs], stab[ncs:ncs + ns].T, stab[ncs + ns:].T
    scb = jnp.concatenate([scb[0::2], scb[1::2]], axis=0)
    ppc = KEY_CHUNK // PAGE_SIZE
    eloc_np = np.zeros((LANES, KEY_CHUNK), np.float32)
    for pp in range(ppc):
        eloc_np[pp, pp * PAGE_SIZE:pp * PAGE_SIZE + 64] = 1.0
        eloc_np[ppc + pp, pp * PAGE_SIZE + 64:(pp + 1) * PAGE_SIZE] = 1.0
    eloc = jnp.asarray(eloc_np, BF16)
    lam = np.arange(LANES)
    rm = (lam[:, None] < 96) & (lam[None, :] < 96) & (lam[:, None] // 48 == lam[None, :] // 48) \
        & (lam[:, None] % 8 == lam[None, :] % 8)
    rmat = jnp.asarray(rm, BF16)

    yp = x_prompt.reshape(tp_, D_MODEL)
    ys = x_sample.reshape(ts_, D_MODEL)
    tm = 512
    outs = {k: [] for k in ("kv_p", "kv_s", "win_p", "win_s", "mem_p", "pool_p", "pool_s", "conv_p", "conv_s")}
    depth = g_mix.shape[0]
    for i in range(depth):
        j = i // 2
        wo = w_out[i]
        xg = tile2(xq_gain[i]) * scale
        mkv_p = _memkv(mem_prompt.reshape(bp * nm, D_MODEL), g_mem[i][None], w_mem_kv[i].astype(BF16), seg,
                       tile2(xk_gain[i]), min(tm, bp * nm)).reshape(bp, nm, 2 * XATTN_WIDTH)
        outs["mem_p"].append(mkv_p.reshape(bp, nm, 2, XATTN_HEADS, HEAD_DIM))
        mkv_s = cache_mem_kv[i].reshape(bs, nm, 2 * XATTN_WIDTH)
        if i % 2 == 0:
            w = w_in_nsa[j]
            w = jnp.concatenate([w[:, perm], w[:, 768:1536], w[:, 1572:1828], w[:, 1536:1572],
                                 jnp.zeros((D_MODEL, NSA_IN_COLS - 1828), F32)], axis=1).astype(BF16)
            qg = tile2(q_gain_nsa[j]) * scale
            kg = tile2(k_gain_nsa[j])
            cmp_args = (jnp.tile(cmp_pos[j], (1, 1, 2))[:, :, None, :], _block_diag2(cmp_w1[j]).astype(BF16),
                        jnp.tile(cmp_b1[j], (1, 2))[:, None, :], _block_diag2(cmp_w2[j]).astype(BF16),
                        jnp.tile(cmp_b2[j], (1, 2))[:, None, :], kg, seg)
            wo_mix = wo[:768][perm].astype(BF16)
            w2 = cmp_w2[j]
            w2sel = jnp.zeros((2, 2, 2, 2 * CMP_HIDDEN, LANES), F32)
            for blk in range(2):
                for g in range(2):
                    w2sel = w2sel.at[:, blk, g, blk * CMP_HIDDEN:(blk + 1) * CMP_HIDDEN,
                                     g * HEAD_DIM:(g + 1) * HEAD_DIM].set(w2)
            cc_args = (jnp.tile(cmp_pos[j].transpose(0, 2, 1), (1, 1, 2))[:, :, None, :],
                       _block_diag2(cmp_w1[j].transpose(0, 2, 1, 3)).astype(BF16).reshape(2, 32, 2 * LANES, 256),
                       cmp_args[2], w2sel.astype(BF16), cmp_args[4], kg, seg, 512)
            wt = w_in_nsa[j][:, 768:1536].T.astype(BF16)
            qn, rows_t, win_t, katt, xqn_p, gates = _nsa_inproj_t(
                yp, g_mix[i][None], w, wt, seg, qg, kg, jnp.tile(k_gain_nsa[j], 2)[:, None], xg, tm, lp)
            pages_t = rows_t[:, :256].reshape(bp, 2, 2, HEAD_DIM, lp // PAGE_SIZE, PAGE_SIZE)
            pages_t = pages_t.transpose(0, 4, 1, 2, 3, 5).reshape(bp * lp // PAGE_SIZE, 2, 2, HEAD_DIM, PAGE_SIZE)
            cmp_p = _compress_cache(pages_t, *cc_args).reshape(bp, nc, 256)
            kvatt = jnp.pad(katt.reshape(bp, lp, 512), ((0, 0), (PAD_ROWS, 0), (0, 0)))
            mix_p = _nsa_prompt_attn(qn.reshape(bp, lp, 768), gates.reshape(bp, lp, 128), cmp_p, kvatt,
                                     oh, wb, nbias, ct, amat).reshape(tp_, 768)
            outs["kv_p"].append(rows_t.reshape(bp, 4, NSA_KV_HEADS, HEAD_DIM, lp).transpose(0, 4, 1, 2, 3))
            wlen = min(WINDOW, lp)
            outs["win_p"].append(win_t[:, :, lp - wlen:].reshape(bp, 2, NSA_KV_HEADS, HEAD_DIM, wlen)
                                 .transpose(0, 4, 1, 2, 3))
            qn, rows, win, _, xqn_s, gates = _nsa_inproj(ys, g_mix[i][None], w, seg, qg, kg, xg, ts_)
            cache_t = cache_nsa_kv[j].transpose(0, 2, 3, 4, 1)
            cmp_phys = _compress_cache(cache_t, *cc_args)
            winc = cache_nsa_win[j].transpose(0, 2, 3, 4, 1)
            mix_s = _nsa_sample_attn(page_table, qn.reshape(bs, ls, 768), gates.reshape(bs, ls, 128),
                                     rows.reshape(bs, ls, 512), winc, win.reshape(bs, ls, 256),
                                     scb, sbt, wbt, rmat, eloc, cmp_phys.reshape(n_phys, 2, 256),
                                     cache_t).reshape(ts_, 768)
            outs["kv_s"].append(rows.reshape(bs, ls, 4, NSA_KV_HEADS, HEAD_DIM))
            wall = jnp.concatenate([cache_nsa_win[j], win.reshape(bs, ls, 2, NSA_KV_HEADS, HEAD_DIM)], axis=1)
            outs["win_s"].append(wall[:, -wbuf:])
        else:
            w = w_in_pool[j].astype(BF16)
            wbd = jnp.zeros((POOL_WIDTH, POOL_WIDTH), F32)
            for gi in range(len(POOL_WINDOWS)):
                sl = slice(gi * POOL_GROUP_DIM, (gi + 1) * POOL_GROUP_DIM)
                wbd = wbd.at[sl, sl].set(w_pool_grp[j, gi])
            wbd = wbd.astype(BF16)
            psc = pool_scale[j][None]
            wo_mix = wo[:768].astype(BF16)
            u_p, xqn_p = _pool_inproj(yp, g_mix[i][None], w, seg, xg, tm)
            u_p3 = u_p.reshape(bp, lp, POOL_WIDTH)
            mix_p = _pool_mix(u_p3, jnp.zeros((bp, 16, POOL_WIDTH), F32), wbd, psc, tm, 0).reshape(tp_, 768)
            outs["pool_p"].append(u_p3[:, -POOL_STATE:])
            u_s, xqn_s = _pool_inproj(ys, g_mix[i][None], w, seg, xg, ts_)
            u_s3 = u_s.reshape(bs, ls, POOL_WIDTH)
            prev16 = jnp.concatenate([jnp.zeros((bs, 1, POOL_WIDTH), F32), state_pool[j]], axis=1)
            mix_s = _pool_mix(u_s3, prev16, wbd, psc, ls, past_len).reshape(ts_, 768)
            outs["pool_s"].append(jnp.concatenate([state_pool[j], u_s3], axis=1)[:, -POOL_STATE:])
        wo_att = wo[768:].astype(BF16)
        att_p = _mem_attn(xqn_p.reshape(bp, lp, 256), mkv_p, tm).reshape(tp_, 256)
        att_s = _mem_attn(xqn_s.reshape(bs, ls, 256), mkv_s, ls).reshape(ts_, 256)
        wu, wd = w_up[i].astype(BF16), w_down[i].astype(BF16)
        yp, c_p = _mix_ffn_prompt(yp, mix_p, att_p, wo_mix, wo_att, g_ffn[i][None], wu, conv_w[i],
                                  conv_b[i][None], wd, tm, lp)
        ys, _ = _outproj(ys, mix_s, att_s, wo_mix, wo_att, g_ffn[i][None], ts_)
        ys, c_s = _ffn_sample(ys, g_ffn[i][None], wu, conv_w[i], conv_b[i][None], wd, state_ffn_conv[i])
        outs["conv_p"].append(c_p)
        outs["conv_s"].append(c_s)
    st = lambda k: jnp.stack(outs[k])
    return (yp.reshape(bp, lp, D_MODEL), ys.reshape(bs, ls, D_MODEL), st("kv_p"), st("kv_s"), st("win_p"),
            st("win_s"), st("mem_p"), st("pool_p"), st("pool_s"), st("conv_p"), st("conv_s"))
```

```python
import functools
import math

import numpy as np
import jax
import jax.numpy as jnp
from jax import lax
from jax.experimental import pallas as pl
from jax.experimental.pallas import tpu as pltpu

F32 = jnp.float32
BF16 = jnp.bfloat16

D_MODEL = 1024
PAGE_SIZE = 128
HEAD_DIM = 64
NSA_HEADS = 12
NSA_KV_HEADS = 2
NSA_GROUP = NSA_HEADS // NSA_KV_HEADS
NSA_WIDTH = NSA_HEADS * HEAD_DIM
NSA_KV_WIDTH = NSA_KV_HEADS * HEAD_DIM
CMP_BLOCK = 64
CMP_HIDDEN = 128
SLC_TOPN = 16
WINDOW = 512
FORCED_SCORE = 1e4
XATTN_HEADS = 4
XATTN_WIDTH = XATTN_HEADS * HEAD_DIM
POOL_WINDOWS = (2, 4, 8, 16)
POOL_GROUP_DIM = 192
POOL_WIDTH = len(POOL_WINDOWS) * POOL_GROUP_DIM
POOL_STATE = max(POOL_WINDOWS) - 1
D_FF = 2816
REL_BUCKETS = 32
REL_MAX_DIST = 128
EPS = 1e-6
BIG = 1e30

LANES = 128
WIN_KEYS = WINDOW + 2 * 64
NEAR_KEYS = 4 * 64
PAD_ROWS = WIN_KEYS
NSA_IN_COLS = 1920
FF_CHUNK = D_FF // 2
VMEM_LIMIT = 56 * 1024 * 1024


def _cparams(*sem):
    return pltpu.CompilerParams(dimension_semantics=sem, vmem_limit_bytes=VMEM_LIMIT)


def _nt(a, b):
    return lax.dot_general(a, b, (((1,), (1,)), ((), ())), preferred_element_type=F32)


def _tn(a, b):
    return lax.dot_general(a, b, (((0,), (0,)), ((), ())), preferred_element_type=F32)


def _dot(a, b):
    return jnp.dot(a, b, preferred_element_type=F32)


def _rms_rows(x, g):
    return x * lax.rsqrt(jnp.mean(x * x, axis=-1, keepdims=True) + EPS) * g


def _seg_rms(zc, seg, gain):
    zz = zc * zc
    hi = zz.astype(BF16)
    lo = (zz - hi.astype(F32)).astype(BF16)
    ms = (_dot(hi, seg) + _dot(lo, seg)) * (1.0 / HEAD_DIM)
    return zc * lax.rsqrt(ms + EPS) * gain


def _nsa_inproj_kernel(x_ref, g_ref, w_ref, seg_ref, qg_ref, kg_ref, xg_ref,
                       qn_ref, rows_ref, win_ref, katt_ref, xqn_ref, gates_ref):
    xn = _rms_rows(x_ref[...], g_ref[...])
    z = _dot(xn.astype(BF16), w_ref[...])
    seg = seg_ref[...]
    for h in range(6):
        qn_ref[:, h * LANES:(h + 1) * LANES] = _seg_rms(z[:, h * LANES:(h + 1) * LANES], seg, qg_ref[...]).astype(BF16)
    kv = 768
    rows_ref[:, 0:256] = z[:, kv:kv + 256]
    ks = _seg_rms(z[:, kv + 256:kv + 384], seg, kg_ref[...])
    vs = z[:, kv + 384:kv + 512]
    kw = _seg_rms(z[:, kv + 512:kv + 640], seg, kg_ref[...])
    vw = z[:, kv + 640:kv + 768]
    rows_ref[:, 256:384] = ks
    rows_ref[:, 384:512] = vs
    win_ref[:, 0:128] = kw
    win_ref[:, 128:256] = vw
    katt_ref[:, 0:128] = ks.astype(BF16)
    katt_ref[:, 128:256] = vs.astype(BF16)
    katt_ref[:, 256:384] = kw.astype(BF16)
    katt_ref[:, 384:512] = vw.astype(BF16)
    for h in range(2):
        c0 = 1536 + h * LANES
        xqn_ref[:, h * LANES:(h + 1) * LANES] = _seg_rms(z[:, c0:c0 + LANES], seg, xg_ref[...]).astype(BF16)
    gates_ref[...] = jax.nn.sigmoid(z[:, 1792:1920])


def _nsa_inproj_t_kernel(x_ref, g_ref, w_ref, wt_ref, seg_ref, qg_ref, kg_ref, kgc_ref, xg_ref,
                         qn_ref, rowst_ref, wint_ref, katt_ref, xqn_ref, gates_ref):
    xn = _rms_rows(x_ref[...], g_ref[...]).astype(BF16)
    z = _dot(xn, w_ref[...])
    zt = _nt(wt_ref[...], xn)

    def norm_t(a):
        parts = []
        for gi in range(2):
            ag = a[gi * HEAD_DIM:(gi + 1) * HEAD_DIM]
            parts.append(ag * lax.rsqrt(jnp.mean(ag * ag, axis=0, keepdims=True) + EPS))
        return jnp.concatenate(parts, axis=0) * kgc_ref[...]

    rowst_ref[0, 0:256, :] = zt[0:256]
    rowst_ref[0, 256:384, :] = norm_t(zt[256:384])
    rowst_ref[0, 384:512, :] = zt[384:512]
    wint_ref[0, 0:128, :] = norm_t(zt[512:640])
    wint_ref[0, 128:256, :] = zt[640:768]
    seg = seg_ref[...]
    for h in range(6):
        qn_ref[:, h * LANES:(h + 1) * LANES] = _seg_rms(z[:, h * LANES:(h + 1) * LANES], seg, qg_ref[...]).astype(BF16)
    kv = 768
    katt_ref[:, 0:128] = _seg_rms(z[:, kv + 256:kv + 384], seg, kg_ref[...]).astype(BF16)
    katt_ref[:, 128:256] = z[:, kv + 384:kv + 512].astype(BF16)
    katt_ref[:, 256:384] = _seg_rms(z[:, kv + 512:kv + 640], seg, kg_ref[...]).astype(BF16)
    katt_ref[:, 384:512] = z[:, kv + 640:kv + 768].astype(BF16)
    for h in range(2):
        c0 = 1536 + h * LANES
        xqn_ref[:, h * LANES:(h + 1) * LANES] = _seg_rms(z[:, c0:c0 + LANES], seg, xg_ref[...]).astype(BF16)
    gates_ref[...] = jax.nn.sigmoid(z[:, 1792:1920])


def _nsa_inproj_t(x, g, w, wt, seg, qg, kg, kgc, xg, tm, seq_len):
    t = x.shape[0]
    tps = seq_len // tm
    full = lambda a: pl.BlockSpec(a.shape, lambda i: (0,) * a.ndim)
    row = lambda n: pl.BlockSpec((tm, n), lambda i: (i, 0))
    col = lambda n: pl.BlockSpec((1, n, tm), lambda i: (i // tps, 0, i % tps))
    return pl.pallas_call(
        _nsa_inproj_t_kernel,
        grid=(t // tm,),
        in_specs=[row(D_MODEL), full(g), full(w), full(wt), full(seg), full(qg), full(kg), full(kgc), full(xg)],
        out_specs=[row(768), col(512), col(256), row(512), row(256), row(128)],
        out_shape=[jax.ShapeDtypeStruct((t, 768), BF16), jax.ShapeDtypeStruct((t // seq_len, 512, seq_len), F32),
                   jax.ShapeDtypeStruct((t // seq_len, 256, seq_len), F32), jax.ShapeDtypeStruct((t, 512), BF16),
                   jax.ShapeDtypeStruct((t, 256), BF16), jax.ShapeDtypeStruct((t, 128), F32)],
        compiler_params=_cparams("parallel"),
    )(x, g, w, wt, seg, qg, kg, kgc, xg)


def _nsa_inproj(x, g, w, seg, qg, kg, xg, tm):
    t = x.shape[0]
    full = lambda a: pl.BlockSpec(a.shape, lambda i: (0,) * a.ndim)
    row = lambda n: pl.BlockSpec((tm, n), lambda i: (i, 0))
    return pl.pallas_call(
        _nsa_inproj_kernel,
        grid=(t // tm,),
        in_specs=[row(D_MODEL), full(g), full(w), full(seg), full(qg), full(kg), full(xg)],
        out_specs=[row(768), row(512), row(256), row(512), row(256), row(128)],
        out_shape=[jax.ShapeDtypeStruct((t, 768), BF16), jax.ShapeDtypeStruct((t, 512), F32),
                   jax.ShapeDtypeStruct((t, 256), F32), jax.ShapeDtypeStruct((t, 512), BF16),
                   jax.ShapeDtypeStruct((t, 256), BF16), jax.ShapeDtypeStruct((t, 128), F32)],
        compiler_params=_cparams("parallel"),
    )(x, g, w, seg, qg, kg, xg)


def _pool_inproj_kernel(x_ref, g_ref, w_ref, seg_ref, xg_ref, u_ref, xqn_ref):
    xn = _rms_rows(x_ref[...], g_ref[...])
    z = _dot(xn.astype(BF16), w_ref[...])
    u_ref[...] = z[:, 0:POOL_WIDTH]
    seg = seg_ref[...]
    for h in range(2):
        c0 = POOL_WIDTH + h * LANES
        xqn_ref[:, h * LANES:(h + 1) * LANES] = _seg_rms(z[:, c0:c0 + LANES], seg, xg_ref[...]).astype(BF16)


def _pool_inproj(x, g, w, seg, xg, tm):
    t = x.shape[0]
    full = lambda a: pl.BlockSpec(a.shape, lambda i: (0,) * a.ndim)
    row = lambda n: pl.BlockSpec((tm, n), lambda i: (i, 0))
    return pl.pallas_call(
        _pool_inproj_kernel,
        grid=(t // tm,),
        in_specs=[row(D_MODEL), full(g), full(w), full(seg), full(xg)],
        out_specs=[row(POOL_WIDTH), row(256)],
        out_shape=[jax.ShapeDtypeStruct((t, POOL_WIDTH), F32), jax.ShapeDtypeStruct((t, 256), BF16)],
        compiler_params=_cparams("parallel"),
    )(x, g, w, seg, xg)


def _memkv_kernel(x_ref, g_ref, w_ref, seg_ref, kg_ref, o_ref):
    xn = _rms_rows(x_ref[...], g_ref[...])
    z = _dot(xn.astype(BF16), w_ref[...])
    seg = seg_ref[...]
    for h in range(2):
        o_ref[:, h * LANES:(h + 1) * LANES] = _seg_rms(z[:, h * LANES:(h + 1) * LANES], seg, kg_ref[...])
    o_ref[:, 256:512] = z[:, 256:512]


def _memkv(x, g, w, seg, kg, tm):
    t = x.shape[0]
    full = lambda a: pl.BlockSpec(a.shape, lambda i: (0,) * a.ndim)
    row = lambda n: pl.BlockSpec((tm, n), lambda i: (i, 0))
    return pl.pallas_call(
        _memkv_kernel,
        grid=(t // tm,),
        in_specs=[row(D_MODEL), full(g), full(w), full(seg), full(kg)],
        out_specs=row(512),
        out_shape=jax.ShapeDtypeStruct((t, 512), F32),
        compiler_params=_cparams("parallel"),
    )(x, g, w, seg, kg)


def _compress_kernel(x_ref, pos_ref, w1_ref, b1_ref, w2_ref, b2_ref, kg_ref, seg_ref, o_ref, acc_ref):
    lc = pl.program_id(1)

    @pl.when(lc == 0)
    def _():
        acc_ref[...] = jnp.zeros_like(acc_ref)

    for comp in range(2):
        part = None
        for l in range(8):
            xl = x_ref[:, l, comp * LANES:(comp + 1) * LANES] + pos_ref[comp, l]
            d = _dot(xl.astype(BF16), w1_ref[comp, l])
            part = d if part is None else part + d
        acc_ref[:, comp * 256:(comp + 1) * 256] += part

    @pl.when(lc == pl.num_programs(1) - 1)
    def _():
        for comp in range(2):
            h = jax.nn.gelu(acc_ref[:, comp * 256:(comp + 1) * 256] + b1_ref[comp])
            o = _dot(h.astype(BF16), w2_ref[comp]) + b2_ref[comp]
            if comp == 0:
                o = _seg_rms(o, seg_ref[...], kg_ref[...])
            o_ref[:, comp * LANES:(comp + 1) * LANES] = o


def _compress(x3, pos_t, w1bd, b1t, w2bd, b2t, kg, seg, nbt):
    nb, _, w = x3.shape
    nbt = min(nbt, nb)
    assert nb % nbt == 0
    full = lambda a: pl.BlockSpec(a.shape, lambda j, l: (0,) * a.ndim)
    return pl.pallas_call(
        _compress_kernel,
        grid=(nb // nbt, 8),
        in_specs=[pl.BlockSpec((nbt, 8, 256), lambda j, l: (j, l, 0)),
                  pl.BlockSpec((2, 8, 1, LANES), lambda j, l: (0, l, 0, 0)),
                  pl.BlockSpec((2, 8, LANES, 256), lambda j, l: (0, l, 0, 0)),
                  full(b1t), full(w2bd), full(b2t), full(kg), full(seg)],
        out_specs=pl.BlockSpec((nbt, 256), lambda j, l: (j, 0)),
        out_shape=jax.ShapeDtypeStruct((nb, 256), F32),
        scratch_shapes=[pltpu.VMEM((nbt, 512), F32)],
        compiler_params=_cparams("parallel", "arbitrary"),
    )(x3, pos_t, w1bd, b1t, w2bd, b2t, kg, seg)


def _compress_cache_kernel(x_ref, pos_ref, w1_ref, b1_ref, w2_ref, b2_ref, kg_ref, seg_ref, o_ref, acc_ref):
    dc = pl.program_id(1)

    @pl.when(dc == 0)
    def _():
        acc_ref[...] = jnp.zeros_like(acc_ref)

    pt, nd = x_ref.shape[0], x_ref.shape[3]
    x2 = x_ref.reshape(pt * 4 * nd, LANES)

    def feature_rows(comp, g, dd):
        return (x2[pl.ds((comp * 2 + g) * nd + dd, pt, stride=4 * nd), :] + pos_ref[comp, dd]).astype(BF16)

    for comp in range(2):
        for g in range(2):
            part = None
            for dp in range(nd // 2):
                xd = jnp.concatenate([feature_rows(comp, g, 2 * dp), feature_rows(comp, g, 2 * dp + 1)], axis=1)
                d = _dot(xd, w1_ref[comp, dp])
                part = d if part is None else part + d
            c0 = (comp * 2 + g) * 256
            acc_ref[:, c0:c0 + 256] += part

    @pl.when(dc == pl.num_programs(1) - 1)
    def _():
        for comp in range(2):
            hid = [jax.nn.gelu(acc_ref[:, (comp * 2 + g) * 256:(comp * 2 + g + 1) * 256] + b1_ref[comp]).astype(BF16)
                   for g in range(2)]
            for blk in range(2):
                o = _dot(hid[0], w2_ref[comp, blk, 0]) + _dot(hid[1], w2_ref[comp, blk, 1]) + b2_ref[comp]
                if comp == 0:
                    o = _seg_rms(o, seg_ref[...], kg_ref[...])
                o_ref[:, blk * 256 + comp * LANES:blk * 256 + (comp + 1) * LANES] = o


CMP_FEATS = 16


def _compress_cache(xt, pos_t, w1t, b1t, w2sel, b2t, kg, seg, pt):
    n_phys = xt.shape[0]
    pt = min(pt, n_phys)
    assert n_phys % pt == 0
    full = lambda a: pl.BlockSpec(a.shape, lambda j, d: (0,) * a.ndim)
    return pl.pallas_call(
        _compress_cache_kernel,
        grid=(n_phys // pt, HEAD_DIM // CMP_FEATS),
        in_specs=[pl.BlockSpec((pt, 2, 2, CMP_FEATS, LANES), lambda j, d: (j, 0, 0, d, 0)),
                  pl.BlockSpec((2, CMP_FEATS, 1, LANES), lambda j, d: (0, d, 0, 0)),
                  pl.BlockSpec((2, CMP_FEATS // 2, 2 * LANES, 256), lambda j, d: (0, d, 0, 0)),
                  full(b1t), full(w2sel), full(b2t), full(kg), full(seg)],
        out_specs=pl.BlockSpec((pt, 512), lambda j, d: (j, 0)),
        out_shape=jax.ShapeDtypeStruct((n_phys, 512), F32),
        scratch_shapes=[pltpu.VMEM((pt, 1024), F32)],
        compiler_params=_cparams("parallel", "arbitrary"),
    )(xt, pos_t, w1t, b1t, w2sel, b2t, kg, seg)


def _bucket_np(d):
    n = np.maximum(d, 0)
    max_exact = REL_BUCKETS // 2
    nf = np.maximum(n, 1).astype(np.float32)
    large = max_exact + (np.log(nf / max_exact) / math.log(REL_MAX_DIST / max_exact)
                         * (REL_BUCKETS - max_exact)).astype(np.int32)
    large = np.minimum(large, REL_BUCKETS - 1)
    return np.where(n < max_exact, n, large).astype(np.int32)


def _bias_table_kernel(rb_ref, code_ref, o_ref):
    code = code_ref[...]
    for h in range(NSA_HEADS):
        far = rb_ref[REL_BUCKETS - 1, h]
        acc = jnp.full(code.shape, -BIG, F32)
        for k in range(REL_BUCKETS):
            acc = jnp.where(code == k, rb_ref[k, h] - far, acc)
        o_ref[h] = acc


def _bias_table(rel_bias, code):
    code = jnp.asarray(code, jnp.int32)
    return pl.pallas_call(
        _bias_table_kernel,
        in_specs=[pl.BlockSpec(memory_space=pltpu.SMEM), pl.BlockSpec(code.shape, lambda: (0, 0))],
        out_specs=pl.BlockSpec((NSA_HEADS,) + code.shape, lambda: (0, 0, 0)),
        out_shape=jax.ShapeDtypeStruct((NSA_HEADS,) + code.shape, F32),
    )(rel_bias, code)


def _window_codes():
    ql = np.arange(64)[:, None]
    j = np.arange(WIN_KEYS)[None, :]
    d = WIN_KEYS - 64 + ql - j
    return np.where((d >= 0) & (d <= WINDOW), _bucket_np(d), -1).astype(np.int32)


def _cmp_codes():
    delta = np.arange(8)[:, None]
    ql = np.arange(64)[None, :]
    d = delta * CMP_BLOCK + ql - (CMP_BLOCK - 1)
    return np.where(d >= 0, _bucket_np(d), -1).astype(np.int32)


FAR_TILE = 512


def _nsa_prompt_kernel(qn_ref, gates_ref, cmp_ref, kv_ref, oh_ref, wb_ref, nb_ref, ct_ref, amat_ref, o_ref):
    i = pl.program_id(1)
    nc = cmp_ref.shape[1]
    gr = NSA_GROUP * 64
    rows = 2 * gr
    q = qn_ref[0].astype(F32)
    gts = gates_ref[0]
    kc = cmp_ref[0, :, 0:128].astype(BF16)
    vc = cmp_ref[0, :, 128:256].astype(BF16)
    lane = lax.broadcasted_iota(jnp.int32, (64, LANES), 1)
    half = [lane < 64, lane >= 64]
    qall = jnp.concatenate([jnp.where(half[g], q[:, h * LANES:(h + 1) * LANES], 0.0)
                            for g in range(2) for h in range(NSA_GROUP)], axis=0).astype(BF16)

    blk_r = lax.broadcasted_iota(jnp.int32, (nc, rows), 0)
    ct = ct_ref[...]
    bias = jnp.where(blk_r == i, ct[0:1], jnp.where(blk_r == i - 1, ct[1:2], jnp.where(
        blk_r == i - 2, ct[2:3], jnp.where(blk_r > i, -BIG, 0.0))))
    lc = _nt(kc, qall) + bias
    e = jnp.exp(lc - jnp.max(lc, axis=0, keepdims=True))
    p = jnp.where(bias > -0.5 * BIG, e / jnp.sum(e, axis=0, keepdims=True), 0.0)
    o_c = _tn(p.astype(BF16), vc)
    imp = []
    for g in range(2):
        s3 = p[:, g * gr:g * gr + 128] + p[:, g * gr + 128:g * gr + 256] + p[:, g * gr + 256:g * gr + 384]
        imp.append(s3 + pltpu.roll(s3, 64, axis=1))
    lane_c = lax.broadcasted_iota(jnp.int32, (nc, LANES), 1)
    blk = lax.broadcasted_iota(jnp.int32, (nc, LANES), 0)
    impp = jnp.where(lane_c < 64, imp[0], imp[1])
    forced = (blk == 0) | (blk == i) | (blk == i - 1)
    score = jnp.where(blk <= i, jnp.where(forced, FORCED_SCORE, impp), -1.0)
    sc8 = [score[r * 8:(r + 1) * 8] for r in range(nc // 8)]
    blk8 = lax.broadcasted_iota(jnp.int32, (8, LANES), 0)
    rank8 = [jnp.zeros((8, LANES), F32) for _ in sc8]
    for cp in range(nc):
        row = score[cp:cp + 1, :]
        for r in range(nc // 8):
            if r * 8 > cp:
                beats = row >= sc8[r]
            elif r * 8 + 7 < cp:
                beats = row > sc8[r]
            else:
                beats = (row > sc8[r]) | ((row == sc8[r]) & (blk8 + r * 8 > cp))
            rank8[r] = rank8[r] + jnp.where(beats, 1.0, 0.0)
    rank = jnp.concatenate(rank8, axis=0)
    notsel =jnp.where((rank < float(min(SLC_TOPN, nc))) & (score >= 0.0), 0.0, 1.0)
    notsel_far = jnp.where(blk > i - 3, 1.0, notsel)
    tail = jnp.where(lax.broadcasted_iota(jnp.int32, (LANES - nc, LANES), 0) == 64 - nc, 1.0, 0.0)
    ns_near = jnp.concatenate([notsel, tail], axis=0).astype(BF16)
    ns_far = jnp.concatenate([notsel_far, tail], axis=0).astype(BF16)
    a = amat_ref[...]
    lhs_near = jnp.concatenate([qall, (_nt(a, ns_near) * -BIG).astype(BF16)], axis=1)
    lhs_far = jnp.concatenate([qall, (_nt(a, ns_far) * -BIG).astype(BF16)], axis=1)
    lane_r = lax.broadcasted_iota(jnp.int32, (rows, LANES), 1)
    lhs_win = jnp.concatenate([qall, jnp.where(lane_r == 64, -BIG, 0.0).astype(BF16)], axis=1)

    def keys_aug(r0, n, c0):
        return jnp.concatenate([kv_ref[0, pl.ds(r0, n), c0:c0 + LANES], oh_ref[pl.ds(r0, n), :]], axis=1)

    def fold(x, op):
        return functools.reduce(op, [x[:, c * LANES:(c + 1) * LANES] for c in range(x.shape[1] // LANES)])

    ntiles = (i + 5) // (FAR_TILE // 64)
    r0n = pl.multiple_of(PAD_ROWS + (i - 3) * 64, 64)
    s_near = _nt(lhs_near, keys_aug(r0n, NEAR_KEYS, 0)) + nb_ref[...]

    def tile_row(t):
        return pl.multiple_of(PAD_ROWS + t * FAR_TILE, LANES)

    def far_logits(t):
        return _nt(lhs_far, keys_aug(tile_row(t), FAR_TILE, 0))

    def absorb(s, v, m, lrun, acc):
        m_new = jnp.maximum(m, jnp.max(fold(s, jnp.maximum), axis=1, keepdims=True))
        alpha = jnp.exp(m - m_new)
        e = jnp.exp(s - m_new)
        return m_new, alpha * lrun + fold(e, jnp.add), alpha * acc + _dot(e.astype(BF16), v)

    def far_tile(t, carry):
        return absorb(far_logits(t), kv_ref[0, pl.ds(tile_row(t), FAR_TILE), 128:256], *carry)

    def far_pair(u, carry):
        sa, sb = far_logits(2 * u), far_logits(2 * u + 1)
        carry = absorb(sa, kv_ref[0, pl.ds(tile_row(2 * u), FAR_TILE), 128:256], *carry)
        return absorb(sb, kv_ref[0, pl.ds(tile_row(2 * u + 1), FAR_TILE), 128:256], *carry)

    zero = jnp.zeros((rows, LANES), F32)
    carry = lax.fori_loop(0, ntiles // 2, far_pair, (jnp.full((rows, 1), -BIG, F32), zero, zero))
    carry = lax.cond(ntiles % 2 == 1, lambda c: far_tile(ntiles - 1, c), lambda c: c, carry)
    _, lrun, acc = absorb(s_near, kv_ref[0, pl.ds(r0n, NEAR_KEYS), 128:256], *carry)
    o_s = acc / jnp.sum(lrun, axis=1, keepdims=True)

    r0w = pl.multiple_of(PAD_ROWS + (i - 9) * 64, 64)
    s = _nt(lhs_win, keys_aug(r0w, WIN_KEYS, 256)) + wb_ref[...]
    e = jnp.exp(s - jnp.max(fold(s, jnp.maximum), axis=1, keepdims=True))
    o_w = _dot(e.astype(BF16), kv_ref[0, pl.ds(r0w, WIN_KEYS), 384:512]) / jnp.sum(fold(e, jnp.add), axis=1,
                                                                                 keepdims=True)

    low = lax.broadcasted_iota(jnp.int32, (64, LANES), 1) < 64
    for h in range(NSA_GROUP):
        per_group = []
        for g in range(2):
            c0 = (g * NSA_GROUP + h) * 3
            r = slice(g * gr + h * 64, g * gr + (h + 1) * 64)
            per_group.append(gts[:, c0:c0 + 1] * o_c[r] + gts[:, c0 + 1:c0 + 2] * o_s[r]
                             + gts[:, c0 + 2:c0 + 3] * o_w[r])
        o_ref[0, :, h * LANES:(h + 1) * LANES] = jnp.where(low, per_group[0], per_group[1]).astype(BF16)


def _nsa_prompt_attn(qn, gates, cmp, kvatt, oh, wb, nb, ct, amat):
    b, l, _ = qn.shape
    nc = l // 64
    full = lambda a: pl.BlockSpec(a.shape, lambda bi, i: (0,) * a.ndim)
    return pl.pallas_call(
        _nsa_prompt_kernel,
        grid=(b, nc),
        in_specs=[pl.BlockSpec((1, 64, 768), lambda bi, i: (bi, i, 0)),
                  pl.BlockSpec((1, 64, 128), lambda bi, i: (bi, i, 0)),
                  pl.BlockSpec((1, nc, 256), lambda bi, i: (bi, 0, 0)),
                  pl.BlockSpec((1, PAD_ROWS + l, 512), lambda bi, i: (bi, 0, 0)),
                  full(oh), full(wb), full(nb), full(ct), full(amat)],
        out_specs=pl.BlockSpec((1, 64, 768), lambda bi, i: (bi, i, 0)),
        out_shape=jax.ShapeDtypeStruct((b, l, 768), BF16),
        compiler_params=_cparams("parallel", "arbitrary"),
    )(qn, gates, cmp, kvatt, oh, wb, nb, ct, amat)


def _mem_attn_kernel(q_ref, kv_ref, o_ref):
    q = q_ref[0].astype(F32)
    k = kv_ref[0, :, 0:256].astype(BF16)
    v = kv_ref[0, :, 256:512].astype(BF16)
    head = lax.broadcasted_iota(jnp.int32, q.shape, 1) // HEAD_DIM
    out = jnp.zeros(q.shape, F32)
    for h in range(XATTN_HEADS):
        qh = jnp.where(head == h, q, 0.0).astype(BF16)
        s = _nt(qh, k)
        e = jnp.exp(s - jnp.max(s, axis=1, keepdims=True))
        p = e / jnp.sum(e, axis=1, keepdims=True)
        out = out + jnp.where(head == h, _dot(p.astype(BF16), v), 0.0)
    o_ref[0] = out.astype(BF16)


def _mem_attn(xqn, mkv, tq):
    b, l, _ = xqn.shape
    nm = mkv.shape[1]
    return pl.pallas_call(
        _mem_attn_kernel,
        grid=(b, l // tq),
        in_specs=[pl.BlockSpec((1, tq, 256), lambda bi, i: (bi, i, 0)),
                  pl.BlockSpec((1, nm, 512), lambda bi, i: (bi, 0, 0))],
        out_specs=pl.BlockSpec((1, tq, 256), lambda bi, i: (bi, i, 0)),
        out_shape=jax.ShapeDtypeStruct((b, l, 256), BF16),
        compiler_params=_cparams("parallel", "parallel"),
    )(xqn, mkv)


def _pool_kernel(u_ref, prev_ref, w_ref, scale_ref, o_ref, carry_ref, *, pos0):
    i = pl.program_id(1)
    tp = u_ref.shape[1]

    @pl.when(i == 0)
    def _():
        carry_ref[...] = prev_ref[0]

    u = u_ref[0]
    ext = jnp.concatenate([carry_ref[...], u], axis=0)
    carry_ref[...] = ext[tp:tp + 16]
    s2 = ext + pltpu.roll(ext, 1, axis=0)
    s4 = s2 + pltpu.roll(s2, 2, axis=0)
    s8 = s4 + pltpu.roll(s4, 4, axis=0)
    s16 = s8 + pltpu.roll(s8, 8, axis=0)
    lane = lax.broadcasted_iota(jnp.int32, (tp, POOL_WIDTH), 1)
    pos1 = (pos0 + 1 + i * tp + lax.broadcasted_iota(jnp.int32, (tp, POOL_WIDTH), 0)).astype(F32)
    grp = lane // POOL_GROUP_DIM
    ssum = jnp.where(grp == 0, s2[16:], jnp.where(grp == 1, s4[16:], jnp.where(grp == 2, s8[16:], s16[16:])))
    win = jnp.where(grp == 0, 2.0, jnp.where(grp == 1, 4.0, jnp.where(grp == 2, 8.0, 16.0)))
    pooled = ssum / jnp.minimum(win, pos1) - u
    o_ref[0] = (_dot(pooled.astype(BF16), w_ref[...]) * scale_ref[...]).astype(BF16)


def _pool_mix(u, prev16, wbd, scale, tp, pos0):
    b, l, _ = u.shape
    full = lambda a: pl.BlockSpec(a.shape, lambda bi, i: (0,) * a.ndim)
    return pl.pallas_call(
        functools.partial(_pool_kernel, pos0=pos0),
        grid=(b, l // tp),
        in_specs=[pl.BlockSpec((1, tp, POOL_WIDTH), lambda bi, i: (bi, i, 0)),
                  pl.BlockSpec((1, 16, POOL_WIDTH), lambda bi, i: (bi, 0, 0)),
                  full(wbd), full(scale)],
        out_specs=pl.BlockSpec((1, tp, POOL_WIDTH), lambda bi, i: (bi, i, 0)),
        out_shape=jax.ShapeDtypeStruct((b, l, POOL_WIDTH), BF16),
        scratch_shapes=[pltpu.VMEM((16, POOL_WIDTH), F32)],
        compiler_params=_cparams("parallel", "arbitrary"),
    )(u, prev16, wbd, scale)


def _outproj_kernel(y_ref, mix_ref, att_ref, wm_ref, wa_ref, g_ref, o_ref, h_ref):
    y = y_ref[...] + _dot(mix_ref[...], wm_ref[...]) + _dot(att_ref[...], wa_ref[...])
    o_ref[...] = y
    h_ref[...] = _rms_rows(y, g_ref[...]).astype(BF16)


def _outproj(y, mix, att, wm, wa, g_ffn, tm):
    t = y.shape[0]
    full = lambda a: pl.BlockSpec(a.shape, lambda i: (0,) * a.ndim)
    row = lambda n: pl.BlockSpec((tm, n), lambda i: (i, 0))
    return pl.pallas_call(
        _outproj_kernel,
        grid=(t // tm,),
        in_specs=[row(D_MODEL), row(768), row(256), full(wm), full(wa), full(g_ffn)],
        out_specs=[row(D_MODEL), row(D_MODEL)],
        out_shape=[jax.ShapeDtypeStruct((t, D_MODEL), F32), jax.ShapeDtypeStruct((t, D_MODEL), BF16)],
        compiler_params=_cparams("parallel"),
    )(y, mix, att, wm, wa, g_ffn)


def _ffn_half(h, w_ref, cw_ref, cb_ref, ext_ref, prev8, fix):
    tm = h.shape[0]
    u = _dot(h, w_ref[...])
    ext_ref[0:8, :] = prev8
    ext_ref[8:8 + tm, :] = u
    s1 = ext_ref[7:7 + tm, :]
    s2 = ext_ref[6:6 + tm, :]
    if fix is not None:
        rowm, p1, p2 = fix
        s1 = jnp.where(rowm >= 1, s1, p1)
        s2 = jnp.where(rowm >= 2, s2, p2)
    cw = cw_ref[...]
    return u, cb_ref[...] + cw[0:1] * s2 + cw[1:2] * s1 + cw[2:3] * u


FF_SUB = (0, 1024, 2048, D_FF)


def _mix_ffn_kernel(y_ref, mix_ref, att_ref, wm_ref, wo_ref, g_ref, cw_ref, cb_ref, wup_hbm, wd_hbm,
                    o_ref, sa_ref, sv_ref, carry_ref, ext_ref, wup_ref, wd_ref, sem, *, tiles_per_seq):
    j = pl.program_id(0)
    tm = y_ref.shape[0]

    @pl.when(j == 0)
    def _():
        up = pltpu.make_async_copy(wup_hbm, wup_ref, sem.at[0])
        down = pltpu.make_async_copy(wd_hbm, wd_ref, sem.at[1])
        up.start()
        down.start()
        up.wait()
        down.wait()

    @pl.when((j % tiles_per_seq) == 0)
    def _():
        carry_ref[...] = jnp.zeros_like(carry_ref)

    y1 = y_ref[...] + _dot(mix_ref[...], wm_ref[...]) + _dot(att_ref[...], wo_ref[...])
    h = _rms_rows(y1, g_ref[...]).astype(BF16)
    ffn = None
    for c0, c1 in zip(FF_SUB[:-1], FF_SUB[1:]):
        w = c1 - c0
        conv = []
        for half, st_ref in ((0, sa_ref), (1, sv_ref)):
            cols = slice(half * D_FF + c0, half * D_FF + c1)
            u = _dot(h, wup_ref[:, cols])
            ext_ref[half, 0:8, 0:w] = carry_ref[:, cols]
            ext_ref[half, 8:8 + tm, 0:w] = u
            s1 = ext_ref[half, 7:7 + tm, 0:w]
            s2 = ext_ref[half, 6:6 + tm, 0:w]
            cw = cw_ref[:, cols]
            conv.append(cb_ref[:, cols] + cw[0:1] * s2 + cw[1:2] * s1 + cw[2:3] * u)
            carry_ref[:, cols] = u[tm - 8:tm]
            st_ref[0, :, c0:c1] = u[tm - 8:tm]
        d = _dot((jax.nn.silu(conv[0]) * conv[1]).astype(BF16), wd_ref[c0:c1, :])
        ffn = d if ffn is None else ffn + d
    o_ref[...] = y1 + ffn


def _mix_ffn_prompt(y, mix, att, wm, wo, g, w_up, conv_w, conv_b, w_down, tm, seq_len):
    t = y.shape[0]
    nseq = t // seq_len
    tps = seq_len // tm
    full = lambda a: pl.BlockSpec(a.shape, lambda j: (0,) * a.ndim)
    row = lambda n: pl.BlockSpec((tm, n), lambda j: (j, 0))
    hbm = pl.BlockSpec(memory_space=pl.ANY)
    y_out, sa, sv = pl.pallas_call(
        functools.partial(_mix_ffn_kernel, tiles_per_seq=tps),
        grid=(t // tm,),
        in_specs=[row(D_MODEL), row(768), row(256), full(wm), full(wo), full(g), full(conv_w), full(conv_b),
                  hbm, hbm],
        out_specs=[row(D_MODEL), pl.BlockSpec((1, 8, D_FF), lambda j: (j, 0, 0)),
                   pl.BlockSpec((1, 8, D_FF), lambda j: (j, 0, 0))],
        out_shape=[jax.ShapeDtypeStruct((t, D_MODEL), F32), jax.ShapeDtypeStruct((t // tm, 8, D_FF), F32),
                   jax.ShapeDtypeStruct((t // tm, 8, D_FF), F32)],
        scratch_shapes=[pltpu.VMEM((8, 2 * D_FF), F32), pltpu.VMEM((2, tm + 8, max(b - a for a, b in zip(FF_SUB[:-1], FF_SUB[1:]))), F32),
                        pltpu.VMEM(w_up.shape, BF16), pltpu.VMEM(w_down.shape, BF16),
                        pltpu.SemaphoreType.DMA((2,))],
        compiler_params=_cparams("arbitrary"),
    )(y, mix, att, wm, wo, g, conv_w, conv_b, w_up, w_down)
    last = lambda s: s.reshape(nseq, tps, 8, D_FF)[:, tps - 1, 6:8]
    return y_out, jnp.concatenate([last(sa), last(sv)], axis=-1)


def _ffn_sample_kernel(y_ref, g_ref, wa_ref, wv_ref, cwa_ref, cwv_ref, cba_ref, cbv_ref, wd_ref,
                       p1a_ref, p1v_ref, p2a_ref, p2v_ref, o_ref, ua_ref, uv_ref, acc_ref, ext_ref):
    c = pl.program_id(0)
    h = _rms_rows(y_ref[...], g_ref[...]).astype(BF16)
    tm = h.shape[0]
    rowm = lax.broadcasted_iota(jnp.int32, (tm, wa_ref.shape[1]), 0) % 8
    zero8 = jnp.zeros((8, wa_ref.shape[1]), F32)
    ua, ya = _ffn_half(h, wa_ref, cwa_ref, cba_ref, ext_ref, zero8, (rowm, p1a_ref[...], p2a_ref[...]))
    uv, yv = _ffn_half(h, wv_ref, cwv_ref, cbv_ref, ext_ref, zero8, (rowm, p1v_ref[...], p2v_ref[...]))
    ua_ref[...] = ua
    uv_ref[...] = uv
    d = _dot((jax.nn.silu(ya) * yv).astype(BF16), wd_ref[...])

    @pl.when(c == 0)
    def _():
        acc_ref[...] = d

    @pl.when(c == pl.num_programs(0) - 1)
    def _():
        o_ref[...] = y_ref[...] + acc_ref[...] + d


def _ffn_sample(y, g, w_up, conv_w, conv_b, w_down, state):
    t = y.shape[0]
    nseq = t // 8
    fc = FF_CHUNK
    ncf = D_FF // fc
    z = jnp.zeros((nseq, 1, 2 * D_FF), F32)
    p1 = jnp.concatenate([state[:, 1:2]] + [z] * 7, axis=1).reshape(t, 2 * D_FF)
    p2 = jnp.concatenate([state[:, 0:1], state[:, 1:2]] + [z] * 6, axis=1).reshape(t, 2 * D_FF)
    full = lambda a: pl.BlockSpec(a.shape, lambda c: (0,) * a.ndim)
    ca = lambda r: pl.BlockSpec((r, fc), lambda c: (0, c))
    cv = lambda r: pl.BlockSpec((r, fc), lambda c: (0, ncf + c))
    y_out, ua, uv = pl.pallas_call(
        _ffn_sample_kernel,
        grid=(ncf,),
        in_specs=[full(y), full(g), ca(D_MODEL), cv(D_MODEL), ca(3), cv(3), ca(1), cv(1),
                  pl.BlockSpec((fc, D_MODEL), lambda c: (c, 0)), ca(t), cv(t), ca(t), cv(t)],
        out_specs=[pl.BlockSpec((t, D_MODEL), lambda c: (0, 0)), ca(t), ca(t)],
        out_shape=[jax.ShapeDtypeStruct((t, D_MODEL), F32), jax.ShapeDtypeStruct((t, D_FF), F32),
                   jax.ShapeDtypeStruct((t, D_FF), F32)],
        scratch_shapes=[pltpu.VMEM((t, D_MODEL), F32), pltpu.VMEM((t + 8, fc), F32)],
        compiler_params=_cparams("arbitrary"),
    )(y, g, w_up, w_up, conv_w, conv_w, conv_b, conv_b, w_down, p1, p1, p2, p2)
    u = jnp.concatenate([ua, uv], axis=-1).reshape(nseq, 8, 2 * D_FF)
    return y_out, u[:, 6:8]


KEY_CHUNK = 1024
NEW_PAD = 16


def _nsa_sample_kernel(pt_ref, qn_ref, gates_ref, rows_ref, winc_ref, winn_ref, scb_ref, sbt_ref, wbt_ref,
                       rmat_ref, eloc_ref, cmp_hbm, cache_hbm, o_ref, kvbuf, kcv, s_ref, m3_ref, sem, sem2):
    b = pl.program_id(0)
    npages = pt_ref.shape[1]
    ncs = 2 * npages
    nkeys = npages * PAGE_SIZE
    nch = nkeys // KEY_CHUNK
    ppc = KEY_CHUNK // PAGE_SIZE
    wbuf = winc_ref.shape[-1]

    slot = b % 2

    def page_copy(seq, p, sl):
        return pltpu.make_async_copy(cache_hbm.at[pt_ref[seq, p], pl.ds(2, 2)], kvbuf.at[sl, p], sem.at[sl])

    def cmp_copy(p):
        return pltpu.make_async_copy(cmp_hbm.at[pt_ref[b, p]], kcv.at[p], sem2)

    def start_cmp(p, c):
        cmp_copy(p).start()
        return c

    lax.fori_loop(0, npages, start_cmp, 0)

    @pl.when(b == 0)
    def _():
        def start_first(p, c):
            page_copy(0, p, 0).start()
            return c

        lax.fori_loop(0, npages, start_first, 0)

    @pl.when(b + 1 < pl.num_programs(0))
    def _():
        def start_next(p, c):
            page_copy(b + 1, p, 1 - slot).start()
            return c

        lax.fori_loop(0, npages, start_next, 0)

    q = qn_ref[0].astype(F32)
    gts = gates_ref[0]
    lane8 = lax.broadcasted_iota(jnp.int32, (8, LANES), 1)
    half = [lane8 < 64, lane8 >= 64]
    pieces = [jnp.where(half[g], q[:, h * LANES:(h + 1) * LANES], 0.0) for g in range(2) for h in range(NSA_GROUP)]
    qrows = jnp.concatenate(pieces + [jnp.zeros((32, LANES), F32)], axis=0).astype(BF16)

    def wait_cmp(p, c):
        cmp_copy(p).wait()
        return c

    lax.fori_loop(0, npages, wait_cmp, 0)

    kc = jnp.concatenate([kcv[:, 0, 0:128], kcv[:, 1, 0:128]], axis=0).astype(BF16)
    vc = jnp.concatenate([kcv[:, 0, 128:256], kcv[:, 1, 128:256]], axis=0).astype(BF16)
    lc = _nt(kc, qrows) + scb_ref[...]
    e = jnp.exp(lc - jnp.max(lc, axis=0, keepdims=True))
    p_c = e / jnp.sum(e, axis=0, keepdims=True)
    o_c = _tn(p_c.astype(BF16), vc)
    hi = p_c.astype(BF16)
    lo = (p_c - hi.astype(F32)).astype(BF16)
    imp = _dot(hi, rmat_ref[...]) + _dot(lo, rmat_ref[...])

    def blk_of(r):
        return jnp.where(r < npages, 2 * r, jnp.where(r < ncs, 2 * (r - npages) + 1, r))

    blk = blk_of(lax.broadcasted_iota(jnp.int32, (ncs + 8, LANES), 0))
    impx = jnp.concatenate([imp, jnp.zeros((8, LANES), F32)], axis=0)
    forced = (blk == 0) | (blk == ncs) | (blk == ncs - 1)
    score = jnp.where(blk <= ncs, jnp.where(forced, FORCED_SCORE, impx), -1.0)
    blkf = blk.astype(F32)
    work, picked = score, jnp.zeros((ncs + 8, LANES), F32)
    for _ in range(SLC_TOPN):
        top = jnp.max(work, axis=0, keepdims=True)
        first = jnp.min(jnp.where(work == top, blkf, float(2 * ncs + 16)), axis=0, keepdims=True)
        hit = blkf == first
        picked = jnp.where(hit, 1.0, picked)
        work = jnp.where(hit, -2.0, work)
    madd = jnp.where((picked > 0.0) & (score >= 0.0), 0.0, -BIG)
    eye = jnp.where(lax.broadcasted_iota(jnp.int32, (LANES, LANES), 0)
                    == lax.broadcasted_iota(jnp.int32, (LANES, LANES), 1), 1.0, 0.0).astype(BF16)
    hb = ppc
    for t in range(nch):
        mt = jnp.concatenate([madd[t * hb:(t + 1) * hb], madd[npages + t * hb:npages + (t + 1) * hb],
                              jnp.zeros((LANES - 2 * hb, LANES), F32)], axis=0).astype(BF16)
        m3_ref[t] = _nt(eye, mt).astype(BF16)

    def wait_page(p, c):
        page_copy(b, p, slot).wait()
        return c

    lax.fori_loop(0, npages, wait_page, 0)

    eloc = eloc_ref[...]
    pad8 = jnp.zeros((NEW_PAD - 8, LANES), F32)
    knew = jnp.concatenate([rows_ref[0, :, 256:384], pad8], axis=0).astype(BF16)
    vnew = jnp.concatenate([rows_ref[0, :, 384:512], pad8], axis=0).astype(BF16)
    kwn = jnp.concatenate([winn_ref[0, :, 0:128], pad8], axis=0).astype(BF16)
    vwn = jnp.concatenate([winn_ref[0, :, 128:256], pad8], axis=0).astype(BF16)
    kwt = winc_ref[0, 0].reshape(LANES, wbuf).astype(BF16)
    vwt = winc_ref[0, 1].reshape(LANES, wbuf).astype(BF16)
    near_pad = jnp.concatenate([jnp.zeros((LANES, KEY_CHUNK - PAGE_SIZE), F32), sbt_ref[:, 0:PAGE_SIZE]], axis=1)
    low = lax.broadcasted_iota(jnp.int32, (8, LANES), 1) < 64

    def page_rows(t, comp):
        tiles = [kvbuf[slot, t * ppc + pp, comp].reshape(LANES, PAGE_SIZE) for pp in range(ppc)]
        return jnp.concatenate(tiles, axis=1).astype(BF16)

    def s_body(t, m):
        lhs = jnp.concatenate([m3_ref[t], qrows], axis=1)
        s = _dot(lhs, jnp.concatenate([eloc, page_rows(t, 0)], axis=0))
        s = s + jnp.where(t == nch - 1, 1.0, 0.0) * near_pad
        s_ref[t] = s
        return jnp.maximum(m, jnp.max(s, axis=1, keepdims=True))

    s_new = _nt(qrows, knew) + sbt_ref[:, PAGE_SIZE:PAGE_SIZE + NEW_PAD]
    unroll = math.gcd(nch, 4)
    m = lax.fori_loop(0, nch, s_body, jnp.max(s_new, axis=1, keepdims=True), unroll=unroll)
    e_new = jnp.exp(s_new - m)

    def pv_body(t, carry):
        l, acc = carry
        et = jnp.exp(s_ref[t] - m)
        return l + jnp.sum(et, axis=1, keepdims=True), acc + _nt(et.astype(BF16), page_rows(t, 1))

    l, acc = lax.fori_loop(0, nch, pv_body, (jnp.sum(e_new, axis=1, keepdims=True),
                                             _dot(e_new.astype(BF16), vnew)), unroll=unroll)
    o_s = acc / l

    sw = _dot(qrows, kwt) + wbt_ref[:, 0:wbuf]
    swn = _nt(qrows, kwn) + wbt_ref[:, wbuf:wbuf + NEW_PAD]
    mw = jnp.maximum(jnp.max(sw, axis=1, keepdims=True), jnp.max(swn, axis=1, keepdims=True))
    ew = jnp.exp(sw - mw)
    ewn = jnp.exp(swn - mw)
    o_w = (_nt(ew.astype(BF16), vwt) + _dot(ewn.astype(BF16), vwn)) / (
        jnp.sum(ew, axis=1, keepdims=True) + jnp.sum(ewn, axis=1, keepdims=True))

    def gate_col(br):
        cols = [gts[:, hh * 3 + br:hh * 3 + br + 1] for hh in range(NSA_HEADS)]
        return jnp.concatenate(cols + [jnp.zeros((32, 1), F32)], axis=0)

    o = gate_col(0) * o_c + gate_col(1) * o_s + gate_col(2) * o_w
    for h in range(NSA_GROUP):
        r0, r1 = h * 8, (NSA_GROUP + h) * 8
        o_ref[0, :, h * LANES:(h + 1) * LANES] = jnp.where(low, o[r0:r0 + 8], o[r1:r1 + 8]).astype(BF16)


def _nsa_sample_attn(page_table, qn, gates, rows, winc, winn, scb, sbt, wbt, rmat, eloc, cmp_phys, cache):
    b, npages = page_table.shape
    ncs = 2 * npages
    nkeys = npages * PAGE_SIZE
    assert nkeys % KEY_CHUNK == 0
    nch = nkeys // KEY_CHUNK
    full = lambda a: pl.BlockSpec(a.shape, lambda bi, pt: (0,) * a.ndim)
    per_b = lambda a: pl.BlockSpec((1,) + a.shape[1:], lambda bi, pt: (bi,) + (0,) * (a.ndim - 1))
    grid_spec = pltpu.PrefetchScalarGridSpec(
        num_scalar_prefetch=1,
        grid=(b,),
        in_specs=[per_b(qn), per_b(gates), per_b(rows), per_b(winc), per_b(winn),
                  full(scb), full(sbt), full(wbt), full(rmat), full(eloc),
                  pl.BlockSpec(memory_space=pl.ANY), pl.BlockSpec(memory_space=pl.ANY)],
        out_specs=pl.BlockSpec((1, 8, 768), lambda bi, pt: (bi, 0, 0)),
        scratch_shapes=[pltpu.VMEM((2, npages, 2, 2, 64, PAGE_SIZE), F32), pltpu.VMEM((npages, 2, 256), F32),
                        pltpu.VMEM((nch, LANES, KEY_CHUNK), F32), pltpu.VMEM((nch, LANES, LANES), BF16),
                        pltpu.SemaphoreType.DMA((2,)), pltpu.SemaphoreType.DMA(())],
    )
    return pl.pallas_call(
        _nsa_sample_kernel,
        grid_spec=grid_spec,
        out_shape=jax.ShapeDtypeStruct((b, 8, 768), BF16),
        compiler_params=_cparams("arbitrary"),
    )(page_table, qn, gates, rows, winc, winn, scb, sbt, wbt, rmat, eloc, cmp_phys, cache)


def _block_diag2(w):
    z = jnp.zeros_like(w)
    return jnp.concatenate([jnp.concatenate([w, z], axis=-1), jnp.concatenate([z, w], axis=-1)], axis=-2)


def _q_slot_perm():
    return np.concatenate([np.r_[h * 64:(h + 1) * 64, (NSA_GROUP + h) * 64:(NSA_GROUP + h + 1) * 64]
                           for h in range(NSA_GROUP)])


def _sample_codes(ncs, wbuf):
    ql = np.arange(8)[None, :]
    c = np.arange(ncs)[:, None]
    scb = _bucket_np((ncs - c) * CMP_BLOCK + ql - (CMP_BLOCK - 1))
    kap = np.arange(PAGE_SIZE + NEW_PAD)[:, None]
    d = np.where(kap < PAGE_SIZE, PAGE_SIZE + ql - kap, ql - (kap - PAGE_SIZE))
    sbt = np.where((d >= 0) & (kap < PAGE_SIZE + 8), _bucket_np(d), -1)
    j = np.arange(wbuf + NEW_PAD)[:, None]
    d = wbuf + ql - j
    wbt = np.where((d >= 0) & (d <= WINDOW) & (j < wbuf + 8), _bucket_np(d), -1)
    return np.concatenate([scb, sbt, wbt], axis=0).astype(np.int32)


def kernel(x_prompt, x_sample, cache_nsa_kv, cache_nsa_win, cache_mem_kv, state_pool, state_ffn_conv,
           page_table, mem_prompt, rel_bias, g_mix, g_mem, g_ffn, w_in_nsa, q_gain_nsa, k_gain_nsa,
           cmp_pos, cmp_w1, cmp_b1, cmp_w2, cmp_b2, w_in_pool, w_pool_grp, pool_scale, w_mem_kv,
           xq_gain, xk_gain, w_out, w_up, conv_w, conv_b, w_down):
    bp, lp, _ = x_prompt.shape
    bs, ls, _ = x_sample.shape
    assert ls == 8 and lp % 512 == 0 and lp // 64 <= 64
    npages = page_table.shape[1]
    past_len = npages * PAGE_SIZE
    n_phys = cache_nsa_kv.shape[1]
    wbuf = cache_nsa_win.shape[2]
    nm = mem_prompt.shape[1]
    tp_, ts_ = bp * lp, bs * ls
    nc = lp // 64
    ncs = 2 * npages
    scale = HEAD_DIM ** -0.5
    tile2 = lambda v: jnp.tile(v, 2)[None, :]

    a = np.arange(LANES)
    seg = jnp.asarray(a[:, None] // 64 == a[None, :] // 64, BF16)
    perm = _q_slot_perm()

    wb = _bias_table(rel_bias, _window_codes()).reshape(NSA_HEADS * 64, WIN_KEYS)
    nbias = jnp.concatenate([jnp.full((NSA_HEADS * 64, 64), -BIG, F32), wb[:, WIN_KEYS - 192:]], axis=1)
    ct = _bias_table(rel_bias, _cmp_codes()).transpose(1, 0, 2).reshape(8, NSA_HEADS * 64)
    amat = np.zeros((2, NSA_GROUP * 64, LANES), np.float32)
    for g in range(2):
        for h in range(NSA_GROUP):
            amat[g, h * 64 + np.arange(64), g * 64 + np.arange(64)] = 1.0
    amat = jnp.asarray(amat.reshape(NSA_HEADS * 64, LANES), BF16)
    r = np.arange(PAD_ROWS + lp)
    oh_np = np.zeros((PAD_ROWS + lp, LANES), np.float32)
    oh_np[r, np.where(r < PAD_ROWS, 64, (r - PAD_ROWS) // 64)] = 1.0
    oh = jnp.asarray(oh_np, BF16)
    stab = _bias_table(rel_bias, _sample_codes(ncs, wbuf))
    stab = jnp.pad(stab.transpose(1, 0, 2).reshape(-1, NSA_HEADS * 8), ((0, 0), (0, LANES - NSA_HEADS * 8)))
    ns = PAGE_SIZE + NEW_PAD
    scb, sbt, wbt = stab[:ncs], stab[ncs:ncs + ns].T, stab[ncs + ns:].T
    scb = jnp.concatenate([scb[0::2], scb[1::2]], axis=0)
    ppc = KEY_CHUNK // PAGE_SIZE
    eloc_np = np.zeros((LANES, KEY_CHUNK), np.float32)
    for pp in range(ppc):
        eloc_np[pp, pp * PAGE_SIZE:pp * PAGE_SIZE + 64] = 1.0
        eloc_np[ppc + pp, pp * PAGE_SIZE + 64:(pp + 1) * PAGE_SIZE] = 1.0
    eloc = jnp.asarray(eloc_np, BF16)
    lam = np.arange(LANES)
    rm = (lam[:, None] < 96) & (lam[None, :] < 96) & (lam[:, None] // 48 == lam[None, :] // 48) \
        & (lam[:, None] % 8 == lam[None, :] % 8)
    rmat = jnp.asarray(rm, BF16)

    yp = x_prompt.reshape(tp_, D_MODEL)
    ys = x_sample.reshape(ts_, D_MODEL)
    tm = 512
    outs = {k: [] for k in ("kv_p", "kv_s", "win_p", "win_s", "mem_p", "pool_p", "pool_s", "conv_p", "conv_s")}
    depth = g_mix.shape[0]
    for i in range(depth):
        j = i // 2
        wo = w_out[i]
        xg = tile2(xq_gain[i]) * scale
        mkv_p = _memkv(mem_prompt.reshape(bp * nm, D_MODEL), g_mem[i][None], w_mem_kv[i].astype(BF16), seg,
                       tile2(xk_gain[i]), min(tm, bp * nm)).reshape(bp, nm, 2 * XATTN_WIDTH)
        outs["mem_p"].append(mkv_p.reshape(bp, nm, 2, XATTN_HEADS, HEAD_DIM))
        mkv_s = cache_mem_kv[i].reshape(bs, nm, 2 * XATTN_WIDTH)
        if i % 2 == 0:
            w = w_in_nsa[j]
            w = jnp.concatenate([w[:, perm], w[:, 768:1536], w[:, 1572:1828], w[:, 1536:1572],
                                 jnp.zeros((D_MODEL, NSA_IN_COLS - 1828), F32)], axis=1).astype(BF16)
            qg = tile2(q_gain_nsa[j]) * scale
            kg = tile2(k_gain_nsa[j])
            cmp_args = (jnp.tile(cmp_pos[j], (1, 1, 2))[:, :, None, :], _block_diag2(cmp_w1[j]).astype(BF16),
                        jnp.tile(cmp_b1[j], (1, 2))[:, None, :], _block_diag2(cmp_w2[j]).astype(BF16),
                        jnp.tile(cmp_b2[j], (1, 2))[:, None, :], kg, seg)
            wo_mix = wo[:768][perm].astype(BF16)
            w2 = cmp_w2[j]
            w2sel = jnp.zeros((2, 2, 2, 2 * CMP_HIDDEN, LANES), F32)
            for blk in range(2):
                for g in range(2):
                    w2sel = w2sel.at[:, blk, g, blk * CMP_HIDDEN:(blk + 1) * CMP_HIDDEN,
                                     g * HEAD_DIM:(g + 1) * HEAD_DIM].set(w2)
            cc_args = (jnp.tile(cmp_pos[j].transpose(0, 2, 1), (1, 1, 2))[:, :, None, :],
                       _block_diag2(cmp_w1[j].transpose(0, 2, 1, 3)).astype(BF16).reshape(2, 32, 2 * LANES, 256),
                       cmp_args[2], w2sel.astype(BF16), cmp_args[4], kg, seg, 512)
            wt = w_in_nsa[j][:, 768:1536].T.astype(BF16)
            qn, rows_t, win_t, katt, xqn_p, gates = _nsa_inproj_t(
                yp, g_mix[i][None], w, wt, seg, qg, kg, jnp.tile(k_gain_nsa[j], 2)[:, None], xg, tm, lp)
            pages_t = rows_t[:, :256].reshape(bp, 2, 2, HEAD_DIM, lp // PAGE_SIZE, PAGE_SIZE)
            pages_t = pages_t.transpose(0, 4, 1, 2, 3, 5).reshape(bp * lp // PAGE_SIZE, 2, 2, HEAD_DIM, PAGE_SIZE)
            cmp_p = _compress_cache(pages_t, *cc_args).reshape(bp, nc, 256)
            kvatt = jnp.pad(katt.reshape(bp, lp, 512), ((0, 0), (PAD_ROWS, 0), (0, 0)))
            mix_p = _nsa_prompt_attn(qn.reshape(bp, lp, 768), gates.reshape(bp, lp, 128), cmp_p, kvatt,
                                     oh, wb, nbias, ct, amat).reshape(tp_, 768)
            outs["kv_p"].append(rows_t.reshape(bp, 4, NSA_KV_HEADS, HEAD_DIM, lp).transpose(0, 4, 1, 2, 3))
            wlen = min(WINDOW, lp)
            outs["win_p"].append(win_t[:, :, lp - wlen:].reshape(bp, 2, NSA_KV_HEADS, HEAD_DIM, wlen)
                                 .transpose(0, 4, 1, 2, 3))
            qn, rows, win, _, xqn_s, gates = _nsa_inproj(ys, g_mix[i][None], w, seg, qg, kg, xg, ts_)
            cache_t = cache_nsa_kv[j].transpose(0, 2, 3, 4, 1)
            cmp_phys = _compress_cache(cache_t, *cc_args)
            winc = cache_nsa_win[j].transpose(0, 2, 3, 4, 1)
            mix_s = _nsa_sample_attn(page_table, qn.reshape(bs, ls, 768), gates.reshape(bs, ls, 128),
                                     rows.reshape(bs, ls, 512), winc, win.reshape(bs, ls, 256),
                                     scb, sbt, wbt, rmat, eloc, cmp_phys.reshape(n_phys, 2, 256),
                                     cache_t).reshape(ts_, 768)
            outs["kv_s"].append(rows.reshape(bs, ls, 4, NSA_KV_HEADS, HEAD_DIM))
            wall = jnp.concatenate([cache_nsa_win[j], win.reshape(bs, ls, 2, NSA_KV_HEADS, HEAD_DIM)], axis=1)
            outs["win_s"].append(wall[:, -wbuf:])
        else:
            w = w_in_pool[j].astype(BF16)
            wbd = jnp.zeros((POOL_WIDTH, POOL_WIDTH), F32)
            for gi in range(len(POOL_WINDOWS)):
                sl = slice(gi * POOL_GROUP_DIM, (gi + 1) * POOL_GROUP_DIM)
                wbd = wbd.at[sl, sl].set(w_pool_grp[j, gi])
            wbd = wbd.astype(BF16)
            psc = pool_scale[j][None]
            wo_mix = wo[:768].astype(BF16)
            u_p, xqn_p = _pool_inproj(yp, g_mix[i][None], w, seg, xg, tm)
            u_p3 = u_p.reshape(bp, lp, POOL_WIDTH)
            mix_p = _pool_mix(u_p3, jnp.zeros((bp, 16, POOL_WIDTH), F32), wbd, psc, tm, 0).reshape(tp_, 768)
            outs["pool_p"].append(u_p3[:, -POOL_STATE:])
            u_s, xqn_s = _pool_inproj(ys, g_mix[i][None], w, seg, xg, ts_)
            u_s3 = u_s.reshape(bs, ls, POOL_WIDTH)
            prev16 = jnp.concatenate([jnp.zeros((bs, 1, POOL_WIDTH), F32), state_pool[j]], axis=1)
            mix_s = _pool_mix(u_s3, prev16, wbd, psc, ls, past_len).reshape(ts_, 768)
            outs["pool_s"].append(jnp.concatenate([state_pool[j], u_s3], axis=1)[:, -POOL_STATE:])
        wo_att = wo[768:].astype(BF16)
        att_p = _mem_attn(xqn_p.reshape(bp, lp, 256), mkv_p, tm).reshape(tp_, 256)
        att_s = _mem_attn(xqn_s.reshape(bs, ls, 256), mkv_s, ls).reshape(ts_, 256)
        wu, wd = w_up[i].astype(BF16), w_down[i].astype(BF16)
        yp, c_p = _mix_ffn_prompt(yp, mix_p, att_p, wo_mix, wo_att, g_ffn[i][None], wu, conv_w[i],
                                  conv_b[i][None], wd, tm, lp)
        ys, _ = _outproj(ys, mix_s, att_s, wo_mix, wo_att, g_ffn[i][None], ts_)
        ys, c_s = _ffn_sample(ys, g_ffn[i][None], wu, conv_w[i], conv_b[i][None], wd, state_ffn_conv[i])
        outs["conv_p"].append(c_p)
        outs["conv_s"].append(c_s)
    st = lambda k: jnp.stack(outs[k])
    return (yp.reshape(bp, lp, D_MODEL), ys.reshape(bs, ls, D_MODEL), st("kv_p"), st("kv_s"), st("win_p"),
            st("win_s"), st("mem_p"), st("pool_p"), st("pool_s"), st("conv_p"), st("conv_s"))
```

```python
import functools
import math

import numpy as np
import jax
import jax.numpy as jnp
from jax import lax
from jax.experimental import pallas as pl
from jax.experimental.pallas import tpu as pltpu

F32 = jnp.float32
BF16 = jnp.bfloat16

D_MODEL = 1024
PAGE_SIZE = 128
HEAD_DIM = 64
NSA_HEADS = 12
NSA_KV_HEADS = 2
NSA_GROUP = NSA_HEADS // NSA_KV_HEADS
NSA_WIDTH = NSA_HEADS * HEAD_DIM
NSA_KV_WIDTH = NSA_KV_HEADS * HEAD_DIM
CMP_BLOCK = 64
CMP_HIDDEN = 128
SLC_TOPN = 16
WINDOW = 512
FORCED_SCORE = 1e4
XATTN_HEADS = 4
XATTN_WIDTH = XATTN_HEADS * HEAD_DIM
POOL_WINDOWS = (2, 4, 8, 16)
POOL_GROUP_DIM = 192
POOL_WIDTH = len(POOL_WINDOWS) * POOL_GROUP_DIM
POOL_STATE = max(POOL_WINDOWS) - 1
D_FF = 2816
REL_BUCKETS = 32
REL_MAX_DIST = 128
EPS = 1e-6
BIG = 1e30

LANES = 128
WIN_KEYS = WINDOW + 2 * 64
NEAR_KEYS = 4 * 64
PAD_ROWS = WIN_KEYS
NSA_IN_COLS = 1920
FF_CHUNK = D_FF // 2
VMEM_LIMIT = 56 * 1024 * 1024


def _cparams(*sem):
    return pltpu.CompilerParams(dimension_semantics=sem, vmem_limit_bytes=VMEM_LIMIT)


def _nt(a, b):
    return lax.dot_general(a, b, (((1,), (1,)), ((), ())), preferred_element_type=F32)


def _tn(a, b):
    return lax.dot_general(a, b, (((0,), (0,)), ((), ())), preferred_element_type=F32)


def _dot(a, b):
    return jnp.dot(a, b, preferred_element_type=F32)


def _rms_rows(x, g):
    return x * lax.rsqrt(jnp.mean(x * x, axis=-1, keepdims=True) + EPS) * g


def _seg_rms(zc, seg, gain):
    zz = zc * zc
    hi = zz.astype(BF16)
    lo = (zz - hi.astype(F32)).astype(BF16)
    ms = (_dot(hi, seg) + _dot(lo, seg)) * (1.0 / HEAD_DIM)
    return zc * lax.rsqrt(ms + EPS) * gain


def _nsa_inproj_kernel(x_ref, g_ref, w_ref, seg_ref, qg_ref, kg_ref, xg_ref,
                       qn_ref, rows_ref, win_ref, katt_ref, xqn_ref, gates_ref):
    xn = _rms_rows(x_ref[...], g_ref[...])
    z = _dot(xn.astype(BF16), w_ref[...])
    seg = seg_ref[...]
    for h in range(6):
        qn_ref[:, h * LANES:(h + 1) * LANES] = _seg_rms(z[:, h * LANES:(h + 1) * LANES], seg, qg_ref[...]).astype(BF16)
    kv = 768
    rows_ref[:, 0:256] = z[:, kv:kv + 256]
    ks = _seg_rms(z[:, kv + 256:kv + 384], seg, kg_ref[...])
    vs = z[:, kv + 384:kv + 512]
    kw = _seg_rms(z[:, kv + 512:kv + 640], seg, kg_ref[...])
    vw = z[:, kv + 640:kv + 768]
    rows_ref[:, 256:384] = ks
    rows_ref[:, 384:512] = vs
    win_ref[:, 0:128] = kw
    win_ref[:, 128:256] = vw
    katt_ref[:, 0:128] = ks.astype(BF16)
    katt_ref[:, 128:256] = vs.astype(BF16)
    katt_ref[:, 256:384] = kw.astype(BF16)
    katt_ref[:, 384:512] = vw.astype(BF16)
    for h in range(2):
        c0 = 1536 + h * LANES
        xqn_ref[:, h * LANES:(h + 1) * LANES] = _seg_rms(z[:, c0:c0 + LANES], seg, xg_ref[...]).astype(BF16)
    gates_ref[...] = jax.nn.sigmoid(z[:, 1792:1920])


def _nsa_inproj_t_kernel(x_ref, g_ref, w_ref, wt_ref, seg_ref, qg_ref, kg_ref, kgc_ref, xg_ref,
                         qn_ref, rowst_ref, wint_ref, katt_ref, xqn_ref, gates_ref):
    xn = _rms_rows(x_ref[...], g_ref[...]).astype(BF16)
    z = _dot(xn, w_ref[...])
    zt = _nt(wt_ref[...], xn)

    def norm_t(a):
        parts = []
        for gi in range(2):
            ag = a[gi * HEAD_DIM:(gi + 1) * HEAD_DIM]
            parts.append(ag * lax.rsqrt(jnp.mean(ag * ag, axis=0, keepdims=True) + EPS))
        return jnp.concatenate(parts, axis=0) * kgc_ref[...]

    rowst_ref[0, 0:256, :] = zt[0:256]
    rowst_ref[0, 256:384, :] = norm_t(zt[256:384])
    rowst_ref[0, 384:512, :] = zt[384:512]
    wint_ref[0, 0:128, :] = norm_t(zt[512:640])
    wint_ref[0, 128:256, :] = zt[640:768]
    seg = seg_ref[...]
    for h in range(6):
        qn_ref[:, h * LANES:(h + 1) * LANES] = _seg_rms(z[:, h * LANES:(h + 1) * LANES], seg, qg_ref[...]).astype(BF16)
    kv = 768
    katt_ref[:, 0:128] = _seg_rms(z[:, kv + 256:kv + 384], seg, kg_ref[...]).astype(BF16)
    katt_ref[:, 128:256] = z[:, kv + 384:kv + 512].astype(BF16)
    katt_ref[:, 256:384] = _seg_rms(z[:, kv + 512:kv + 640], seg, kg_ref[...]).astype(BF16)
    katt_ref[:, 384:512] = z[:, kv + 640:kv + 768].astype(BF16)
    for h in range(2):
        c0 = 1536 + h * LANES
        xqn_ref[:, h * LANES:(h + 1) * LANES] = _seg_rms(z[:, c0:c0 + LANES], seg, xg_ref[...]).astype(BF16)
    gates_ref[...] = jax.nn.sigmoid(z[:, 1792:1920])


def _nsa_inproj_t(x, g, w, wt, seg, qg, kg, kgc, xg, tm, seq_len):
    t = x.shape[0]
    tps = seq_len // tm
    full = lambda a: pl.BlockSpec(a.shape, lambda i: (0,) * a.ndim)
    row = lambda n: pl.BlockSpec((tm, n), lambda i: (i, 0))
    col = lambda n: pl.BlockSpec((1, n, tm), lambda i: (i // tps, 0, i % tps))
    wlen = min(WINDOW, seq_len)
    assert wlen % tm == 0
    skip = tps - wlen // tm
    win = pl.BlockSpec((1, 256, tm), lambda i: (i // tps, 0, jnp.maximum(i % tps - skip, 0)))
    return pl.pallas_call(
        _nsa_inproj_t_kernel,
        grid=(t // tm,),
        in_specs=[row(D_MODEL), full(g), full(w), full(wt), full(seg), full(qg), full(kg), full(kgc), full(xg)],
        out_specs=[row(768), col(512), win, row(512), row(256), row(128)],
        out_shape=[jax.ShapeDtypeStruct((t, 768), BF16), jax.ShapeDtypeStruct((t // seq_len, 512, seq_len), F32),
                   jax.ShapeDtypeStruct((t // seq_len, 256, wlen), F32), jax.ShapeDtypeStruct((t, 512), BF16),
                   jax.ShapeDtypeStruct((t, 256), BF16), jax.ShapeDtypeStruct((t, 128), F32)],
        compiler_params=_cparams("arbitrary"),
    )(x, g, w, wt, seg, qg, kg, kgc, xg)


def _nsa_inproj(x, g, w, seg, qg, kg, xg, tm):
    t = x.shape[0]
    full = lambda a: pl.BlockSpec(a.shape, lambda i: (0,) * a.ndim)
    row = lambda n: pl.BlockSpec((tm, n), lambda i: (i, 0))
    return pl.pallas_call(
        _nsa_inproj_kernel,
        grid=(t // tm,),
        in_specs=[row(D_MODEL), full(g), full(w), full(seg), full(qg), full(kg), full(xg)],
        out_specs=[row(768), row(512), row(256), row(512), row(256), row(128)],
        out_shape=[jax.ShapeDtypeStruct((t, 768), BF16), jax.ShapeDtypeStruct((t, 512), F32),
                   jax.ShapeDtypeStruct((t, 256), F32), jax.ShapeDtypeStruct((t, 512), BF16),
                   jax.ShapeDtypeStruct((t, 256), BF16), jax.ShapeDtypeStruct((t, 128), F32)],
        compiler_params=_cparams("parallel"),
    )(x, g, w, seg, qg, kg, xg)


def _pool_inproj_kernel(x_ref, g_ref, w_ref, seg_ref, xg_ref, u_ref, xqn_ref):
    xn = _rms_rows(x_ref[...], g_ref[...])
    z = _dot(xn.astype(BF16), w_ref[...])
    u_ref[...] = z[:, 0:POOL_WIDTH]
    seg = seg_ref[...]
    for h in range(2):
        c0 = POOL_WIDTH + h * LANES
        xqn_ref[:, h * LANES:(h + 1) * LANES] = _seg_rms(z[:, c0:c0 + LANES], seg, xg_ref[...]).astype(BF16)


def _pool_inproj(x, g, w, seg, xg, tm):
    t = x.shape[0]
    full = lambda a: pl.BlockSpec(a.shape, lambda i: (0,) * a.ndim)
    row = lambda n: pl.BlockSpec((tm, n), lambda i: (i, 0))
    return pl.pallas_call(
        _pool_inproj_kernel,
        grid=(t // tm,),
        in_specs=[row(D_MODEL), full(g), full(w), full(seg), full(xg)],
        out_specs=[row(POOL_WIDTH), row(256)],
        out_shape=[jax.ShapeDtypeStruct((t, POOL_WIDTH), F32), jax.ShapeDtypeStruct((t, 256), BF16)],
        compiler_params=_cparams("parallel"),
    )(x, g, w, seg, xg)


def _memkv_kernel(x_ref, g_ref, w_ref, seg_ref, kg_ref, o_ref):
    xn = _rms_rows(x_ref[...], g_ref[...])
    z = _dot(xn.astype(BF16), w_ref[...])
    seg = seg_ref[...]
    for h in range(2):
        o_ref[:, h * LANES:(h + 1) * LANES] = _seg_rms(z[:, h * LANES:(h + 1) * LANES], seg, kg_ref[...])
    o_ref[:, 256:512] = z[:, 256:512]


def _memkv(x, g, w, seg, kg, tm):
    t = x.shape[0]
    full = lambda a: pl.BlockSpec(a.shape, lambda i: (0,) * a.ndim)
    row = lambda n: pl.BlockSpec((tm, n), lambda i: (i, 0))
    return pl.pallas_call(
        _memkv_kernel,
        grid=(t // tm,),
        in_specs=[row(D_MODEL), full(g), full(w), full(seg), full(kg)],
        out_specs=row(512),
        out_shape=jax.ShapeDtypeStruct((t, 512), F32),
        compiler_params=_cparams("parallel"),
    )(x, g, w, seg, kg)


def _compress_cache_kernel(x_ref, pos_ref, w1_ref, b1_ref, w2_ref, b2_ref, kg_ref, seg_ref, o_ref, acc_ref):
    dc = pl.program_id(1)

    @pl.when(dc == 0)
    def _():
        acc_ref[...] = jnp.zeros_like(acc_ref)

    pt, nd = x_ref.shape[0], x_ref.shape[3]
    x2 = x_ref.reshape(pt * 4 * nd, LANES)

    def feature_rows(comp, g, dd):
        return (x2[pl.ds((comp * 2 + g) * nd + dd, pt, stride=4 * nd), :] + pos_ref[comp, dd]).astype(BF16)

    for comp in range(2):
        for g in range(2):
            part = None
            for dp in range(nd // 2):
                xd = jnp.concatenate([feature_rows(comp, g, 2 * dp), feature_rows(comp, g, 2 * dp + 1)], axis=1)
                d = _dot(xd, w1_ref[comp, dp])
                part = d if part is None else part + d
            c0 = (comp * 2 + g) * 256
            acc_ref[:, c0:c0 + 256] += part

    @pl.when(dc == pl.num_programs(1) - 1)
    def _():
        for comp in range(2):
            hid = [jax.nn.gelu(acc_ref[:, (comp * 2 + g) * 256:(comp * 2 + g + 1) * 256] + b1_ref[comp]).astype(BF16)
                   for g in range(2)]
            for blk in range(2):
                o = _dot(hid[0], w2_ref[comp, blk, 0]) + _dot(hid[1], w2_ref[comp, blk, 1]) + b2_ref[comp]
                if comp == 0:
                    o = _seg_rms(o, seg_ref[...], kg_ref[...])
                o_ref[:, blk * 256 + comp * LANES:blk * 256 + (comp + 1) * LANES] = o


CMP_FEATS = 16


def _compress_cache(xt, pos_t, w1t, b1t, w2sel, b2t, kg, seg, pt):
    n_phys = xt.shape[0]
    pt = min(pt, n_phys)
    assert n_phys % pt == 0
    full = lambda a: pl.BlockSpec(a.shape, lambda j, d: (0,) * a.ndim)
    return pl.pallas_call(
        _compress_cache_kernel,
        grid=(n_phys // pt, HEAD_DIM // CMP_FEATS),
        in_specs=[pl.BlockSpec((pt, 2, 2, CMP_FEATS, LANES), lambda j, d: (j, 0, 0, d, 0)),
                  pl.BlockSpec((2, CMP_FEATS, 1, LANES), lambda j, d: (0, d, 0, 0)),
                  pl.BlockSpec((2, CMP_FEATS // 2, 2 * LANES, 256), lambda j, d: (0, d, 0, 0)),
                  full(b1t), full(w2sel), full(b2t), full(kg), full(seg)],
        out_specs=pl.BlockSpec((pt, 512), lambda j, d: (j, 0)),
        out_shape=jax.ShapeDtypeStruct((n_phys, 512), F32),
        scratch_shapes=[pltpu.VMEM((pt, 1024), F32)],
        compiler_params=_cparams("parallel", "arbitrary"),
    )(xt, pos_t, w1t, b1t, w2sel, b2t, kg, seg)


def _bucket_np(d):
    n = np.maximum(d, 0)
    max_exact = REL_BUCKETS // 2
    nf = np.maximum(n, 1).astype(np.float32)
    large = max_exact + (np.log(nf / max_exact) / math.log(REL_MAX_DIST / max_exact)
                         * (REL_BUCKETS - max_exact)).astype(np.int32)
    large = np.minimum(large, REL_BUCKETS - 1)
    return np.where(n < max_exact, n, large).astype(np.int32)


def _bias_table_kernel(rb_ref, code_ref, o_ref):
    code = code_ref[...]
    for h in range(NSA_HEADS):
        far = rb_ref[REL_BUCKETS - 1, h]
        acc = jnp.full(code.shape, -BIG, F32)
        for k in range(REL_BUCKETS):
            acc = jnp.where(code == k, rb_ref[k, h] - far, acc)
        o_ref[h] = acc


def _bias_table(rel_bias, code):
    code = jnp.asarray(code, jnp.int32)
    return pl.pallas_call(
        _bias_table_kernel,
        in_specs=[pl.BlockSpec(memory_space=pltpu.SMEM), pl.BlockSpec(code.shape, lambda: (0, 0))],
        out_specs=pl.BlockSpec((NSA_HEADS,) + code.shape, lambda: (0, 0, 0)),
        out_shape=jax.ShapeDtypeStruct((NSA_HEADS,) + code.shape, F32),
    )(rel_bias, code)


def _window_codes():
    ql = np.arange(64)[:, None]
    j = np.arange(WIN_KEYS)[None, :]
    d = WIN_KEYS - 64 + ql - j
    return np.where((d >= 0) & (d <= WINDOW), _bucket_np(d), -1).astype(np.int32)


def _cmp_codes():
    delta = np.arange(8)[:, None]
    ql = np.arange(64)[None, :]
    d = delta * CMP_BLOCK + ql - (CMP_BLOCK - 1)
    return np.where(d >= 0, _bucket_np(d), -1).astype(np.int32)


FAR_TILE = 512


def _nsa_prompt_kernel(qn_ref, gates_ref, cmp_ref, kv_ref, oh_ref, wb_ref, nb_ref, ct_ref, amat_ref, o_ref):
    i = pl.program_id(1)
    nc = cmp_ref.shape[1]
    gr = NSA_GROUP * 64
    rows = 2 * gr
    q = qn_ref[0].astype(F32)
    gts = gates_ref[0]
    kc = cmp_ref[0, :, 0:128].astype(BF16)
    vc = cmp_ref[0, :, 128:256].astype(BF16)
    lane = lax.broadcasted_iota(jnp.int32, (64, LANES), 1)
    half = [lane < 64, lane >= 64]
    qall = jnp.concatenate([jnp.where(half[g], q[:, h * LANES:(h + 1) * LANES], 0.0)
                            for g in range(2) for h in range(NSA_GROUP)], axis=0).astype(BF16)

    blk_r = lax.broadcasted_iota(jnp.int32, (nc, rows), 0)
    ct = ct_ref[...]
    bias = jnp.where(blk_r == i, ct[0:1], jnp.where(blk_r == i - 1, ct[1:2], jnp.where(
        blk_r == i - 2, ct[2:3], jnp.where(blk_r > i, -BIG, 0.0))))
    lc = _nt(kc, qall) + bias
    e = jnp.exp(lc - jnp.max(lc, axis=0, keepdims=True))
    p = jnp.where(bias > -0.5 * BIG, e / jnp.sum(e, axis=0, keepdims=True), 0.0)
    o_c = _tn(p.astype(BF16), vc)
    imp = []
    for g in range(2):
        s3 = p[:, g * gr:g * gr + 128] + p[:, g * gr + 128:g * gr + 256] + p[:, g * gr + 256:g * gr + 384]
        imp.append(s3 + pltpu.roll(s3, 64, axis=1))
    lane_c = lax.broadcasted_iota(jnp.int32, (nc, LANES), 1)
    blk = lax.broadcasted_iota(jnp.int32, (nc, LANES), 0)
    impp = jnp.where(lane_c < 64, imp[0], imp[1])
    forced = (blk == 0) | (blk == i) | (blk == i - 1)
    score = jnp.where(blk <= i, jnp.where(forced, FORCED_SCORE, impp), -1.0)
    sc8 = [score[r * 8:(r + 1) * 8] for r in range(nc // 8)]
    blk8 = lax.broadcasted_iota(jnp.int32, (8, LANES), 0)
    rank8 = [jnp.zeros((8, LANES), F32) for _ in sc8]
    for cp in range(nc):
        row = score[cp:cp + 1, :]
        for r in range(nc // 8):
            if r * 8 > cp:
                beats = row >= sc8[r]
            elif r * 8 + 7 < cp:
                beats = row > sc8[r]
            else:
                beats = (row > sc8[r]) | ((row == sc8[r]) & (blk8 + r * 8 > cp))
            rank8[r] = rank8[r] + jnp.where(beats, 1.0, 0.0)
    rank = jnp.concatenate(rank8, axis=0)
    notsel =jnp.where((rank < float(min(SLC_TOPN, nc))) & (score >= 0.0), 0.0, 1.0)
    notsel_far = jnp.where(blk > i - 3, 1.0, notsel)
    tail = jnp.where(lax.broadcasted_iota(jnp.int32, (LANES - nc, LANES), 0) == 64 - nc, 1.0, 0.0)
    ns_near = jnp.concatenate([notsel, tail], axis=0).astype(BF16)
    ns_far = jnp.concatenate([notsel_far, tail], axis=0).astype(BF16)
    a = amat_ref[...]
    lhs_near = jnp.concatenate([qall, (_nt(a, ns_near) * -BIG).astype(BF16)], axis=1)
    lhs_far = jnp.concatenate([qall, (_nt(a, ns_far) * -BIG).astype(BF16)], axis=1)
    lane_r = lax.broadcasted_iota(jnp.int32, (rows, LANES), 1)
    lhs_win = jnp.concatenate([qall, jnp.where(lane_r == 64, -BIG, 0.0).astype(BF16)], axis=1)

    def keys_aug(r0, n, c0):
        return jnp.concatenate([kv_ref[0, pl.ds(r0, n), c0:c0 + LANES], oh_ref[pl.ds(r0, n), :]], axis=1)

    def fold(x, op):
        return functools.reduce(op, [x[:, c * LANES:(c + 1) * LANES] for c in range(x.shape[1] // LANES)])

    ntiles = (i + 5) // (FAR_TILE // 64)
    r0n = pl.multiple_of(PAD_ROWS + (i - 3) * 64, 64)
    s_near = _nt(lhs_near, keys_aug(r0n, NEAR_KEYS, 0)) + nb_ref[...]

    def tile_row(t):
        return pl.multiple_of(PAD_ROWS + t * FAR_TILE, LANES)

    def far_logits(t):
        return _nt(lhs_far, keys_aug(tile_row(t), FAR_TILE, 0))

    def absorb(s, v, m, lrun, acc):
        m_new = jnp.maximum(m, jnp.max(fold(s, jnp.maximum), axis=1, keepdims=True))
        alpha = jnp.exp(m - m_new)
        e = jnp.exp(s - m_new)
        return m_new, alpha * lrun + fold(e, jnp.add), alpha * acc + _dot(e.astype(BF16), v)

    def far_tile(t, carry):
        return absorb(far_logits(t), kv_ref[0, pl.ds(tile_row(t), FAR_TILE), 128:256], *carry)

    def far_pair(u, carry):
        sa, sb = far_logits(2 * u), far_logits(2 * u + 1)
        carry = absorb(sa, kv_ref[0, pl.ds(tile_row(2 * u), FAR_TILE), 128:256], *carry)
        return absorb(sb, kv_ref[0, pl.ds(tile_row(2 * u + 1), FAR_TILE), 128:256], *carry)

    zero = jnp.zeros((rows, LANES), F32)
    carry = lax.fori_loop(0, ntiles // 2, far_pair, (jnp.full((rows, 1), -BIG, F32), zero, zero))
    carry = lax.cond(ntiles % 2 == 1, lambda c: far_tile(ntiles - 1, c), lambda c: c, carry)
    _, lrun, acc = absorb(s_near, kv_ref[0, pl.ds(r0n, NEAR_KEYS), 128:256], *carry)
    o_s = acc / jnp.sum(lrun, axis=1, keepdims=True)

    r0w = pl.multiple_of(PAD_ROWS + (i - 9) * 64, 64)
    s = _nt(lhs_win, keys_aug(r0w, WIN_KEYS, 256)) + wb_ref[...]
    e = jnp.exp(s - jnp.max(fold(s, jnp.maximum), axis=1, keepdims=True))
    o_w = _dot(e.astype(BF16), kv_ref[0, pl.ds(r0w, WIN_KEYS), 384:512]) / jnp.sum(fold(e, jnp.add), axis=1,
                                                                                 keepdims=True)

    low = lax.broadcasted_iota(jnp.int32, (64, LANES), 1) < 64
    for h in range(NSA_GROUP):
        per_group = []
        for g in range(2):
            c0 = (g * NSA_GROUP + h) * 3
            r = slice(g * gr + h * 64, g * gr + (h + 1) * 64)
            per_group.append(gts[:, c0:c0 + 1] * o_c[r] + gts[:, c0 + 1:c0 + 2] * o_s[r]
                             + gts[:, c0 + 2:c0 + 3] * o_w[r])
        o_ref[0, :, h * LANES:(h + 1) * LANES] = jnp.where(low, per_group[0], per_group[1]).astype(BF16)


def _nsa_prompt_attn(qn, gates, cmp, kvatt, oh, wb, nb, ct, amat):
    b, l, _ = qn.shape
    nc = l // 64
    full = lambda a: pl.BlockSpec(a.shape, lambda bi, i: (0,) * a.ndim)
    return pl.pallas_call(
        _nsa_prompt_kernel,
        grid=(b, nc),
        in_specs=[pl.BlockSpec((1, 64, 768), lambda bi, i: (bi, i, 0)),
                  pl.BlockSpec((1, 64, 128), lambda bi, i: (bi, i, 0)),
                  pl.BlockSpec((1, nc, 256), lambda bi, i: (bi, 0, 0)),
                  pl.BlockSpec((1, PAD_ROWS + l, 512), lambda bi, i: (bi, 0, 0)),
                  full(oh), full(wb), full(nb), full(ct), full(amat)],
        out_specs=pl.BlockSpec((1, 64, 768), lambda bi, i: (bi, i, 0)),
        out_shape=jax.ShapeDtypeStruct((b, l, 768), BF16),
        compiler_params=_cparams("parallel", "arbitrary"),
    )(qn, gates, cmp, kvatt, oh, wb, nb, ct, amat)


def _mem_attn_kernel(q_ref, kv_ref, o_ref):
    q = q_ref[0].astype(F32)
    k = kv_ref[0, :, 0:256].astype(BF16)
    v = kv_ref[0, :, 256:512].astype(BF16)
    head = lax.broadcasted_iota(jnp.int32, q.shape, 1) // HEAD_DIM
    out = jnp.zeros(q.shape, F32)
    for h in range(XATTN_HEADS):
        qh = jnp.where(head == h, q, 0.0).astype(BF16)
        s = _nt(qh, k)
        e = jnp.exp(s - jnp.max(s, axis=1, keepdims=True))
        p = e / jnp.sum(e, axis=1, keepdims=True)
        out = out + jnp.where(head == h, _dot(p.astype(BF16), v), 0.0)
    o_ref[0] = out.astype(BF16)


def _mem_attn(xqn, mkv, tq):
    b, l, _ = xqn.shape
    nm = mkv.shape[1]
    return pl.pallas_call(
        _mem_attn_kernel,
        grid=(b, l // tq),
        in_specs=[pl.BlockSpec((1, tq, 256), lambda bi, i: (bi, i, 0)),
                  pl.BlockSpec((1, nm, 512), lambda bi, i: (bi, 0, 0))],
        out_specs=pl.BlockSpec((1, tq, 256), lambda bi, i: (bi, i, 0)),
        out_shape=jax.ShapeDtypeStruct((b, l, 256), BF16),
        compiler_params=_cparams("parallel", "parallel"),
    )(xqn, mkv)


def _pool_kernel(u_ref, prev_ref, w_ref, scale_ref, o_ref, carry_ref, *, pos0):
    i = pl.program_id(1)
    tp = u_ref.shape[1]

    @pl.when(i == 0)
    def _():
        carry_ref[...] = prev_ref[0]

    u = u_ref[0]
    ext = jnp.concatenate([carry_ref[...], u], axis=0)
    carry_ref[...] = ext[tp:tp + 16]
    s2 = ext + pltpu.roll(ext, 1, axis=0)
    s4 = s2 + pltpu.roll(s2, 2, axis=0)
    s8 = s4 + pltpu.roll(s4, 4, axis=0)
    s16 = s8 + pltpu.roll(s8, 8, axis=0)
    lane = lax.broadcasted_iota(jnp.int32, (tp, POOL_WIDTH), 1)
    pos1 = (pos0 + 1 + i * tp + lax.broadcasted_iota(jnp.int32, (tp, POOL_WIDTH), 0)).astype(F32)
    grp = lane // POOL_GROUP_DIM
    ssum = jnp.where(grp == 0, s2[16:], jnp.where(grp == 1, s4[16:], jnp.where(grp == 2, s8[16:], s16[16:])))
    win = jnp.where(grp == 0, 2.0, jnp.where(grp == 1, 4.0, jnp.where(grp == 2, 8.0, 16.0)))
    pooled = ssum / jnp.minimum(win, pos1) - u
    o_ref[0] = (_dot(pooled.astype(BF16), w_ref[...]) * scale_ref[...]).astype(BF16)


def _pool_mix(u, prev16, wbd, scale, tp, pos0):
    b, l, _ = u.shape
    full = lambda a: pl.BlockSpec(a.shape, lambda bi, i: (0,) * a.ndim)
    return pl.pallas_call(
        functools.partial(_pool_kernel, pos0=pos0),
        grid=(b, l // tp),
        in_specs=[pl.BlockSpec((1, tp, POOL_WIDTH), lambda bi, i: (bi, i, 0)),
                  pl.BlockSpec((1, 16, POOL_WIDTH), lambda bi, i: (bi, 0, 0)),
                  full(wbd), full(scale)],
        out_specs=pl.BlockSpec((1, tp, POOL_WIDTH), lambda bi, i: (bi, i, 0)),
        out_shape=jax.ShapeDtypeStruct((b, l, POOL_WIDTH), BF16),
        scratch_shapes=[pltpu.VMEM((16, POOL_WIDTH), F32)],
        compiler_params=_cparams("parallel", "arbitrary"),
    )(u, prev16, wbd, scale)


def _outproj_kernel(y_ref, mix_ref, att_ref, wm_ref, wa_ref, g_ref, o_ref, h_ref):
    y = y_ref[...] + _dot(mix_ref[...], wm_ref[...]) + _dot(att_ref[...], wa_ref[...])
    o_ref[...] = y
    h_ref[...] = _rms_rows(y, g_ref[...]).astype(BF16)


def _outproj(y, mix, att, wm, wa, g_ffn, tm):
    t = y.shape[0]
    full = lambda a: pl.BlockSpec(a.shape, lambda i: (0,) * a.ndim)
    row = lambda n: pl.BlockSpec((tm, n), lambda i: (i, 0))
    return pl.pallas_call(
        _outproj_kernel,
        grid=(t // tm,),
        in_specs=[row(D_MODEL), row(768), row(256), full(wm), full(wa), full(g_ffn)],
        out_specs=[row(D_MODEL), row(D_MODEL)],
        out_shape=[jax.ShapeDtypeStruct((t, D_MODEL), F32), jax.ShapeDtypeStruct((t, D_MODEL), BF16)],
        compiler_params=_cparams("parallel"),
    )(y, mix, att, wm, wa, g_ffn)


def _ffn_half(h, w_ref, cw_ref, cb_ref, ext_ref, prev8, fix):
    tm = h.shape[0]
    u = _dot(h, w_ref[...])
    ext_ref[0:8, :] = prev8
    ext_ref[8:8 + tm, :] = u
    s1 = ext_ref[7:7 + tm, :]
    s2 = ext_ref[6:6 + tm, :]
    if fix is not None:
        rowm, p1, p2 = fix
        s1 = jnp.where(rowm >= 1, s1, p1)
        s2 = jnp.where(rowm >= 2, s2, p2)
    cw = cw_ref[...]
    return u, cb_ref[...] + cw[0:1] * s2 + cw[1:2] * s1 + cw[2:3] * u


FF_SUB = (0, 1024, 2048, D_FF)


def _mix_ffn_kernel(y_ref, mix_ref, att_ref, wm_ref, wo_ref, g_ref, cw_ref, cb_ref, wup_hbm, wd_hbm,
                    o_ref, sa_ref, sv_ref, carry_ref, ext_ref, wup_ref, wd_ref, sem, *, tiles_per_seq):
    j = pl.program_id(0)
    tm = y_ref.shape[0]

    @pl.when(j == 0)
    def _():
        up = pltpu.make_async_copy(wup_hbm, wup_ref, sem.at[0])
        down = pltpu.make_async_copy(wd_hbm, wd_ref, sem.at[1])
        up.start()
        down.start()
        up.wait()
        down.wait()

    @pl.when((j % tiles_per_seq) == 0)
    def _():
        carry_ref[...] = jnp.zeros_like(carry_ref)

    y1 = y_ref[...] + _dot(mix_ref[...], wm_ref[...]) + _dot(att_ref[...], wo_ref[...])
    h = _rms_rows(y1, g_ref[...]).astype(BF16)
    ffn = None
    for c0, c1 in zip(FF_SUB[:-1], FF_SUB[1:]):
        w = c1 - c0
        conv = []
        for half, st_ref in ((0, sa_ref), (1, sv_ref)):
            cols = slice(half * D_FF + c0, half * D_FF + c1)
            u = _dot(h, wup_ref[:, cols])
            ext_ref[half, 0:8, 0:w] = carry_ref[:, cols]
            ext_ref[half, 8:8 + tm, 0:w] = u
            s1 = ext_ref[half, 7:7 + tm, 0:w]
            s2 = ext_ref[half, 6:6 + tm, 0:w]
            cw = cw_ref[:, cols]
            conv.append(cb_ref[:, cols] + cw[0:1] * s2 + cw[1:2] * s1 + cw[2:3] * u)
            carry_ref[:, cols] = u[tm - 8:tm]
            st_ref[0, :, c0:c1] = u[tm - 8:tm]
        d = _dot((jax.nn.silu(conv[0]) * conv[1]).astype(BF16), wd_ref[c0:c1, :])
        ffn = d if ffn is None else ffn + d
    o_ref[...] = y1 + ffn


def _mix_ffn_prompt(y, mix, att, wm, wo, g, w_up, conv_w, conv_b, w_down, tm, seq_len):
    t = y.shape[0]
    nseq = t // seq_len
    tps = seq_len // tm
    full = lambda a: pl.BlockSpec(a.shape, lambda j: (0,) * a.ndim)
    row = lambda n: pl.BlockSpec((tm, n), lambda j: (j, 0))
    hbm = pl.BlockSpec(memory_space=pl.ANY)
    y_out, sa, sv = pl.pallas_call(
        functools.partial(_mix_ffn_kernel, tiles_per_seq=tps),
        grid=(t // tm,),
        in_specs=[row(D_MODEL), row(768), row(256), full(wm), full(wo), full(g), full(conv_w), full(conv_b),
                  hbm, hbm],
        out_specs=[row(D_MODEL), pl.BlockSpec((1, 8, D_FF), lambda j: (j, 0, 0)),
                   pl.BlockSpec((1, 8, D_FF), lambda j: (j, 0, 0))],
        out_shape=[jax.ShapeDtypeStruct((t, D_MODEL), F32), jax.ShapeDtypeStruct((t // tm, 8, D_FF), F32),
                   jax.ShapeDtypeStruct((t // tm, 8, D_FF), F32)],
        scratch_shapes=[pltpu.VMEM((8, 2 * D_FF), F32), pltpu.VMEM((2, tm + 8, max(b - a for a, b in zip(FF_SUB[:-1], FF_SUB[1:]))), F32),
                        pltpu.VMEM(w_up.shape, BF16), pltpu.VMEM(w_down.shape, BF16),
                        pltpu.SemaphoreType.DMA((2,))],
        compiler_params=_cparams("arbitrary"),
    )(y, mix, att, wm, wo, g, conv_w, conv_b, w_up, w_down)
    last = lambda s: s.reshape(nseq, tps, 8, D_FF)[:, tps - 1, 6:8]
    return y_out, jnp.concatenate([last(sa), last(sv)], axis=-1)


def _ffn_sample_kernel(y_ref, g_ref, wa_ref, wv_ref, cwa_ref, cwv_ref, cba_ref, cbv_ref, wd_ref,
                       p1a_ref, p1v_ref, p2a_ref, p2v_ref, o_ref, ua_ref, uv_ref, acc_ref, ext_ref):
    c = pl.program_id(0)
    h = _rms_rows(y_ref[...], g_ref[...]).astype(BF16)
    tm = h.shape[0]
    rowm = lax.broadcasted_iota(jnp.int32, (tm, wa_ref.shape[1]), 0) % 8
    zero8 = jnp.zeros((8, wa_ref.shape[1]), F32)
    ua, ya = _ffn_half(h, wa_ref, cwa_ref, cba_ref, ext_ref, zero8, (rowm, p1a_ref[...], p2a_ref[...]))
    uv, yv = _ffn_half(h, wv_ref, cwv_ref, cbv_ref, ext_ref, zero8, (rowm, p1v_ref[...], p2v_ref[...]))
    ua_ref[...] = ua
    uv_ref[...] = uv
    d = _dot((jax.nn.silu(ya) * yv).astype(BF16), wd_ref[...])

    @pl.when(c == 0)
    def _():
        acc_ref[...] = d

    @pl.when(c == pl.num_programs(0) - 1)
    def _():
        o_ref[...] = y_ref[...] + acc_ref[...] + d


def _ffn_sample(y, g, w_up, conv_w, conv_b, w_down, state):
    t = y.shape[0]
    nseq = t // 8
    fc = FF_CHUNK
    ncf = D_FF // fc
    z = jnp.zeros((nseq, 1, 2 * D_FF), F32)
    p1 = jnp.concatenate([state[:, 1:2]] + [z] * 7, axis=1).reshape(t, 2 * D_FF)
    p2 = jnp.concatenate([state[:, 0:1], state[:, 1:2]] + [z] * 6, axis=1).reshape(t, 2 * D_FF)
    full = lambda a: pl.BlockSpec(a.shape, lambda c: (0,) * a.ndim)
    ca = lambda r: pl.BlockSpec((r, fc), lambda c: (0, c))
    cv = lambda r: pl.BlockSpec((r, fc), lambda c: (0, ncf + c))
    y_out, ua, uv = pl.pallas_call(
        _ffn_sample_kernel,
        grid=(ncf,),
        in_specs=[full(y), full(g), ca(D_MODEL), cv(D_MODEL), ca(3), cv(3), ca(1), cv(1),
                  pl.BlockSpec((fc, D_MODEL), lambda c: (c, 0)), ca(t), cv(t), ca(t), cv(t)],
        out_specs=[pl.BlockSpec((t, D_MODEL), lambda c: (0, 0)), ca(t), ca(t)],
        out_shape=[jax.ShapeDtypeStruct((t, D_MODEL), F32), jax.ShapeDtypeStruct((t, D_FF), F32),
                   jax.ShapeDtypeStruct((t, D_FF), F32)],
        scratch_shapes=[pltpu.VMEM((t, D_MODEL), F32), pltpu.VMEM((t + 8, fc), F32)],
        compiler_params=_cparams("arbitrary"),
    )(y, g, w_up, w_up, conv_w, conv_w, conv_b, conv_b, w_down, p1, p1, p2, p2)
    u = jnp.concatenate([ua, uv], axis=-1).reshape(nseq, 8, 2 * D_FF)
    return y_out, u[:, 6:8]


KEY_CHUNK = 1024
NEW_PAD = 16


def _nsa_sample_kernel(pt_ref, qn_ref, gates_ref, rows_ref, winc_ref, winn_ref, scb_ref, sbt_ref, wbt_ref,
                       rmat_ref, eloc_ref, cmp_hbm, cache_hbm, o_ref, kvbuf, kcv, s_ref, m3_ref, sem, sem2):
    b = pl.program_id(0)
    npages = pt_ref.shape[1]
    ncs = 2 * npages
    nkeys = npages * PAGE_SIZE
    nch = nkeys // KEY_CHUNK
    ppc = KEY_CHUNK // PAGE_SIZE
    wbuf = winc_ref.shape[-1]

    slot = b % 2

    def page_copy(seq, p, sl):
        return pltpu.make_async_copy(cache_hbm.at[pt_ref[seq, p], pl.ds(2, 2)], kvbuf.at[sl, p], sem.at[sl])

    def cmp_copy(p):
        return pltpu.make_async_copy(cmp_hbm.at[pt_ref[b, p]], kcv.at[p], sem2)

    def start_cmp(p, c):
        cmp_copy(p).start()
        return c

    lax.fori_loop(0, npages, start_cmp, 0)

    @pl.when(b == 0)
    def _():
        def start_first(p, c):
            page_copy(0, p, 0).start()
            return c

        lax.fori_loop(0, npages, start_first, 0)

    @pl.when(b + 1 < pl.num_programs(0))
    def _():
        def start_next(p, c):
            page_copy(b + 1, p, 1 - slot).start()
            return c

        lax.fori_loop(0, npages, start_next, 0)

    q = qn_ref[0].astype(F32)
    gts = gates_ref[0]
    lane8 = lax.broadcasted_iota(jnp.int32, (8, LANES), 1)
    half = [lane8 < 64, lane8 >= 64]
    pieces = [jnp.where(half[g], q[:, h * LANES:(h + 1) * LANES], 0.0) for g in range(2) for h in range(NSA_GROUP)]
    qrows = jnp.concatenate(pieces + [jnp.zeros((32, LANES), F32)], axis=0).astype(BF16)

    def wait_cmp(p, c):
        cmp_copy(p).wait()
        return c

    lax.fori_loop(0, npages, wait_cmp, 0)

    kc = jnp.concatenate([kcv[:, 0, 0:128], kcv[:, 1, 0:128]], axis=0).astype(BF16)
    vc = jnp.concatenate([kcv[:, 0, 128:256], kcv[:, 1, 128:256]], axis=0).astype(BF16)
    lc = _nt(kc, qrows) + scb_ref[...]
    e = jnp.exp(lc - jnp.max(lc, axis=0, keepdims=True))
    p_c = e / jnp.sum(e, axis=0, keepdims=True)
    o_c = _tn(p_c.astype(BF16), vc)
    hi = p_c.astype(BF16)
    lo = (p_c - hi.astype(F32)).astype(BF16)
    imp = _dot(hi, rmat_ref[...]) + _dot(lo, rmat_ref[...])

    def blk_of(r):
        return jnp.where(r < npages, 2 * r, jnp.where(r < ncs, 2 * (r - npages) + 1, r))

    blk = blk_of(lax.broadcasted_iota(jnp.int32, (ncs + 8, LANES), 0))
    impx = jnp.concatenate([imp, jnp.zeros((8, LANES), F32)], axis=0)
    forced = (blk == 0) | (blk == ncs) | (blk == ncs - 1)
    score = jnp.where(blk <= ncs, jnp.where(forced, FORCED_SCORE, impx), -1.0)
    blkf = blk.astype(F32)
    work, picked = score, jnp.zeros((ncs + 8, LANES), F32)
    for _ in range(SLC_TOPN):
        top = jnp.max(work, axis=0, keepdims=True)
        first = jnp.min(jnp.where(work == top, blkf, float(2 * ncs + 16)), axis=0, keepdims=True)
        hit = blkf == first
        picked = jnp.where(hit, 1.0, picked)
        work = jnp.where(hit, -2.0, work)
    madd = jnp.where((picked > 0.0) & (score >= 0.0), 0.0, -BIG)
    eye = jnp.where(lax.broadcasted_iota(jnp.int32, (LANES, LANES), 0)
                    == lax.broadcasted_iota(jnp.int32, (LANES, LANES), 1), 1.0, 0.0).astype(BF16)
    hb = ppc
    for t in range(nch):
        mt = jnp.concatenate([madd[t * hb:(t + 1) * hb], madd[npages + t * hb:npages + (t + 1) * hb],
                              jnp.zeros((LANES - 2 * hb, LANES), F32)], axis=0).astype(BF16)
        m3_ref[t] = _nt(eye, mt).astype(BF16)

    def wait_page(p, c):
        page_copy(b, p, slot).wait()
        return c

    lax.fori_loop(0, npages, wait_page, 0)

    eloc = eloc_ref[...]
    pad8 = jnp.zeros((NEW_PAD - 8, LANES), F32)
    knew = jnp.concatenate([rows_ref[0, :, 256:384], pad8], axis=0).astype(BF16)
    vnew = jnp.concatenate([rows_ref[0, :, 384:512], pad8], axis=0).astype(BF16)
    kwn = jnp.concatenate([winn_ref[0, :, 0:128], pad8], axis=0).astype(BF16)
    vwn = jnp.concatenate([winn_ref[0, :, 128:256], pad8], axis=0).astype(BF16)
    kwt = winc_ref[0, 0].reshape(LANES, wbuf).astype(BF16)
    vwt = winc_ref[0, 1].reshape(LANES, wbuf).astype(BF16)
    near_pad = jnp.concatenate([jnp.zeros((LANES, KEY_CHUNK - PAGE_SIZE), F32), sbt_ref[:, 0:PAGE_SIZE]], axis=1)
    low = lax.broadcasted_iota(jnp.int32, (8, LANES), 1) < 64

    def page_rows(t, comp):
        tiles = [kvbuf[slot, t * ppc + pp, comp].reshape(LANES, PAGE_SIZE) for pp in range(ppc)]
        return jnp.concatenate(tiles, axis=1).astype(BF16)

    def s_body(t, m):
        lhs = jnp.concatenate([m3_ref[t], qrows], axis=1)
        s = _dot(lhs, jnp.concatenate([eloc, page_rows(t, 0)], axis=0))
        s = s + jnp.where(t == nch - 1, 1.0, 0.0) * near_pad
        s_ref[t] = s
        return jnp.maximum(m, jnp.max(s, axis=1, keepdims=True))

    s_new = _nt(qrows, knew) + sbt_ref[:, PAGE_SIZE:PAGE_SIZE + NEW_PAD]
    unroll = math.gcd(nch, 4)
    m = lax.fori_loop(0, nch, s_body, jnp.max(s_new, axis=1, keepdims=True), unroll=unroll)
    e_new = jnp.exp(s_new - m)

    def pv_body(t, carry):
        l, acc = carry
        et = jnp.exp(s_ref[t] - m)
        return l + jnp.sum(et, axis=1, keepdims=True), acc + _nt(et.astype(BF16), page_rows(t, 1))

    l, acc = lax.fori_loop(0, nch, pv_body, (jnp.sum(e_new, axis=1, keepdims=True),
                                             _dot(e_new.astype(BF16), vnew)), unroll=unroll)
    o_s = acc / l

    sw = _dot(qrows, kwt) + wbt_ref[:, 0:wbuf]
    swn = _nt(qrows, kwn) + wbt_ref[:, wbuf:wbuf + NEW_PAD]
    mw = jnp.maximum(jnp.max(sw, axis=1, keepdims=True), jnp.max(swn, axis=1, keepdims=True))
    ew = jnp.exp(sw - mw)
    ewn = jnp.exp(swn - mw)
    o_w = (_nt(ew.astype(BF16), vwt) + _dot(ewn.astype(BF16), vwn)) / (
        jnp.sum(ew, axis=1, keepdims=True) + jnp.sum(ewn, axis=1, keepdims=True))

    def gate_col(br):
        cols = [gts[:, hh * 3 + br:hh * 3 + br + 1] for hh in range(NSA_HEADS)]
        return jnp.concatenate(cols + [jnp.zeros((32, 1), F32)], axis=0)

    o = gate_col(0) * o_c + gate_col(1) * o_s + gate_col(2) * o_w
    for h in range(NSA_GROUP):
        r0, r1 = h * 8, (NSA_GROUP + h) * 8
        o_ref[0, :, h * LANES:(h + 1) * LANES] = jnp.where(low, o[r0:r0 + 8], o[r1:r1 + 8]).astype(BF16)


def _nsa_sample_attn(page_table, qn, gates, rows, winc, winn, scb, sbt, wbt, rmat, eloc, cmp_phys, cache):
    b, npages = page_table.shape
    ncs = 2 * npages
    nkeys = npages * PAGE_SIZE
    assert nkeys % KEY_CHUNK == 0
    nch = nkeys // KEY_CHUNK
    full = lambda a: pl.BlockSpec(a.shape, lambda bi, pt: (0,) * a.ndim)
    per_b = lambda a: pl.BlockSpec((1,) + a.shape[1:], lambda bi, pt: (bi,) + (0,) * (a.ndim - 1))
    grid_spec = pltpu.PrefetchScalarGridSpec(
        num_scalar_prefetch=1,
        grid=(b,),
        in_specs=[per_b(qn), per_b(gates), per_b(rows), per_b(winc), per_b(winn),
                  full(scb), full(sbt), full(wbt), full(rmat), full(eloc),
                  pl.BlockSpec(memory_space=pl.ANY), pl.BlockSpec(memory_space=pl.ANY)],
        out_specs=pl.BlockSpec((1, 8, 768), lambda bi, pt: (bi, 0, 0)),
        scratch_shapes=[pltpu.VMEM((2, npages, 2, 2, 64, PAGE_SIZE), F32), pltpu.VMEM((npages, 2, 256), F32),
                        pltpu.VMEM((nch, LANES, KEY_CHUNK), F32), pltpu.VMEM((nch, LANES, LANES), BF16),
                        pltpu.SemaphoreType.DMA((2,)), pltpu.SemaphoreType.DMA(())],
    )
    return pl.pallas_call(
        _nsa_sample_kernel,
        grid_spec=grid_spec,
        out_shape=jax.ShapeDtypeStruct((b, 8, 768), BF16),
        compiler_params=_cparams("arbitrary"),
    )(page_table, qn, gates, rows, winc, winn, scb, sbt, wbt, rmat, eloc, cmp_phys, cache)


def _block_diag2(w):
    z = jnp.zeros_like(w)
    return jnp.concatenate([jnp.concatenate([w, z], axis=-1), jnp.concatenate([z, w], axis=-1)], axis=-2)


def _q_slot_perm():
    return np.concatenate([np.r_[h * 64:(h + 1) * 64, (NSA_GROUP + h) * 64:(NSA_GROUP + h + 1) * 64]
                           for h in range(NSA_GROUP)])


def _sample_codes(ncs, wbuf):
    ql = np.arange(8)[None, :]
    c = np.arange(ncs)[:, None]
    scb = _bucket_np((ncs - c) * CMP_BLOCK + ql - (CMP_BLOCK - 1))
    kap = np.arange(PAGE_SIZE + NEW_PAD)[:, None]
    d = np.where(kap < PAGE_SIZE, PAGE_SIZE + ql - kap, ql - (kap - PAGE_SIZE))
    sbt = np.where((d >= 0) & (kap < PAGE_SIZE + 8), _bucket_np(d), -1)
    j = np.arange(wbuf + NEW_PAD)[:, None]
    d = wbuf + ql - j
    wbt = np.where((d >= 0) & (d <= WINDOW) & (j < wbuf + 8), _bucket_np(d), -1)
    return np.concatenate([scb, sbt, wbt], axis=0).astype(np.int32)


def kernel(x_prompt, x_sample, cache_nsa_kv, cache_nsa_win, cache_mem_kv, state_pool, state_ffn_conv,
           page_table, mem_prompt, rel_bias, g_mix, g_mem, g_ffn, w_in_nsa, q_gain_nsa, k_gain_nsa,
           cmp_pos, cmp_w1, cmp_b1, cmp_w2, cmp_b2, w_in_pool, w_pool_grp, pool_scale, w_mem_kv,
           xq_gain, xk_gain, w_out, w_up, conv_w, conv_b, w_down):
    bp, lp, _ = x_prompt.shape
    bs, ls, _ = x_sample.shape
    assert ls == 8 and lp % 512 == 0 and lp // 64 <= 64
    npages = page_table.shape[1]
    past_len = npages * PAGE_SIZE
    n_phys = cache_nsa_kv.shape[1]
    wbuf = cache_nsa_win.shape[2]
    nm = mem_prompt.shape[1]
    tp_, ts_ = bp * lp, bs * ls
    nc = lp // 64
    ncs = 2 * npages
    scale = HEAD_DIM ** -0.5
    tile2 = lambda v: jnp.tile(v, 2)[None, :]

    a = np.arange(LANES)
    seg = jnp.asarray(a[:, None] // 64 == a[None, :] // 64, BF16)
    perm = _q_slot_perm()

    wb = _bias_table(rel_bias, _window_codes()).reshape(NSA_HEADS * 64, WIN_KEYS)
    nbias = jnp.concatenate([jnp.full((NSA_HEADS * 64, 64), -BIG, F32), wb[:, WIN_KEYS - 192:]], axis=1)
    ct = _bias_table(rel_bias, _cmp_codes()).transpose(1, 0, 2).reshape(8, NSA_HEADS * 64)
    amat = np.zeros((2, NSA_GROUP * 64, LANES), np.float32)
    for g in range(2):
        for h in range(NSA_GROUP):
            amat[g, h * 64 + np.arange(64), g * 64 + np.arange(64)] = 1.0
    amat = jnp.asarray(amat.reshape(NSA_HEADS * 64, LANES), BF16)
    r = np.arange(PAD_ROWS + lp)
    oh_np = np.zeros((PAD_ROWS + lp, LANES), np.float32)
    oh_np[r, np.where(r < PAD_ROWS, 64, (r - PAD_ROWS) // 64)] = 1.0
    oh = jnp.asarray(oh_np, BF16)
    stab = _bias_table(rel_bias, _sample_codes(ncs, wbuf))
    stab = jnp.pad(stab.transpose(1, 0, 2).reshape(-1, NSA_HEADS * 8), ((0, 0), (0, LANES - NSA_HEADS * 8)))
    ns = PAGE_SIZE + NEW_PAD
    scb, sbt, wbt = stab[:ncs], stab[ncs:ncs + ns].T, stab[ncs + ns:].T
    scb = jnp.concatenate([scb[0::2], scb[1::2]], axis=0)
    ppc = KEY_CHUNK // PAGE_SIZE
    eloc_np = np.zeros((LANES, KEY_CHUNK), np.float32)
    for pp in range(ppc):
        eloc_np[pp, pp * PAGE_SIZE:pp * PAGE_SIZE + 64] = 1.0
        eloc_np[ppc + pp, pp * PAGE_SIZE + 64:(pp + 1) * PAGE_SIZE] = 1.0
    eloc = jnp.asarray(eloc_np, BF16)
    lam = np.arange(LANES)
    rm = (lam[:, None] < 96) & (lam[None, :] < 96) & (lam[:, None] // 48 == lam[None, :] // 48) \
        & (lam[:, None] % 8 == lam[None, :] % 8)
    rmat = jnp.asarray(rm, BF16)

    yp = x_prompt.reshape(tp_, D_MODEL)
    ys = x_sample.reshape(ts_, D_MODEL)
    tm = 512
    outs = {k: [] for k in ("kv_p", "kv_s", "win_p", "win_s", "mem_p", "pool_p", "pool_s", "conv_p", "conv_s")}
    depth = g_mix.shape[0]
    for i in range(depth):
        j = i // 2
        wo = w_out[i]
        xg = tile2(xq_gain[i]) * scale
        mkv_p = _memkv(mem_prompt.reshape(bp * nm, D_MODEL), g_mem[i][None], w_mem_kv[i].astype(BF16), seg,
                       tile2(xk_gain[i]), min(tm, bp * nm)).reshape(bp, nm, 2 * XATTN_WIDTH)
        outs["mem_p"].append(mkv_p.reshape(bp, nm, 2, XATTN_HEADS, HEAD_DIM))
        mkv_s = cache_mem_kv[i].reshape(bs, nm, 2 * XATTN_WIDTH)
        if i % 2 == 0:
            w = w_in_nsa[j]
            w = jnp.concatenate([w[:, perm], w[:, 768:1536], w[:, 1572:1828], w[:, 1536:1572],
                                 jnp.zeros((D_MODEL, NSA_IN_COLS - 1828), F32)], axis=1).astype(BF16)
            qg = tile2(q_gain_nsa[j]) * scale
            kg = tile2(k_gain_nsa[j])
            wo_mix = wo[:768][perm].astype(BF16)
            w2 = cmp_w2[j]
            w2sel = jnp.zeros((2, 2, 2, 2 * CMP_HIDDEN, LANES), F32)
            for blk in range(2):
                for g in range(2):
                    w2sel = w2sel.at[:, blk, g, blk * CMP_HIDDEN:(blk + 1) * CMP_HIDDEN,
                                     g * HEAD_DIM:(g + 1) * HEAD_DIM].set(w2)
            cc_args = (jnp.tile(cmp_pos[j].transpose(0, 2, 1), (1, 1, 2))[:, :, None, :],
                       _block_diag2(cmp_w1[j].transpose(0, 2, 1, 3)).astype(BF16).reshape(2, 32, 2 * LANES, 256),
                       jnp.tile(cmp_b1[j], (1, 2))[:, None, :], w2sel.astype(BF16),
                       jnp.tile(cmp_b2[j], (1, 2))[:, None, :], kg, seg, 512)
            wt = w_in_nsa[j][:, 768:1536].T.astype(BF16)
            qn, rows_t, win_t, katt, xqn_p, gates = _nsa_inproj_t(
                yp, g_mix[i][None], w, wt, seg, qg, kg, jnp.tile(k_gain_nsa[j], 2)[:, None], xg, tm, lp)
            pages_t = rows_t[:, :256].reshape(bp, 2, 2, HEAD_DIM, lp // PAGE_SIZE, PAGE_SIZE)
            pages_t = pages_t.transpose(0, 4, 1, 2, 3, 5).reshape(bp * lp // PAGE_SIZE, 2, 2, HEAD_DIM, PAGE_SIZE)
            cmp_p = _compress_cache(pages_t, *cc_args).reshape(bp, nc, 256)
            kvatt = jnp.pad(katt.reshape(bp, lp, 512), ((0, 0), (PAD_ROWS, 0), (0, 0)))
            mix_p = _nsa_prompt_attn(qn.reshape(bp, lp, 768), gates.reshape(bp, lp, 128), cmp_p, kvatt,
                                     oh, wb, nbias, ct, amat).reshape(tp_, 768)
            outs["kv_p"].append(rows_t.reshape(bp, 4, NSA_KV_HEADS, HEAD_DIM, lp).transpose(0, 4, 1, 2, 3))
            wlen = min(WINDOW, lp)
            outs["win_p"].append(win_t.reshape(bp, 2, NSA_KV_HEADS, HEAD_DIM, wlen).transpose(0, 4, 1, 2, 3))
            qn, rows, win, _, xqn_s, gates = _nsa_inproj(ys, g_mix[i][None], w, seg, qg, kg, xg, ts_)
            cache_t = cache_nsa_kv[j].transpose(0, 2, 3, 4, 1)
            cmp_phys = _compress_cache(cache_t, *cc_args)
            winc = cache_nsa_win[j].transpose(0, 2, 3, 4, 1)
            mix_s = _nsa_sample_attn(page_table, qn.reshape(bs, ls, 768), gates.reshape(bs, ls, 128),
                                     rows.reshape(bs, ls, 512), winc, win.reshape(bs, ls, 256),
                                     scb, sbt, wbt, rmat, eloc, cmp_phys.reshape(n_phys, 2, 256),
                                     cache_t).reshape(ts_, 768)
            outs["kv_s"].append(rows.reshape(bs, ls, 4, NSA_KV_HEADS, HEAD_DIM))
            wall = jnp.concatenate([cache_nsa_win[j], win.reshape(bs, ls, 2, NSA_KV_HEADS, HEAD_DIM)], axis=1)
            outs["win_s"].append(wall[:, -wbuf:])
        else:
            w = w_in_pool[j].astype(BF16)
            wbd = jnp.zeros((POOL_WIDTH, POOL_WIDTH), F32)
            for gi in range(len(POOL_WINDOWS)):
                sl = slice(gi * POOL_GROUP_DIM, (gi + 1) * POOL_GROUP_DIM)
                wbd = wbd.at[sl, sl].set(w_pool_grp[j, gi])
            wbd = wbd.astype(BF16)
            psc = pool_scale[j][None]
            wo_mix = wo[:768].astype(BF16)
            u_p, xqn_p = _pool_inproj(yp, g_mix[i][None], w, seg, xg, tm)
            u_p3 = u_p.reshape(bp, lp, POOL_WIDTH)
            mix_p = _pool_mix(u_p3, jnp.zeros((bp, 16, POOL_WIDTH), F32), wbd, psc, tm, 0).reshape(tp_, 768)
            outs["pool_p"].append(u_p3[:, -POOL_STATE:])
            u_s, xqn_s = _pool_inproj(ys, g_mix[i][None], w, seg, xg, ts_)
            u_s3 = u_s.reshape(bs, ls, POOL_WIDTH)
            prev16 = jnp.concatenate([jnp.zeros((bs, 1, POOL_WIDTH), F32), state_pool[j]], axis=1)
            mix_s = _pool_mix(u_s3, prev16, wbd, psc, ls, past_len).reshape(ts_, 768)
            outs["pool_s"].append(jnp.concatenate([state_pool[j], u_s3], axis=1)[:, -POOL_STATE:])
        wo_att = wo[768:].astype(BF16)
        att_p = _mem_attn(xqn_p.reshape(bp, lp, 256), mkv_p, tm).reshape(tp_, 256)
        att_s = _mem_attn(xqn_s.reshape(bs, ls, 256), mkv_s, ls).reshape(ts_, 256)
        wu, wd = w_up[i].astype(BF16), w_down[i].astype(BF16)
        yp, c_p = _mix_ffn_prompt(yp, mix_p, att_p, wo_mix, wo_att, g_ffn[i][None], wu, conv_w[i],
                                  conv_b[i][None], wd, tm, lp)
        ys, _ = _outproj(ys, mix_s, att_s, wo_mix, wo_att, g_ffn[i][None], ts_)
        ys, c_s = _ffn_sample(ys, g_ffn[i][None], wu, conv_w[i], conv_b[i][None], wd, state_ffn_conv[i])
        outs["conv_p"].append(c_p)
        outs["conv_s"].append(c_s)
    st = lambda k: jnp.stack(outs[k])
    return (yp.reshape(bp, lp, D_MODEL), ys.reshape(bs, ls, D_MODEL), st("kv_p"), st("kv_s"), st("win_p"),
            st("win_s"), st("mem_p"), st("pool_p"), st("pool_s"), st("conv_p"), st("conv_s"))
```

```python
import functools
import math

import numpy as np
import jax
import jax.numpy as jnp
from jax import lax
from jax.experimental import pallas as pl
from jax.experimental.pallas import tpu as pltpu

F32 = jnp.float32
BF16 = jnp.bfloat16

D_MODEL = 1024
PAGE_SIZE = 128
HEAD_DIM = 64
NSA_HEADS = 12
NSA_KV_HEADS = 2
NSA_GROUP = NSA_HEADS // NSA_KV_HEADS
NSA_WIDTH = NSA_HEADS * HEAD_DIM
NSA_KV_WIDTH = NSA_KV_HEADS * HEAD_DIM
CMP_BLOCK = 64
CMP_HIDDEN = 128
SLC_TOPN = 16
WINDOW = 512
FORCED_SCORE = 1e4
XATTN_HEADS = 4
XATTN_WIDTH = XATTN_HEADS * HEAD_DIM
POOL_WINDOWS = (2, 4, 8, 16)
POOL_GROUP_DIM = 192
POOL_WIDTH = len(POOL_WINDOWS) * POOL_GROUP_DIM
POOL_STATE = max(POOL_WINDOWS) - 1
D_FF = 2816
REL_BUCKETS = 32
REL_MAX_DIST = 128
EPS = 1e-6
BIG = 1e30

LANES = 128
WIN_KEYS = WINDOW + 2 * 64
NEAR_KEYS = 4 * 64
PAD_ROWS = WIN_KEYS
NSA_IN_COLS = 1920
FF_CHUNK = D_FF // 2
VMEM_LIMIT = 56 * 1024 * 1024


def _cparams(*sem):
    return pltpu.CompilerParams(dimension_semantics=sem, vmem_limit_bytes=VMEM_LIMIT)


def _nt(a, b):
    return lax.dot_general(a, b, (((1,), (1,)), ((), ())), preferred_element_type=F32)


def _tn(a, b):
    return lax.dot_general(a, b, (((0,), (0,)), ((), ())), preferred_element_type=F32)


def _dot(a, b):
    return jnp.dot(a, b, preferred_element_type=F32)


def _rms_rows(x, g):
    return x * lax.rsqrt(jnp.mean(x * x, axis=-1, keepdims=True) + EPS) * g


def _seg_rms(zc, seg, gain):
    zz = zc * zc
    hi = zz.astype(BF16)
    lo = (zz - hi.astype(F32)).astype(BF16)
    ms = (_dot(hi, seg) + _dot(lo, seg)) * (1.0 / HEAD_DIM)
    return zc * lax.rsqrt(ms + EPS) * gain


def _nsa_inproj_kernel(x_ref, g_ref, w_ref, seg_ref, qg_ref, kg_ref, xg_ref,
                       qn_ref, rows_ref, win_ref, katt_ref, xqn_ref, gates_ref):
    xn = _rms_rows(x_ref[...], g_ref[...])
    z = _dot(xn.astype(BF16), w_ref[...])
    seg = seg_ref[...]
    for h in range(6):
        qn_ref[:, h * LANES:(h + 1) * LANES] = _seg_rms(z[:, h * LANES:(h + 1) * LANES], seg, qg_ref[...]).astype(BF16)
    kv = 768
    rows_ref[:, 0:256] = z[:, kv:kv + 256]
    ks = _seg_rms(z[:, kv + 256:kv + 384], seg, kg_ref[...])
    vs = z[:, kv + 384:kv + 512]
    kw = _seg_rms(z[:, kv + 512:kv + 640], seg, kg_ref[...])
    vw = z[:, kv + 640:kv + 768]
    rows_ref[:, 256:384] = ks
    rows_ref[:, 384:512] = vs
    win_ref[:, 0:128] = kw
    win_ref[:, 128:256] = vw
    katt_ref[:, 0:128] = ks.astype(BF16)
    katt_ref[:, 128:256] = vs.astype(BF16)
    katt_ref[:, 256:384] = kw.astype(BF16)
    katt_ref[:, 384:512] = vw.astype(BF16)
    for h in range(2):
        c0 = 1536 + h * LANES
        xqn_ref[:, h * LANES:(h + 1) * LANES] = _seg_rms(z[:, c0:c0 + LANES], seg, xg_ref[...]).astype(BF16)
    gates_ref[...] = jax.nn.sigmoid(z[:, 1792:1920])


def _nsa_inproj_t_kernel(x_ref, g_ref, w_ref, wt_ref, seg_ref, qg_ref, kg_ref, kgc_ref, xg_ref,
                         qn_ref, rowst_ref, wint_ref, katt_ref, xqn_ref, gates_ref):
    xn = _rms_rows(x_ref[...], g_ref[...]).astype(BF16)
    z = _dot(xn, w_ref[...])
    zt = _nt(wt_ref[...], xn)

    def norm_t(a):
        parts = []
        for gi in range(2):
            ag = a[gi * HEAD_DIM:(gi + 1) * HEAD_DIM]
            parts.append(ag * lax.rsqrt(jnp.mean(ag * ag, axis=0, keepdims=True) + EPS))
        return jnp.concatenate(parts, axis=0) * kgc_ref[...]

    rowst_ref[0, 0:256, :] = zt[0:256]
    rowst_ref[0, 256:384, :] = norm_t(zt[256:384])
    rowst_ref[0, 384:512, :] = zt[384:512]
    wint_ref[0, 0:128, :] = norm_t(zt[512:640])
    wint_ref[0, 128:256, :] = zt[640:768]
    seg = seg_ref[...]
    for h in range(6):
        qn_ref[:, h * LANES:(h + 1) * LANES] = _seg_rms(z[:, h * LANES:(h + 1) * LANES], seg, qg_ref[...]).astype(BF16)
    kv = 768
    katt_ref[:, 0:128] = _seg_rms(z[:, kv + 256:kv + 384], seg, kg_ref[...]).astype(BF16)
    katt_ref[:, 128:256] = z[:, kv + 384:kv + 512].astype(BF16)
    katt_ref[:, 256:384] = _seg_rms(z[:, kv + 512:kv + 640], seg, kg_ref[...]).astype(BF16)
    katt_ref[:, 384:512] = z[:, kv + 640:kv + 768].astype(BF16)
    for h in range(2):
        c0 = 1536 + h * LANES
        xqn_ref[:, h * LANES:(h + 1) * LANES] = _seg_rms(z[:, c0:c0 + LANES], seg, xg_ref[...]).astype(BF16)
    gates_ref[...] = jax.nn.sigmoid(z[:, 1792:1920])


def _nsa_inproj_t(x, g, w, wt, seg, qg, kg, kgc, xg, tm, seq_len):
    t = x.shape[0]
    tps = seq_len // tm
    full = lambda a: pl.BlockSpec(a.shape, lambda i: (0,) * a.ndim)
    row = lambda n: pl.BlockSpec((tm, n), lambda i: (i, 0))
    col = lambda n: pl.BlockSpec((1, n, tm), lambda i: (i // tps, 0, i % tps))
    wlen = min(WINDOW, seq_len)
    assert wlen % tm == 0
    skip = tps - wlen // tm
    win = pl.BlockSpec((1, 256, tm), lambda i: (i // tps, 0, jnp.maximum(i % tps - skip, 0)))
    return pl.pallas_call(
        _nsa_inproj_t_kernel,
        grid=(t // tm,),
        in_specs=[row(D_MODEL), full(g), full(w), full(wt), full(seg), full(qg), full(kg), full(kgc), full(xg)],
        out_specs=[row(768), col(512), win, row(512), row(256), row(128)],
        out_shape=[jax.ShapeDtypeStruct((t, 768), BF16), jax.ShapeDtypeStruct((t // seq_len, 512, seq_len), F32),
                   jax.ShapeDtypeStruct((t // seq_len, 256, wlen), F32), jax.ShapeDtypeStruct((t, 512), BF16),
                   jax.ShapeDtypeStruct((t, 256), BF16), jax.ShapeDtypeStruct((t, 128), F32)],
        compiler_params=_cparams("arbitrary"),
    )(x, g, w, wt, seg, qg, kg, kgc, xg)


def _nsa_inproj(x, g, w, seg, qg, kg, xg, tm):
    t = x.shape[0]
    full = lambda a: pl.BlockSpec(a.shape, lambda i: (0,) * a.ndim)
    row = lambda n: pl.BlockSpec((tm, n), lambda i: (i, 0))
    return pl.pallas_call(
        _nsa_inproj_kernel,
        grid=(t // tm,),
        in_specs=[row(D_MODEL), full(g), full(w), full(seg), full(qg), full(kg), full(xg)],
        out_specs=[row(768), row(512), row(256), row(512), row(256), row(128)],
        out_shape=[jax.ShapeDtypeStruct((t, 768), BF16), jax.ShapeDtypeStruct((t, 512), F32),
                   jax.ShapeDtypeStruct((t, 256), F32), jax.ShapeDtypeStruct((t, 512), BF16),
                   jax.ShapeDtypeStruct((t, 256), BF16), jax.ShapeDtypeStruct((t, 128), F32)],
        compiler_params=_cparams("parallel"),
    )(x, g, w, seg, qg, kg, xg)


def _pool_layer_kernel(x_ref, g_ref, w_ref, seg_ref, xg_ref, prev_ref, wbd_ref, scale_ref,
                       mix_ref, xqn_ref, tail_ref, carry_ref, *, pos0, tiles_per_seq):
    li = pl.program_id(0) % tiles_per_seq
    tm = x_ref.shape[0]

    @pl.when(li == 0)
    def _():
        carry_ref[...] = prev_ref[0]

    xn = _rms_rows(x_ref[...], g_ref[...])
    z = _dot(xn.astype(BF16), w_ref[...])
    seg = seg_ref[...]
    for h in range(2):
        c0 = POOL_WIDTH + h * LANES
        xqn_ref[:, h * LANES:(h + 1) * LANES] = _seg_rms(z[:, c0:c0 + LANES], seg, xg_ref[...]).astype(xqn_ref.dtype)
    u = z[:, 0:POOL_WIDTH]
    ext = jnp.concatenate([carry_ref[...], u], axis=0)
    carry_ref[...] = ext[tm:tm + 16]
    tail_ref[0] = ext[tm:tm + 16]
    s2 = ext + pltpu.roll(ext, 1, axis=0)
    s4 = s2 + pltpu.roll(s2, 2, axis=0)
    s8 = s4 + pltpu.roll(s4, 4, axis=0)
    s16 = s8 + pltpu.roll(s8, 8, axis=0)
    lane = lax.broadcasted_iota(jnp.int32, (tm, POOL_WIDTH), 1)
    pos1 = (pos0 + 1 + li * tm + lax.broadcasted_iota(jnp.int32, (tm, POOL_WIDTH), 0)).astype(F32)
    grp = lane // POOL_GROUP_DIM
    ssum = jnp.where(grp == 0, s2[16:], jnp.where(grp == 1, s4[16:], jnp.where(grp == 2, s8[16:], s16[16:])))
    win = jnp.where(grp == 0, 2.0, jnp.where(grp == 1, 4.0, jnp.where(grp == 2, 8.0, 16.0)))
    pooled = ssum / jnp.minimum(win, pos1) - u
    mix_ref[...] = (_dot(pooled.astype(BF16), wbd_ref[...]) * scale_ref[...]).astype(mix_ref.dtype)


def _pool_layer(x, g, w, seg, xg, prev16, wbd, scale, tm, seq_len, pos0):
    t = x.shape[0]
    nseq, tps = t // seq_len, seq_len // tm
    odt = BF16 if tm % 16 == 0 else F32
    full = lambda a: pl.BlockSpec(a.shape, lambda i: (0,) * a.ndim)
    row = lambda n: pl.BlockSpec((tm, n), lambda i: (i, 0))
    mix, xqn, tail = pl.pallas_call(
        functools.partial(_pool_layer_kernel, pos0=pos0, tiles_per_seq=tps),
        grid=(t // tm,),
        in_specs=[row(D_MODEL), full(g), full(w), full(seg), full(xg),
                  pl.BlockSpec((1, 16, POOL_WIDTH), lambda i: (i // tps, 0, 0)), full(wbd), full(scale)],
        out_specs=[row(POOL_WIDTH), row(256), pl.BlockSpec((1, 16, POOL_WIDTH), lambda i: (i, 0, 0))],
        out_shape=[jax.ShapeDtypeStruct((t, POOL_WIDTH), odt), jax.ShapeDtypeStruct((t, 256), odt),
                   jax.ShapeDtypeStruct((t // tm, 16, POOL_WIDTH), F32)],
        scratch_shapes=[pltpu.VMEM((16, POOL_WIDTH), F32)],
        compiler_params=_cparams("arbitrary"),
    )(x, g, w, seg, xg, prev16, wbd, scale)
    tail = tail.reshape(nseq, tps, 16, POOL_WIDTH)[:, tps - 1, 16 - POOL_STATE:]
    return mix.astype(BF16), xqn.astype(BF16), tail


def _memkv_kernel(x_ref, g_ref, w_ref, seg_ref, kg_ref, o_ref):
    xn = _rms_rows(x_ref[...], g_ref[...])
    z = _dot(xn.astype(BF16), w_ref[...])
    seg = seg_ref[...]
    for h in range(2):
        o_ref[:, h * LANES:(h + 1) * LANES] = _seg_rms(z[:, h * LANES:(h + 1) * LANES], seg, kg_ref[...])
    o_ref[:, 256:512] = z[:, 256:512]


def _memkv(x, g, w, seg, kg, tm):
    t = x.shape[0]
    full = lambda a: pl.BlockSpec(a.shape, lambda i: (0,) * a.ndim)
    row = lambda n: pl.BlockSpec((tm, n), lambda i: (i, 0))
    return pl.pallas_call(
        _memkv_kernel,
        grid=(t // tm,),
        in_specs=[row(D_MODEL), full(g), full(w), full(seg), full(kg)],
        out_specs=row(512),
        out_shape=jax.ShapeDtypeStruct((t, 512), F32),
        compiler_params=_cparams("parallel"),
    )(x, g, w, seg, kg)


def _compress_cache_kernel(x_ref, pos_ref, w1_ref, b1_ref, w2_ref, b2_ref, kg_ref, seg_ref, o_ref, acc_ref):
    dc = pl.program_id(1)

    @pl.when(dc == 0)
    def _():
        acc_ref[...] = jnp.zeros_like(acc_ref)

    pt, nd = x_ref.shape[0], x_ref.shape[3]
    x2 = x_ref.reshape(pt * 4 * nd, LANES)

    def feature_rows(comp, g, dd):
        return (x2[pl.ds((comp * 2 + g) * nd + dd, pt, stride=4 * nd), :] + pos_ref[comp, dd]).astype(BF16)

    for comp in range(2):
        for g in range(2):
            part = None
            for dp in range(nd // 2):
                xd = jnp.concatenate([feature_rows(comp, g, 2 * dp), feature_rows(comp, g, 2 * dp + 1)], axis=1)
                d = _dot(xd, w1_ref[comp, dp])
                part = d if part is None else part + d
            c0 = (comp * 2 + g) * 256
            acc_ref[:, c0:c0 + 256] += part

    @pl.when(dc == pl.num_programs(1) - 1)
    def _():
        for comp in range(2):
            hid = [jax.nn.gelu(acc_ref[:, (comp * 2 + g) * 256:(comp * 2 + g + 1) * 256] + b1_ref[comp]).astype(BF16)
                   for g in range(2)]
            for blk in range(2):
                o = _dot(hid[0], w2_ref[comp, blk, 0]) + _dot(hid[1], w2_ref[comp, blk, 1]) + b2_ref[comp]
                if comp == 0:
                    o = _seg_rms(o, seg_ref[...], kg_ref[...])
                o_ref[:, blk * 256 + comp * LANES:blk * 256 + (comp + 1) * LANES] = o


CMP_FEATS = 16


def _compress_cache(xt, pos_t, w1t, b1t, w2sel, b2t, kg, seg, pt):
    n_phys = xt.shape[0]
    pt = min(pt, n_phys)
    assert n_phys % pt == 0
    full = lambda a: pl.BlockSpec(a.shape, lambda j, d: (0,) * a.ndim)
    return pl.pallas_call(
        _compress_cache_kernel,
        grid=(n_phys // pt, HEAD_DIM // CMP_FEATS),
        in_specs=[pl.BlockSpec((pt, 2, 2, CMP_FEATS, LANES), lambda j, d: (j, 0, 0, d, 0)),
                  pl.BlockSpec((2, CMP_FEATS, 1, LANES), lambda j, d: (0, d, 0, 0)),
                  pl.BlockSpec((2, CMP_FEATS // 2, 2 * LANES, 256), lambda j, d: (0, d, 0, 0)),
                  full(b1t), full(w2sel), full(b2t), full(kg), full(seg)],
        out_specs=pl.BlockSpec((pt, 512), lambda j, d: (j, 0)),
        out_shape=jax.ShapeDtypeStruct((n_phys, 512), F32),
        scratch_shapes=[pltpu.VMEM((pt, 1024), F32)],
        compiler_params=_cparams("parallel", "arbitrary"),
    )(xt, pos_t, w1t, b1t, w2sel, b2t, kg, seg)


def _bucket_np(d):
    n = np.maximum(d, 0)
    max_exact = REL_BUCKETS // 2
    nf = np.maximum(n, 1).astype(np.float32)
    large = max_exact + (np.log(nf / max_exact) / math.log(REL_MAX_DIST / max_exact)
                         * (REL_BUCKETS - max_exact)).astype(np.int32)
    large = np.minimum(large, REL_BUCKETS - 1)
    return np.where(n < max_exact, n, large).astype(np.int32)


def _bias_table_kernel(rb_ref, code_ref, o_ref):
    code = code_ref[...]
    for h in range(NSA_HEADS):
        far = rb_ref[REL_BUCKETS - 1, h]
        acc = jnp.full(code.shape, -BIG, F32)
        for k in range(REL_BUCKETS):
            acc = jnp.where(code == k, rb_ref[k, h] - far, acc)
        o_ref[h] = acc


def _bias_table(rel_bias, code):
    code = jnp.asarray(code, jnp.int32)
    return pl.pallas_call(
        _bias_table_kernel,
        in_specs=[pl.BlockSpec(memory_space=pltpu.SMEM), pl.BlockSpec(code.shape, lambda: (0, 0))],
        out_specs=pl.BlockSpec((NSA_HEADS,) + code.shape, lambda: (0, 0, 0)),
        out_shape=jax.ShapeDtypeStruct((NSA_HEADS,) + code.shape, F32),
    )(rel_bias, code)


def _window_codes():
    ql = np.arange(64)[:, None]
    j = np.arange(WIN_KEYS)[None, :]
    d = WIN_KEYS - 64 + ql - j
    return np.where((d >= 0) & (d <= WINDOW), _bucket_np(d), -1).astype(np.int32)


def _cmp_codes():
    delta = np.arange(8)[:, None]
    ql = np.arange(64)[None, :]
    d = delta * CMP_BLOCK + ql - (CMP_BLOCK - 1)
    return np.where(d >= 0, _bucket_np(d), -1).astype(np.int32)


FAR_TILE = 512


def _nsa_prompt_kernel(qn_ref, gates_ref, cmp_ref, kv_ref, oh_ref, wb_ref, nb_ref, ct_ref, amat_ref, o_ref):
    i = pl.program_id(1)
    nc = cmp_ref.shape[1]
    gr = NSA_GROUP * 64
    rows = 2 * gr
    q = qn_ref[0].astype(F32)
    gts = gates_ref[0]
    kc = cmp_ref[0, :, 0:128].astype(BF16)
    vc = cmp_ref[0, :, 128:256].astype(BF16)
    lane = lax.broadcasted_iota(jnp.int32, (64, LANES), 1)
    half = [lane < 64, lane >= 64]
    qall = jnp.concatenate([jnp.where(half[g], q[:, h * LANES:(h + 1) * LANES], 0.0)
                            for g in range(2) for h in range(NSA_GROUP)], axis=0).astype(BF16)

    blk_r = lax.broadcasted_iota(jnp.int32, (nc, rows), 0)
    ct = ct_ref[...]
    bias = jnp.where(blk_r == i, ct[0:1], jnp.where(blk_r == i - 1, ct[1:2], jnp.where(
        blk_r == i - 2, ct[2:3], jnp.where(blk_r > i, -BIG, 0.0))))
    lc = _nt(kc, qall) + bias
    e = jnp.exp(lc - jnp.max(lc, axis=0, keepdims=True))
    p = jnp.where(bias > -0.5 * BIG, e / jnp.sum(e, axis=0, keepdims=True), 0.0)
    o_c = _tn(p.astype(BF16), vc)
    imp = []
    for g in range(2):
        s3 = p[:, g * gr:g * gr + 128] + p[:, g * gr + 128:g * gr + 256] + p[:, g * gr + 256:g * gr + 384]
        imp.append(s3 + pltpu.roll(s3, 64, axis=1))
    lane_c = lax.broadcasted_iota(jnp.int32, (nc, LANES), 1)
    blk = lax.broadcasted_iota(jnp.int32, (nc, LANES), 0)
    impp = jnp.where(lane_c < 64, imp[0], imp[1])
    forced = (blk == 0) | (blk == i) | (blk == i - 1)
    score = jnp.where(blk <= i, jnp.where(forced, FORCED_SCORE, impp), -1.0)
    sc8 = [score[r * 8:(r + 1) * 8] for r in range(nc // 8)]
    blk8 = lax.broadcasted_iota(jnp.int32, (8, LANES), 0)
    rank8 = [jnp.zeros((8, LANES), F32) for _ in sc8]
    for cp in range(nc):
        row = score[cp:cp + 1, :]
        for r in range(nc // 8):
            if r * 8 > cp:
                beats = row >= sc8[r]
            elif r * 8 + 7 < cp:
                beats = row > sc8[r]
            else:
                beats = (row > sc8[r]) | ((row == sc8[r]) & (blk8 + r * 8 > cp))
            rank8[r] = rank8[r] + jnp.where(beats, 1.0, 0.0)
    rank = jnp.concatenate(rank8, axis=0)
    notsel =jnp.where((rank < float(min(SLC_TOPN, nc))) & (score >= 0.0), 0.0, 1.0)
    notsel_far = jnp.where(blk > i - 3, 1.0, notsel)
    tail = jnp.where(lax.broadcasted_iota(jnp.int32, (LANES - nc, LANES), 0) == 64 - nc, 1.0, 0.0)
    ns_near = jnp.concatenate([notsel, tail], axis=0).astype(BF16)
    ns_far = jnp.concatenate([notsel_far, tail], axis=0).astype(BF16)
    a = amat_ref[...]
    lhs_near = jnp.concatenate([qall, (_nt(a, ns_near) * -BIG).astype(BF16)], axis=1)
    lhs_far = jnp.concatenate([qall, (_nt(a, ns_far) * -BIG).astype(BF16)], axis=1)
    lane_r = lax.broadcasted_iota(jnp.int32, (rows, LANES), 1)
    lhs_win = jnp.concatenate([qall, jnp.where(lane_r == 64, -BIG, 0.0).astype(BF16)], axis=1)

    def keys_aug(r0, n, c0):
        return jnp.concatenate([kv_ref[0, pl.ds(r0, n), c0:c0 + LANES], oh_ref[pl.ds(r0, n), :]], axis=1)

    def fold(x, op):
        return functools.reduce(op, [x[:, c * LANES:(c + 1) * LANES] for c in range(x.shape[1] // LANES)])

    ntiles = (i + 5) // (FAR_TILE // 64)
    r0n = pl.multiple_of(PAD_ROWS + (i - 3) * 64, 64)
    s_near = _nt(lhs_near, keys_aug(r0n, NEAR_KEYS, 0)) + nb_ref[...]

    def tile_row(t):
        return pl.multiple_of(PAD_ROWS + t * FAR_TILE, LANES)

    def far_logits(t):
        return _nt(lhs_far, keys_aug(tile_row(t), FAR_TILE, 0))

    def absorb(s, v, m, lrun, acc):
        m_new = jnp.maximum(m, jnp.max(fold(s, jnp.maximum), axis=1, keepdims=True))
        alpha = jnp.exp(m - m_new)
        e = jnp.exp(s - m_new)
        return m_new, alpha * lrun + fold(e, jnp.add), alpha * acc + _dot(e.astype(BF16), v)

    def far_tile(t, carry):
        return absorb(far_logits(t), kv_ref[0, pl.ds(tile_row(t), FAR_TILE), 128:256], *carry)

    def far_pair(u, carry):
        sa, sb = far_logits(2 * u), far_logits(2 * u + 1)
        carry = absorb(sa, kv_ref[0, pl.ds(tile_row(2 * u), FAR_TILE), 128:256], *carry)
        return absorb(sb, kv_ref[0, pl.ds(tile_row(2 * u + 1), FAR_TILE), 128:256], *carry)

    zero = jnp.zeros((rows, LANES), F32)
    carry = lax.fori_loop(0, ntiles // 2, far_pair, (jnp.full((rows, 1), -BIG, F32), zero, zero))
    carry = lax.cond(ntiles % 2 == 1, lambda c: far_tile(ntiles - 1, c), lambda c: c, carry)
    _, lrun, acc = absorb(s_near, kv_ref[0, pl.ds(r0n, NEAR_KEYS), 128:256], *carry)
    o_s = acc / jnp.sum(lrun, axis=1, keepdims=True)

    r0w = pl.multiple_of(PAD_ROWS + (i - 9) * 64, 64)
    s = _nt(lhs_win, keys_aug(r0w, WIN_KEYS, 256)) + wb_ref[...]
    e = jnp.exp(s - jnp.max(fold(s, jnp.maximum), axis=1, keepdims=True))
    o_w = _dot(e.astype(BF16), kv_ref[0, pl.ds(r0w, WIN_KEYS), 384:512]) / jnp.sum(fold(e, jnp.add), axis=1,
                                                                                 keepdims=True)

    low = lax.broadcasted_iota(jnp.int32, (64, LANES), 1) < 64
    for h in range(NSA_GROUP):
        per_group = []
        for g in range(2):
            c0 = (g * NSA_GROUP + h) * 3
            r = slice(g * gr + h * 64, g * gr + (h + 1) * 64)
            per_group.append(gts[:, c0:c0 + 1] * o_c[r] + gts[:, c0 + 1:c0 + 2] * o_s[r]
                             + gts[:, c0 + 2:c0 + 3] * o_w[r])
        o_ref[0, :, h * LANES:(h + 1) * LANES] = jnp.where(low, per_group[0], per_group[1]).astype(BF16)


def _nsa_prompt_attn(qn, gates, cmp, kvatt, oh, wb, nb, ct, amat):
    b, l, _ = qn.shape
    nc = l // 64
    full = lambda a: pl.BlockSpec(a.shape, lambda bi, i: (0,) * a.ndim)
    return pl.pallas_call(
        _nsa_prompt_kernel,
        grid=(b, nc),
        in_specs=[pl.BlockSpec((1, 64, 768), lambda bi, i: (bi, i, 0)),
                  pl.BlockSpec((1, 64, 128), lambda bi, i: (bi, i, 0)),
                  pl.BlockSpec((1, nc, 256), lambda bi, i: (bi, 0, 0)),
                  pl.BlockSpec((1, PAD_ROWS + l, 512), lambda bi, i: (bi, 0, 0)),
                  full(oh), full(wb), full(nb), full(ct), full(amat)],
        out_specs=pl.BlockSpec((1, 64, 768), lambda bi, i: (bi, i, 0)),
        out_shape=jax.ShapeDtypeStruct((b, l, 768), BF16),
        compiler_params=_cparams("parallel", "arbitrary"),
    )(qn, gates, cmp, kvatt, oh, wb, nb, ct, amat)


def _mem_attn_kernel(q_ref, kv_ref, o_ref):
    q = q_ref[0].astype(F32)
    k = kv_ref[0, :, 0:256].astype(BF16)
    v = kv_ref[0, :, 256:512].astype(BF16)
    head = lax.broadcasted_iota(jnp.int32, q.shape, 1) // HEAD_DIM
    out = jnp.zeros(q.shape, F32)
    for h in range(XATTN_HEADS):
        qh = jnp.where(head == h, q, 0.0).astype(BF16)
        s = _nt(qh, k)
        e = jnp.exp(s - jnp.max(s, axis=1, keepdims=True))
        p = e / jnp.sum(e, axis=1, keepdims=True)
        out = out + jnp.where(head == h, _dot(p.astype(BF16), v), 0.0)
    o_ref[0] = out.astype(BF16)


def _mem_attn(xqn, mkv, tq):
    b, l, _ = xqn.shape
    nm = mkv.shape[1]
    return pl.pallas_call(
        _mem_attn_kernel,
        grid=(b, l // tq),
        in_specs=[pl.BlockSpec((1, tq, 256), lambda bi, i: (bi, i, 0)),
                  pl.BlockSpec((1, nm, 512), lambda bi, i: (bi, 0, 0))],
        out_specs=pl.BlockSpec((1, tq, 256), lambda bi, i: (bi, i, 0)),
        out_shape=jax.ShapeDtypeStruct((b, l, 256), BF16),
        compiler_params=_cparams("parallel", "parallel"),
    )(xqn, mkv)


def _outproj_kernel(y_ref, mix_ref, att_ref, wm_ref, wa_ref, g_ref, o_ref, h_ref):
    y = y_ref[...] + _dot(mix_ref[...], wm_ref[...]) + _dot(att_ref[...], wa_ref[...])
    o_ref[...] = y
    h_ref[...] = _rms_rows(y, g_ref[...]).astype(BF16)


def _outproj(y, mix, att, wm, wa, g_ffn, tm):
    t = y.shape[0]
    full = lambda a: pl.BlockSpec(a.shape, lambda i: (0,) * a.ndim)
    row = lambda n: pl.BlockSpec((tm, n), lambda i: (i, 0))
    return pl.pallas_call(
        _outproj_kernel,
        grid=(t // tm,),
        in_specs=[row(D_MODEL), row(768), row(256), full(wm), full(wa), full(g_ffn)],
        out_specs=[row(D_MODEL), row(D_MODEL)],
        out_shape=[jax.ShapeDtypeStruct((t, D_MODEL), F32), jax.ShapeDtypeStruct((t, D_MODEL), BF16)],
        compiler_params=_cparams("parallel"),
    )(y, mix, att, wm, wa, g_ffn)


def _ffn_half(h, w_ref, cw_ref, cb_ref, ext_ref, prev8, fix):
    tm = h.shape[0]
    u = _dot(h, w_ref[...])
    ext_ref[0:8, :] = prev8
    ext_ref[8:8 + tm, :] = u
    s1 = ext_ref[7:7 + tm, :]
    s2 = ext_ref[6:6 + tm, :]
    if fix is not None:
        rowm, p1, p2 = fix
        s1 = jnp.where(rowm >= 1, s1, p1)
        s2 = jnp.where(rowm >= 2, s2, p2)
    cw = cw_ref[...]
    return u, cb_ref[...] + cw[0:1] * s2 + cw[1:2] * s1 + cw[2:3] * u


FF_SUB = (0, 1024, 2048, D_FF)


def _mix_ffn_kernel(y_ref, mix_ref, att_ref, wm_ref, wo_ref, g_ref, cw_ref, cb_ref, wup_hbm, wd_hbm,
                    o_ref, sa_ref, sv_ref, carry_ref, ext_ref, wup_ref, wd_ref, sem, *, tiles_per_seq):
    j = pl.program_id(0)
    tm = y_ref.shape[0]

    @pl.when(j == 0)
    def _():
        up = pltpu.make_async_copy(wup_hbm, wup_ref, sem.at[0])
        down = pltpu.make_async_copy(wd_hbm, wd_ref, sem.at[1])
        up.start()
        down.start()
        up.wait()
        down.wait()

    @pl.when((j % tiles_per_seq) == 0)
    def _():
        carry_ref[...] = jnp.zeros_like(carry_ref)

    y1 = y_ref[...] + _dot(mix_ref[...], wm_ref[...]) + _dot(att_ref[...], wo_ref[...])
    h = _rms_rows(y1, g_ref[...]).astype(BF16)
    ffn = None
    for c0, c1 in zip(FF_SUB[:-1], FF_SUB[1:]):
        w = c1 - c0
        conv = []
        for half, st_ref in ((0, sa_ref), (1, sv_ref)):
            cols = slice(half * D_FF + c0, half * D_FF + c1)
            u = _dot(h, wup_ref[:, cols])
            ext_ref[half, 0:8, 0:w] = carry_ref[:, cols]
            ext_ref[half, 8:8 + tm, 0:w] = u
            s1 = ext_ref[half, 7:7 + tm, 0:w]
            s2 = ext_ref[half, 6:6 + tm, 0:w]
            cw = cw_ref[:, cols]
            conv.append(cb_ref[:, cols] + cw[0:1] * s2 + cw[1:2] * s1 + cw[2:3] * u)
            carry_ref[:, cols] = u[tm - 8:tm]
            st_ref[0, :, c0:c1] = u[tm - 8:tm]
        d = _dot((jax.nn.silu(conv[0]) * conv[1]).astype(BF16), wd_ref[c0:c1, :])
        ffn = d if ffn is None else ffn + d
    o_ref[...] = y1 + ffn


def _mix_ffn_prompt(y, mix, att, wm, wo, g, w_up, conv_w, conv_b, w_down, tm, seq_len):
    t = y.shape[0]
    nseq = t // seq_len
    tps = seq_len // tm
    full = lambda a: pl.BlockSpec(a.shape, lambda j: (0,) * a.ndim)
    row = lambda n: pl.BlockSpec((tm, n), lambda j: (j, 0))
    hbm = pl.BlockSpec(memory_space=pl.ANY)
    y_out, sa, sv = pl.pallas_call(
        functools.partial(_mix_ffn_kernel, tiles_per_seq=tps),
        grid=(t // tm,),
        in_specs=[row(D_MODEL), row(768), row(256), full(wm), full(wo), full(g), full(conv_w), full(conv_b),
                  hbm, hbm],
        out_specs=[row(D_MODEL), pl.BlockSpec((1, 8, D_FF), lambda j: (j, 0, 0)),
                   pl.BlockSpec((1, 8, D_FF), lambda j: (j, 0, 0))],
        out_shape=[jax.ShapeDtypeStruct((t, D_MODEL), F32), jax.ShapeDtypeStruct((t // tm, 8, D_FF), F32),
                   jax.ShapeDtypeStruct((t // tm, 8, D_FF), F32)],
        scratch_shapes=[pltpu.VMEM((8, 2 * D_FF), F32), pltpu.VMEM((2, tm + 8, max(b - a for a, b in zip(FF_SUB[:-1], FF_SUB[1:]))), F32),
                        pltpu.VMEM(w_up.shape, BF16), pltpu.VMEM(w_down.shape, BF16),
                        pltpu.SemaphoreType.DMA((2,))],
        compiler_params=_cparams("arbitrary"),
    )(y, mix, att, wm, wo, g, conv_w, conv_b, w_up, w_down)
    last = lambda s: s.reshape(nseq, tps, 8, D_FF)[:, tps - 1, 6:8]
    return y_out, jnp.concatenate([last(sa), last(sv)], axis=-1)


def _ffn_sample_kernel(y_ref, g_ref, wa_ref, wv_ref, cwa_ref, cwv_ref, cba_ref, cbv_ref, wd_ref,
                       p1a_ref, p1v_ref, p2a_ref, p2v_ref, o_ref, ua_ref, uv_ref, acc_ref, ext_ref):
    c = pl.program_id(0)
    h = _rms_rows(y_ref[...], g_ref[...]).astype(BF16)
    tm = h.shape[0]
    rowm = lax.broadcasted_iota(jnp.int32, (tm, wa_ref.shape[1]), 0) % 8
    zero8 = jnp.zeros((8, wa_ref.shape[1]), F32)
    ua, ya = _ffn_half(h, wa_ref, cwa_ref, cba_ref, ext_ref, zero8, (rowm, p1a_ref[...], p2a_ref[...]))
    uv, yv = _ffn_half(h, wv_ref, cwv_ref, cbv_ref, ext_ref, zero8, (rowm, p1v_ref[...], p2v_ref[...]))
    ua_ref[...] = ua
    uv_ref[...] = uv
    d = _dot((jax.nn.silu(ya) * yv).astype(BF16), wd_ref[...])

    @pl.when(c == 0)
    def _():
        acc_ref[...] = d

    @pl.when(c == pl.num_programs(0) - 1)
    def _():
        o_ref[...] = y_ref[...] + acc_ref[...] + d


def _ffn_sample(y, g, w_up, conv_w, conv_b, w_down, state):
    t = y.shape[0]
    nseq = t // 8
    fc = FF_CHUNK
    ncf = D_FF // fc
    z = jnp.zeros((nseq, 1, 2 * D_FF), F32)
    p1 = jnp.concatenate([state[:, 1:2]] + [z] * 7, axis=1).reshape(t, 2 * D_FF)
    p2 = jnp.concatenate([state[:, 0:1], state[:, 1:2]] + [z] * 6, axis=1).reshape(t, 2 * D_FF)
    full = lambda a: pl.BlockSpec(a.shape, lambda c: (0,) * a.ndim)
    ca = lambda r: pl.BlockSpec((r, fc), lambda c: (0, c))
    cv = lambda r: pl.BlockSpec((r, fc), lambda c: (0, ncf + c))
    y_out, ua, uv = pl.pallas_call(
        _ffn_sample_kernel,
        grid=(ncf,),
        in_specs=[full(y), full(g), ca(D_MODEL), cv(D_MODEL), ca(3), cv(3), ca(1), cv(1),
                  pl.BlockSpec((fc, D_MODEL), lambda c: (c, 0)), ca(t), cv(t), ca(t), cv(t)],
        out_specs=[pl.BlockSpec((t, D_MODEL), lambda c: (0, 0)), ca(t), ca(t)],
        out_shape=[jax.ShapeDtypeStruct((t, D_MODEL), F32), jax.ShapeDtypeStruct((t, D_FF), F32),
                   jax.ShapeDtypeStruct((t, D_FF), F32)],
        scratch_shapes=[pltpu.VMEM((t, D_MODEL), F32), pltpu.VMEM((t + 8, fc), F32)],
        compiler_params=_cparams("arbitrary"),
    )(y, g, w_up, w_up, conv_w, conv_w, conv_b, conv_b, w_down, p1, p1, p2, p2)
    u = jnp.concatenate([ua, uv], axis=-1).reshape(nseq, 8, 2 * D_FF)
    return y_out, u[:, 6:8]


KEY_CHUNK = 1024
NEW_PAD = 16


def _nsa_sample_kernel(pt_ref, qn_ref, gates_ref, rows_ref, winc_ref, winn_ref, scb_ref, sbt_ref, wbt_ref,
                       rmat_ref, eloc_ref, cmp_hbm, cache_hbm, o_ref, kvbuf, kcv, s_ref, m3_ref, sem, sem2):
    b = pl.program_id(0)
    npages = pt_ref.shape[1]
    ncs = 2 * npages
    nkeys = npages * PAGE_SIZE
    nch = nkeys // KEY_CHUNK
    ppc = KEY_CHUNK // PAGE_SIZE
    wbuf = winc_ref.shape[-1]

    slot = b % 2

    def page_copy(seq, p, sl):
        return pltpu.make_async_copy(cache_hbm.at[pt_ref[seq, p], pl.ds(2, 2)], kvbuf.at[sl, p], sem.at[sl])

    def cmp_copy(p):
        return pltpu.make_async_copy(cmp_hbm.at[pt_ref[b, p]], kcv.at[p], sem2)

    def start_cmp(p, c):
        cmp_copy(p).start()
        return c

    lax.fori_loop(0, npages, start_cmp, 0)

    @pl.when(b == 0)
    def _():
        def start_first(p, c):
            page_copy(0, p, 0).start()
            return c

        lax.fori_loop(0, npages, start_first, 0)

    @pl.when(b + 1 < pl.num_programs(0))
    def _():
        def start_next(p, c):
            page_copy(b + 1, p, 1 - slot).start()
            return c

        lax.fori_loop(0, npages, start_next, 0)

    q = qn_ref[0].astype(F32)
    gts = gates_ref[0]
    lane8 = lax.broadcasted_iota(jnp.int32, (8, LANES), 1)
    half = [lane8 < 64, lane8 >= 64]
    pieces = [jnp.where(half[g], q[:, h * LANES:(h + 1) * LANES], 0.0) for g in range(2) for h in range(NSA_GROUP)]
    qrows = jnp.concatenate(pieces + [jnp.zeros((32, LANES), F32)], axis=0).astype(BF16)

    def wait_cmp(p, c):
        cmp_copy(p).wait()
        return c

    lax.fori_loop(0, npages, wait_cmp, 0)

    kc = jnp.concatenate([kcv[:, 0, 0:128], kcv[:, 1, 0:128]], axis=0).astype(BF16)
    vc = jnp.concatenate([kcv[:, 0, 128:256], kcv[:, 1, 128:256]], axis=0).astype(BF16)
    lc = _nt(kc, qrows) + scb_ref[...]
    e = jnp.exp(lc - jnp.max(lc, axis=0, keepdims=True))
    p_c = e / jnp.sum(e, axis=0, keepdims=True)
    o_c = _tn(p_c.astype(BF16), vc)
    hi = p_c.astype(BF16)
    lo = (p_c - hi.astype(F32)).astype(BF16)
    imp = _dot(hi, rmat_ref[...]) + _dot(lo, rmat_ref[...])

    def blk_of(r):
        return jnp.where(r < npages, 2 * r, jnp.where(r < ncs, 2 * (r - npages) + 1, r))

    blk = blk_of(lax.broadcasted_iota(jnp.int32, (ncs + 8, LANES), 0))
    impx = jnp.concatenate([imp, jnp.zeros((8, LANES), F32)], axis=0)
    forced = (blk == 0) | (blk == ncs) | (blk == ncs - 1)
    score = jnp.where(blk <= ncs, jnp.where(forced, FORCED_SCORE, impx), -1.0)
    blkf = blk.astype(F32)
    work, picked = score, jnp.zeros((ncs + 8, LANES), F32)
    for _ in range(SLC_TOPN):
        top = jnp.max(work, axis=0, keepdims=True)
        first = jnp.min(jnp.where(work == top, blkf, float(2 * ncs + 16)), axis=0, keepdims=True)
        hit = blkf == first
        picked = jnp.where(hit, 1.0, picked)
        work = jnp.where(hit, -2.0, work)
    madd = jnp.where((picked > 0.0) & (score >= 0.0), 0.0, -BIG)
    eye = jnp.where(lax.broadcasted_iota(jnp.int32, (LANES, LANES), 0)
                    == lax.broadcasted_iota(jnp.int32, (LANES, LANES), 1), 1.0, 0.0).astype(BF16)
    hb = ppc
    for t in range(nch):
        mt = jnp.concatenate([madd[t * hb:(t + 1) * hb], madd[npages + t * hb:npages + (t + 1) * hb],
                              jnp.zeros((LANES - 2 * hb, LANES), F32)], axis=0).astype(BF16)
        m3_ref[t] = _nt(eye, mt).astype(BF16)

    def wait_page(p, c):
        page_copy(b, p, slot).wait()
        return c

    lax.fori_loop(0, npages, wait_page, 0)

    eloc = eloc_ref[...]
    pad8 = jnp.zeros((NEW_PAD - 8, LANES), F32)
    knew = jnp.concatenate([rows_ref[0, :, 256:384], pad8], axis=0).astype(BF16)
    vnew = jnp.concatenate([rows_ref[0, :, 384:512], pad8], axis=0).astype(BF16)
    kwn = jnp.concatenate([winn_ref[0, :, 0:128], pad8], axis=0).astype(BF16)
    vwn = jnp.concatenate([winn_ref[0, :, 128:256], pad8], axis=0).astype(BF16)
    kwt = winc_ref[0, 0].reshape(LANES, wbuf).astype(BF16)
    vwt = winc_ref[0, 1].reshape(LANES, wbuf).astype(BF16)
    near_pad = jnp.concatenate([jnp.zeros((LANES, KEY_CHUNK - PAGE_SIZE), F32), sbt_ref[:, 0:PAGE_SIZE]], axis=1)
    low = lax.broadcasted_iota(jnp.int32, (8, LANES), 1) < 64

    def page_rows(t, comp):
        tiles = [kvbuf[slot, t * ppc + pp, comp].reshape(LANES, PAGE_SIZE) for pp in range(ppc)]
        return jnp.concatenate(tiles, axis=1).astype(BF16)

    def s_body(t, m):
        lhs = jnp.concatenate([m3_ref[t], qrows], axis=1)
        s = _dot(lhs, jnp.concatenate([eloc, page_rows(t, 0)], axis=0))
        s = s + jnp.where(t == nch - 1, 1.0, 0.0) * near_pad
        s_ref[t] = s
        return jnp.maximum(m, jnp.max(s, axis=1, keepdims=True))

    s_new = _nt(qrows, knew) + sbt_ref[:, PAGE_SIZE:PAGE_SIZE + NEW_PAD]
    unroll = math.gcd(nch, 4)
    m = lax.fori_loop(0, nch, s_body, jnp.max(s_new, axis=1, keepdims=True), unroll=unroll)
    e_new = jnp.exp(s_new - m)

    def pv_body(t, carry):
        l, acc = carry
        et = jnp.exp(s_ref[t] - m)
        return l + jnp.sum(et, axis=1, keepdims=True), acc + _nt(et.astype(BF16), page_rows(t, 1))

    l, acc = lax.fori_loop(0, nch, pv_body, (jnp.sum(e_new, axis=1, keepdims=True),
                                             _dot(e_new.astype(BF16), vnew)), unroll=unroll)
    o_s = acc / l

    sw = _dot(qrows, kwt) + wbt_ref[:, 0:wbuf]
    swn = _nt(qrows, kwn) + wbt_ref[:, wbuf:wbuf + NEW_PAD]
    mw = jnp.maximum(jnp.max(sw, axis=1, keepdims=True), jnp.max(swn, axis=1, keepdims=True))
    ew = jnp.exp(sw - mw)
    ewn = jnp.exp(swn - mw)
    o_w = (_nt(ew.astype(BF16), vwt) + _dot(ewn.astype(BF16), vwn)) / (
        jnp.sum(ew, axis=1, keepdims=True) + jnp.sum(ewn, axis=1, keepdims=True))

    def gate_col(br):
        cols = [gts[:, hh * 3 + br:hh * 3 + br + 1] for hh in range(NSA_HEADS)]
        return jnp.concatenate(cols + [jnp.zeros((32, 1), F32)], axis=0)

    o = gate_col(0) * o_c + gate_col(1) * o_s + gate_col(2) * o_w
    for h in range(NSA_GROUP):
        r0, r1 = h * 8, (NSA_GROUP + h) * 8
        o_ref[0, :, h * LANES:(h + 1) * LANES] = jnp.where(low, o[r0:r0 + 8], o[r1:r1 + 8]).astype(BF16)


def _nsa_sample_attn(page_table, qn, gates, rows, winc, winn, scb, sbt, wbt, rmat, eloc, cmp_phys, cache):
    b, npages = page_table.shape
    ncs = 2 * npages
    nkeys = npages * PAGE_SIZE
    assert nkeys % KEY_CHUNK == 0
    nch = nkeys // KEY_CHUNK
    full = lambda a: pl.BlockSpec(a.shape, lambda bi, pt: (0,) * a.ndim)
    per_b = lambda a: pl.BlockSpec((1,) + a.shape[1:], lambda bi, pt: (bi,) + (0,) * (a.ndim - 1))
    grid_spec = pltpu.PrefetchScalarGridSpec(
        num_scalar_prefetch=1,
        grid=(b,),
        in_specs=[per_b(qn), per_b(gates), per_b(rows), per_b(winc), per_b(winn),
                  full(scb), full(sbt), full(wbt), full(rmat), full(eloc),
                  pl.BlockSpec(memory_space=pl.ANY), pl.BlockSpec(memory_space=pl.ANY)],
        out_specs=pl.BlockSpec((1, 8, 768), lambda bi, pt: (bi, 0, 0)),
        scratch_shapes=[pltpu.VMEM((2, npages, 2, 2, 64, PAGE_SIZE), F32), pltpu.VMEM((npages, 2, 256), F32),
                        pltpu.VMEM((nch, LANES, KEY_CHUNK), F32), pltpu.VMEM((nch, LANES, LANES), BF16),
                        pltpu.SemaphoreType.DMA((2,)), pltpu.SemaphoreType.DMA(())],
    )
    return pl.pallas_call(
        _nsa_sample_kernel,
        grid_spec=grid_spec,
        out_shape=jax.ShapeDtypeStruct((b, 8, 768), BF16),
        compiler_params=_cparams("arbitrary"),
    )(page_table, qn, gates, rows, winc, winn, scb, sbt, wbt, rmat, eloc, cmp_phys, cache)


def _block_diag2(w):
    z = jnp.zeros_like(w)
    return jnp.concatenate([jnp.concatenate([w, z], axis=-1), jnp.concatenate([z, w], axis=-1)], axis=-2)


def _q_slot_perm():
    return np.concatenate([np.r_[h * 64:(h + 1) * 64, (NSA_GROUP + h) * 64:(NSA_GROUP + h + 1) * 64]
                           for h in range(NSA_GROUP)])


def _sample_codes(ncs, wbuf):
    ql = np.arange(8)[None, :]
    c = np.arange(ncs)[:, None]
    scb = _bucket_np((ncs - c) * CMP_BLOCK + ql - (CMP_BLOCK - 1))
    kap = np.arange(PAGE_SIZE + NEW_PAD)[:, None]
    d = np.where(kap < PAGE_SIZE, PAGE_SIZE + ql - kap, ql - (kap - PAGE_SIZE))
    sbt = np.where((d >= 0) & (kap < PAGE_SIZE + 8), _bucket_np(d), -1)
    j = np.arange(wbuf + NEW_PAD)[:, None]
    d = wbuf + ql - j
    wbt = np.where((d >= 0) & (d <= WINDOW) & (j < wbuf + 8), _bucket_np(d), -1)
    return np.concatenate([scb, sbt, wbt], axis=0).astype(np.int32)


def kernel(x_prompt, x_sample, cache_nsa_kv, cache_nsa_win, cache_mem_kv, state_pool, state_ffn_conv,
           page_table, mem_prompt, rel_bias, g_mix, g_mem, g_ffn, w_in_nsa, q_gain_nsa, k_gain_nsa,
           cmp_pos, cmp_w1, cmp_b1, cmp_w2, cmp_b2, w_in_pool, w_pool_grp, pool_scale, w_mem_kv,
           xq_gain, xk_gain, w_out, w_up, conv_w, conv_b, w_down):
    bp, lp, _ = x_prompt.shape
    bs, ls, _ = x_sample.shape
    assert ls == 8 and lp % 512 == 0 and lp // 64 <= 64
    npages = page_table.shape[1]
    past_len = npages * PAGE_SIZE
    n_phys = cache_nsa_kv.shape[1]
    wbuf = cache_nsa_win.shape[2]
    nm = mem_prompt.shape[1]
    tp_, ts_ = bp * lp, bs * ls
    nc = lp // 64
    ncs = 2 * npages
    scale = HEAD_DIM ** -0.5
    tile2 = lambda v: jnp.tile(v, 2)[None, :]

    a = np.arange(LANES)
    seg = jnp.asarray(a[:, None] // 64 == a[None, :] // 64, BF16)
    perm = _q_slot_perm()

    wb = _bias_table(rel_bias, _window_codes()).reshape(NSA_HEADS * 64, WIN_KEYS)
    nbias = jnp.concatenate([jnp.full((NSA_HEADS * 64, 64), -BIG, F32), wb[:, WIN_KEYS - 192:]], axis=1)
    ct = _bias_table(rel_bias, _cmp_codes()).transpose(1, 0, 2).reshape(8, NSA_HEADS * 64)
    amat = np.zeros((2, NSA_GROUP * 64, LANES), np.float32)
    for g in range(2):
        for h in range(NSA_GROUP):
            amat[g, h * 64 + np.arange(64), g * 64 + np.arange(64)] = 1.0
    amat = jnp.asarray(amat.reshape(NSA_HEADS * 64, LANES), BF16)
    r = np.arange(PAD_ROWS + lp)
    oh_np = np.zeros((PAD_ROWS + lp, LANES), np.float32)
    oh_np[r, np.where(r < PAD_ROWS, 64, (r - PAD_ROWS) // 64)] = 1.0
    oh = jnp.asarray(oh_np, BF16)
    stab = _bias_table(rel_bias, _sample_codes(ncs, wbuf))
    stab = jnp.pad(stab.transpose(1, 0, 2).reshape(-1, NSA_HEADS * 8), ((0, 0), (0, LANES - NSA_HEADS * 8)))
    ns = PAGE_SIZE + NEW_PAD
    scb, sbt, wbt = stab[:ncs], stab[ncs:ncs + ns].T, stab[ncs + ns:].T
    scb = jnp.concatenate([scb[0::2], scb[1::2]], axis=0)
    ppc = KEY_CHUNK // PAGE_SIZE
    eloc_np = np.zeros((LANES, KEY_CHUNK), np.float32)
    for pp in range(ppc):
        eloc_np[pp, pp * PAGE_SIZE:pp * PAGE_SIZE + 64] = 1.0
        eloc_np[ppc + pp, pp * PAGE_SIZE + 64:(pp + 1) * PAGE_SIZE] = 1.0
    eloc = jnp.asarray(eloc_np, BF16)
    lam = np.arange(LANES)
    rm = (lam[:, None] < 96) & (lam[None, :] < 96) & (lam[:, None] // 48 == lam[None, :] // 48) \
        & (lam[:, None] % 8 == lam[None, :] % 8)
    rmat = jnp.asarray(rm, BF16)

    yp = x_prompt.reshape(tp_, D_MODEL)
    ys = x_sample.reshape(ts_, D_MODEL)
    tm = 512
    outs = {k: [] for k in ("kv_p", "kv_s", "win_p", "win_s", "mem_p", "pool_p", "pool_s", "conv_p", "conv_s")}
    depth = g_mix.shape[0]
    for i in range(depth):
        j = i // 2
        wo = w_out[i]
        xg = tile2(xq_gain[i]) * scale
        mkv_p = _memkv(mem_prompt.reshape(bp * nm, D_MODEL), g_mem[i][None], w_mem_kv[i].astype(BF16), seg,
                       tile2(xk_gain[i]), min(tm, bp * nm)).reshape(bp, nm, 2 * XATTN_WIDTH)
        outs["mem_p"].append(mkv_p.reshape(bp, nm, 2, XATTN_HEADS, HEAD_DIM))
        mkv_s = cache_mem_kv[i].reshape(bs, nm, 2 * XATTN_WIDTH)
        if i % 2 == 0:
            w = w_in_nsa[j]
            w = jnp.concatenate([w[:, perm], w[:, 768:1536], w[:, 1572:1828], w[:, 1536:1572],
                                 jnp.zeros((D_MODEL, NSA_IN_COLS - 1828), F32)], axis=1).astype(BF16)
            qg = tile2(q_gain_nsa[j]) * scale
            kg = tile2(k_gain_nsa[j])
            wo_mix = wo[:768][perm].astype(BF16)
            w2 = cmp_w2[j]
            w2sel = jnp.zeros((2, 2, 2, 2 * CMP_HIDDEN, LANES), F32)
            for blk in range(2):
                for g in range(2):
                    w2sel = w2sel.at[:, blk, g, blk * CMP_HIDDEN:(blk + 1) * CMP_HIDDEN,
                                     g * HEAD_DIM:(g + 1) * HEAD_DIM].set(w2)
            cc_args = (jnp.tile(cmp_pos[j].transpose(0, 2, 1), (1, 1, 2))[:, :, None, :],
                       _block_diag2(cmp_w1[j].transpose(0, 2, 1, 3)).astype(BF16).reshape(2, 32, 2 * LANES, 256),
                       jnp.tile(cmp_b1[j], (1, 2))[:, None, :], w2sel.astype(BF16),
                       jnp.tile(cmp_b2[j], (1, 2))[:, None, :], kg, seg, 512)
            wt = w_in_nsa[j][:, 768:1536].T.astype(BF16)
            qn, rows_t, win_t, katt, xqn_p, gates = _nsa_inproj_t(
                yp, g_mix[i][None], w, wt, seg, qg, kg, jnp.tile(k_gain_nsa[j], 2)[:, None], xg, tm, lp)
            pages_t = rows_t[:, :256].reshape(bp, 2, 2, HEAD_DIM, lp // PAGE_SIZE, PAGE_SIZE)
            pages_t = pages_t.transpose(0, 4, 1, 2, 3, 5).reshape(bp * lp // PAGE_SIZE, 2, 2, HEAD_DIM, PAGE_SIZE)
            cmp_p = _compress_cache(pages_t, *cc_args).reshape(bp, nc, 256)
            kvatt = jnp.pad(katt.reshape(bp, lp, 512), ((0, 0), (PAD_ROWS, 0), (0, 0)))
            mix_p = _nsa_prompt_attn(qn.reshape(bp, lp, 768), gates.reshape(bp, lp, 128), cmp_p, kvatt,
                                     oh, wb, nbias, ct, amat).reshape(tp_, 768)
            outs["kv_p"].append(rows_t.reshape(bp, 4, NSA_KV_HEADS, HEAD_DIM, lp).transpose(0, 4, 1, 2, 3))
            wlen = min(WINDOW, lp)
            outs["win_p"].append(win_t.reshape(bp, 2, NSA_KV_HEADS, HEAD_DIM, wlen).transpose(0, 4, 1, 2, 3))
            qn, rows, win, _, xqn_s, gates = _nsa_inproj(ys, g_mix[i][None], w, seg, qg, kg, xg, ts_)
            cache_t = cache_nsa_kv[j].transpose(0, 2, 3, 4, 1)
            cmp_phys = _compress_cache(cache_t, *cc_args)
            winc = cache_nsa_win[j].transpose(0, 2, 3, 4, 1)
            mix_s = _nsa_sample_attn(page_table, qn.reshape(bs, ls, 768), gates.reshape(bs, ls, 128),
                                     rows.reshape(bs, ls, 512), winc, win.reshape(bs, ls, 256),
                                     scb, sbt, wbt, rmat, eloc, cmp_phys.reshape(n_phys, 2, 256),
                                     cache_t).reshape(ts_, 768)
            outs["kv_s"].append(rows.reshape(bs, ls, 4, NSA_KV_HEADS, HEAD_DIM))
            wall = jnp.concatenate([cache_nsa_win[j], win.reshape(bs, ls, 2, NSA_KV_HEADS, HEAD_DIM)], axis=1)
            outs["win_s"].append(wall[:, -wbuf:])
        else:
            w = w_in_pool[j].astype(BF16)
            wbd = jnp.zeros((POOL_WIDTH, POOL_WIDTH), F32)
            for gi in range(len(POOL_WINDOWS)):
                sl = slice(gi * POOL_GROUP_DIM, (gi + 1) * POOL_GROUP_DIM)
                wbd = wbd.at[sl, sl].set(w_pool_grp[j, gi])
            wbd = wbd.astype(BF16)
            psc = pool_scale[j][None]
            wo_mix = wo[:768].astype(BF16)
            mix_p, xqn_p, st_p = _pool_layer(yp, g_mix[i][None], w, seg, xg, jnp.zeros((bp, 16, POOL_WIDTH), F32),
                                             wbd, psc, tm, lp, 0)
            outs["pool_p"].append(st_p)
            prev16 = jnp.concatenate([jnp.zeros((bs, 1, POOL_WIDTH), F32), state_pool[j]], axis=1)
            mix_s, xqn_s, st_s = _pool_layer(ys, g_mix[i][None], w, seg, xg, prev16, wbd, psc, ls, ls, past_len)
            outs["pool_s"].append(st_s)
        wo_att = wo[768:].astype(BF16)
        att_p = _mem_attn(xqn_p.reshape(bp, lp, 256), mkv_p, tm).reshape(tp_, 256)
        att_s = _mem_attn(xqn_s.reshape(bs, ls, 256), mkv_s, ls).reshape(ts_, 256)
        wu, wd = w_up[i].astype(BF16), w_down[i].astype(BF16)
        yp, c_p = _mix_ffn_prompt(yp, mix_p, att_p, wo_mix, wo_att, g_ffn[i][None], wu, conv_w[i],
                                  conv_b[i][None], wd, tm, lp)
        ys, _ = _outproj(ys, mix_s, att_s, wo_mix, wo_att, g_ffn[i][None], ts_)
        ys, c_s = _ffn_sample(ys, g_ffn[i][None], wu, conv_w[i], conv_b[i][None], wd, state_ffn_conv[i])
        outs["conv_p"].append(c_p)
        outs["conv_s"].append(c_s)
    st = lambda k: jnp.stack(outs[k])
    return (yp.reshape(bp, lp, D_MODEL), ys.reshape(bs, ls, D_MODEL), st("kv_p"), st("kv_s"), st("win_p"),
            st("win_s"), st("mem_p"), st("pool_p"), st("pool_s"), st("conv_p"), st("conv_s"))
```
